```python
import jax
import jax.numpy as jnp
from jax import lax
import numpy as np

D_MODEL = 2048
BATCH = 2
SEQ = 16384
DEPTH = 2
DEC_BATCH = 4
DEC_SEQ = 8192
PAST_LEN = 128

RW_WIDTH = D_MODEL // 2
RW_HEAD = 64
RW_HEADS = RW_WIDTH // RW_HEAD
RW_DECAY_LORA = 64
RW_A_LORA = 64
RW_G_LORA = 160
RW_GN_EPS = 64e-5
ML_HEADS = 4
ML_QK = 128
ML_V = 256
ML_QK_WIDTH = ML_HEADS * ML_QK
ML_V_WIDTH = ML_HEADS * ML_V
ML_CHUNK = 128
ML_NORM_EPS = 1e-6
PEER_HEADS = 8
PEER_NKEYS = 128
PEER_N = PEER_NKEYS * PEER_NKEYS
PEER_QDIM = 256
PEER_HALF = PEER_QDIM // 2
PEER_TOPK = 16
PEER_BLOCK = 128
RMS_EPS = 1e-6
RW_COLS = 3 * RW_WIDTH + RW_DECAY_LORA + RW_A_LORA + RW_G_LORA
ML_COLS = 2 * ML_QK_WIDTH + 2 * ML_V_WIDTH + 4 * ML_HEADS
GATE_COLS = 2 * D_MODEL
IN_COLS = RW_COLS + ML_COLS + GATE_COLS

kernel_name = 'hybrid_rwkv7_mlstm_peer_encoder'


def rmsnorm(x, g):
    xf = x.astype(jnp.float32)
    y = xf * lax.rsqrt(jnp.mean(xf * xf, axis=-1, keepdims=True) + RMS_EPS) * g.astype(jnp.float32)
    return y.astype(x.dtype)


def centred_shift(p, mu_prev, mu_next):
    zero = jnp.zeros_like(p[:, :1])
    prev = jnp.concatenate([zero, p[:, :-1]], axis=1)
    nxt = jnp.concatenate([p[:, 1:], zero], axis=1)
    return p + mu_prev * (prev - p) + mu_next * (nxt - p)


def rwkv7_scan(r, w, k, v, a, b):
    B, T, H, N = r.shape

    def step(S, xs):
        r_t, w_t, k_t, v_t, a_t, b_t = xs
        sa = jnp.einsum('bhvk,bhk->bhv', S, a_t)
        S = S * w_t[:, :, None, :] + sa[..., None] * b_t[:, :, None, :] + v_t[..., None] * k_t[:, :, None, :]
        return S, jnp.einsum('bhvk,bhk->bhv', S, r_t)

    xs = tuple(jnp.moveaxis(t, 1, 0) for t in (r, w, k, v, a, b))
    _, y = lax.scan(step, jnp.zeros((B, H, N, N), jnp.float32), xs)
    return jnp.moveaxis(y, 0, 1)


def rwkv7_branch(p, mu_prev, mu_next, w0, w2, a0, a2, g2, k_k, k_a, r_k, ln_w, ln_b):
    B, T, _ = p.shape
    W = RW_WIDTH
    p = centred_shift(p, mu_prev, mu_next).astype(jnp.float32)
    r, k, v, wd, ad, gd = jnp.split(p, [W, 2 * W, 3 * W, 3 * W + RW_DECAY_LORA, 3 * W + RW_DECAY_LORA + RW_A_LORA], axis=-1)
    a = jax.nn.sigmoid(a0 + ad @ a2)
    g = jax.nn.sigmoid(gd) @ g2
    kk = (k * k_k).reshape(B, T, RW_HEADS, RW_HEAD)
    kk = kk / jnp.maximum(jnp.sqrt(jnp.sum(kk * kk, axis=-1, keepdims=True)), 1e-12)
    k = k * (1.0 + (a - 1.0) * k_a)
    wlow = jnp.tanh(wd)

    def decay(d):
        lw = -jax.nn.softplus(-(w0[d] + wlow @ w2[d])) - 0.5
        return jnp.exp(-jnp.exp(lw)).reshape(B, T, RW_HEADS, RW_HEAD)

    hs = lambda t: t.reshape(B, T, RW_HEADS, RW_HEAD)
    r, k, v, a = hs(r), hs(k), hs(v), hs(a)
    ia, ib = -kk, kk * a
    fl = lambda t: jnp.flip(t, axis=1)
    y = rwkv7_scan(r, decay(0), k, v, ia, ib) + fl(rwkv7_scan(fl(r), fl(decay(1)), fl(k), fl(v), fl(ia), fl(ib)))
    mu = jnp.mean(y, axis=-1, keepdims=True)
    var = jnp.mean(jnp.square(y - mu), axis=-1, keepdims=True)
    y = ((y - mu) * lax.rsqrt(var + RW_GN_EPS)).reshape(B, T, W) * ln_w + ln_b
    bonus = (jnp.sum(r * k * r_k, axis=-1, keepdims=True) * v).reshape(B, T, W)
    return (y + bonus) * g


def mlstm_scan(q, k, v, ig, lf):
    B, H, T, DK = q.shape
    DV = v.shape[-1]
    L = ML_CHUNK
    NC = T // L
    q = q * (DK ** -0.5)
    ch = lambda t: jnp.moveaxis(t.reshape((B, H, NC, L) + t.shape[3:]), 2, 0)
    mask = jnp.tril(jnp.ones((L, L), dtype=bool))

    def step(carry, xs):
        C, n, m = carry
        qc, kc, vc, ic, fc = xs
        b = jnp.cumsum(fc, axis=-1)
        dmat = jnp.where(mask, b[..., :, None] - b[..., None, :] + ic[..., None, :], -jnp.inf)
        inter = b + m[..., None]
        m_t = jnp.maximum(inter, jnp.max(dmat, axis=-1))
        s = jnp.einsum('bhtd,bhsd->bhts', qc, kc) * jnp.exp(dmat - m_t[..., None])
        e_inter = jnp.exp(inter - m_t)
        num = jnp.einsum('bhts,bhsv->bhtv', s, vc) + e_inter[..., None] * jnp.einsum('bhvd,bhtd->bhtv', C, qc)
        den = jnp.sum(s, axis=-1) + e_inter * jnp.einsum('bhd,bhtd->bht', n, qc)
        h = num / jnp.maximum(jnp.abs(den), jnp.exp(-m_t))[..., None]
        b_last = b[..., -1]
        g_s = b_last[..., None] - b + ic
        m_new = jnp.maximum(b_last + m, jnp.max(g_s, axis=-1))
        w_s = jnp.exp(g_s - m_new[..., None])
        dec = jnp.exp(b_last + m - m_new)
        C = dec[..., None, None] * C + jnp.einsum('bhsv,bhsd->bhvd', vc * w_s[..., None], kc)
        n = dec[..., None] * n + jnp.einsum('bhs,bhsd->bhd', w_s, kc)
        return (C, n, m_new), h

    init = (jnp.zeros((B, H, DV, DK), jnp.float32), jnp.zeros((B, H, DK), jnp.float32), jnp.zeros((B, H), jnp.float32))
    _, h = lax.scan(step, init, (ch(q), ch(k), ch(v), ch(ig), ch(lf)))
    return jnp.moveaxis(h, 0, 2).reshape(B, H, T, DV)


def mlstm_branch(p, b_i, b_f, norm_w):
    B, T, _ = p.shape
    QW, VW, H = ML_QK_WIDTH, ML_V_WIDTH, ML_HEADS
    p = p.astype(jnp.float32)
    q, k, v, o, ig, fg = jnp.split(p, [QW, 2 * QW, 2 * QW + VW, 2 * QW + 2 * VW, 2 * QW + 2 * VW + 2 * H], axis=-1)
    hd = lambda t, d: t.reshape(B, T, H, d).transpose(0, 2, 1, 3)
    q, k, v = hd(q, ML_QK), hd(k, ML_QK), hd(v, ML_V)
    ig = (ig.reshape(B, T, 2, H) + b_i).transpose(2, 0, 3, 1)
    lf = jax.nn.log_sigmoid(fg.reshape(B, T, 2, H) + b_f).transpose(2, 0, 3, 1)
    fl = lambda t: jnp.flip(t, axis=2)
    h = mlstm_scan(q, k, v, ig[0], lf[0]) + fl(mlstm_scan(fl(q), fl(k), fl(v), fl(ig[1]), fl(lf[1])))
    h = h * lax.rsqrt(jnp.mean(h * h, axis=-1, keepdims=True) + ML_NORM_EPS)
    h = h.transpose(0, 2, 1, 3).reshape(B, T, VW) * norm_w
    return h * jax.nn.sigmoid(o)


def peer(h, wq, keys, u_tab, v_tab):
    B, T, D = h.shape
    hb = h.reshape(-1, PEER_BLOCK, D)
    K = PEER_TOPK

    def block(xb):
        q = (xb @ wq).astype(jnp.float32).reshape(PEER_BLOCK, PEER_HEADS, 2, PEER_HALF)
        s = jnp.einsum('thpc,hpnc->thpn', q, keys.astype(jnp.float32))
        sv, si = lax.top_k(s, K)
        cand = (sv[:, :, 0, :, None] + sv[:, :, 1, None, :]).reshape(PEER_BLOCK, PEER_HEADS, K * K)
        cidx = (si[:, :, 0, :, None] * PEER_NKEYS + si[:, :, 1, None, :]).reshape(PEER_BLOCK, PEER_HEADS, K * K)
        best, pos = lax.top_k(cand, K)
        idx = jnp.take_along_axis(cidx, pos, axis=-1)
        gw = jax.nn.softmax(best, axis=-1)
        ue = jnp.take(u_tab, idx, axis=0)
        act = jax.nn.gelu(jnp.einsum('td,thkd->thk', xb, ue).astype(jnp.float32), approximate=False) * gw
        ve = jnp.take(v_tab, idx, axis=0)
        return jnp.einsum('thk,thkd->td', act.astype(xb.dtype), ve)

    return lax.map(block, hb).reshape(B, T, D).astype(h.dtype)


def trunk(x, norm_mix, w_in, rw_mu_prev, rw_mu_next, rw_w0, rw_w2, rw_a0, rw_a2, rw_g2, rw_k_k, rw_k_a,
          rw_r_k, rw_ln_w, rw_ln_b, ml_b_i, ml_b_f, ml_norm_w, p_a, p_b, b_gate, w_out, norm_ffn,
          peer_wq, peer_keys, peer_u, peer_v, norm_final):
    for l in range(DEPTH):
        h = rmsnorm(x, norm_mix[l])
        proj = h @ w_in[l]
        p_rw, p_ml, p_gt = jnp.split(proj, [RW_COLS, RW_COLS + ML_COLS], axis=-1)
        y_a = rwkv7_branch(p_rw, rw_mu_prev[l], rw_mu_next[l], rw_w0[l], rw_w2[l], rw_a0[l], rw_a2[l], rw_g2[l],
                           rw_k_k[l], rw_k_a[l], rw_r_k[l], rw_ln_w[l], rw_ln_b[l]).astype(x.dtype) @ p_a[l]
        y_b = mlstm_branch(p_ml, ml_b_i[l], ml_b_f[l], ml_norm_w[l]).astype(x.dtype) @ p_b[l]
        gates = jax.nn.sigmoid(p_gt.astype(jnp.float32) + b_gate[l])
        merged = gates[..., :D_MODEL] * y_a + gates[..., D_MODEL:] * y_b
        x = x + merged.astype(x.dtype) @ w_out[l]
        x = x + peer(rmsnorm(x, norm_ffn[l]), peer_wq[l], peer_keys[l], peer_u[l], peer_v[l])
    return rmsnorm(x, norm_final)


def setup_inputs(seed: int = 0) -> dict:
    key = jax.random.key(seed)
    ks = jax.random.split(key, 32)
    f32 = jnp.float32
    nrm = lambda k, shape, s: jax.random.normal(k, shape, f32) * s
    L = DEPTH
    return {
        'x_prompt': nrm(ks[0], (BATCH, SEQ, D_MODEL), 1.0),
        'x_sample': nrm(ks[1], (DEC_BATCH, DEC_SEQ, D_MODEL), 1.0),
        'norm_mix': 1.0 + nrm(ks[2], (L, D_MODEL), 0.1),
        'w_in': nrm(ks[3], (L, D_MODEL, IN_COLS), D_MODEL ** -0.5),
        'rw_mu_prev': 0.3 + nrm(ks[4], (L, RW_COLS), 0.1),
        'rw_mu_next': 0.3 + nrm(ks[5], (L, RW_COLS), 0.1),
        'rw_w0': -0.5 + nrm(ks[6], (L, 2, RW_WIDTH), 0.5),
        'rw_w2': nrm(ks[7], (L, 2, RW_DECAY_LORA, RW_WIDTH), 0.3 * RW_DECAY_LORA ** -0.5),
        'rw_a0': nrm(ks[8], (L, RW_WIDTH), 0.1),
        'rw_a2': nrm(ks[9], (L, RW_A_LORA, RW_WIDTH), 0.3 * RW_A_LORA ** -0.5),
        'rw_g2': nrm(ks[10], (L, RW_G_LORA, RW_WIDTH), RW_G_LORA ** -0.5),
        'rw_k_k': 0.85 + nrm(ks[11], (L, RW_WIDTH), 0.1),
        'rw_k_a': 1.0 + nrm(ks[12], (L, RW_WIDTH), 0.1),
        'rw_r_k': nrm(ks[13], (L, RW_HEADS, RW_HEAD), 0.1),
        'rw_ln_w': 1.0 + nrm(ks[14], (L, RW_WIDTH), 0.1),
        'rw_ln_b': nrm(ks[15], (L, RW_WIDTH), 0.01),
        'ml_b_i': -1.0 + nrm(ks[16], (L, 2, ML_HEADS), 0.1),
        'ml_b_f': 3.0 + nrm(ks[17], (L, 2, ML_HEADS), 0.5),
        'ml_norm_w': 1.0 + nrm(ks[18], (L, ML_V_WIDTH), 0.1),
        'p_a': nrm(ks[19], (L, RW_WIDTH, D_MODEL), RW_WIDTH ** -0.5),
        'p_b': nrm(ks[20], (L, ML_V_WIDTH, D_MODEL), ML_V_WIDTH ** -0.5),
        'b_gate': nrm(ks[21], (L, GATE_COLS), 0.01),
        'w_out': nrm(ks[22], (L, D_MODEL, D_MODEL), D_MODEL ** -0.5),
        'norm_ffn': 1.0 + nrm(ks[23], (L, D_MODEL), 0.1),
        'peer_wq': nrm(ks[24], (L, D_MODEL, PEER_HEADS * PEER_QDIM), D_MODEL ** -0.5),
        'peer_keys': nrm(ks[25], (L, PEER_HEADS, 2, PEER_NKEYS, PEER_HALF), PEER_HALF ** -0.5),
        'peer_u': nrm(ks[26], (L, PEER_N, D_MODEL), D_MODEL ** -0.5),
        'peer_v': nrm(ks[27], (L, PEER_N, D_MODEL), PEER_TOPK ** -0.5),
        'norm_final': 1.0 + nrm(ks[28], (D_MODEL,), 0.1),
    }


def reference(x_prompt, x_sample, norm_mix, w_in, rw_mu_prev, rw_mu_next, rw_w0, rw_w2, rw_a0, rw_a2, rw_g2,
              rw_k_k, rw_k_a, rw_r_k, rw_ln_w, rw_ln_b, ml_b_i, ml_b_f, ml_norm_w, p_a, p_b, b_gate, w_out,
              norm_ffn, peer_wq, peer_keys, peer_u, peer_v, norm_final):
    y_prompt = trunk(x_prompt, norm_mix, w_in, rw_mu_prev, rw_mu_next, rw_w0, rw_w2, rw_a0, rw_a2, rw_g2,
                     rw_k_k, rw_k_a, rw_r_k, rw_ln_w, rw_ln_b, ml_b_i, ml_b_f, ml_norm_w, p_a, p_b, b_gate,
                     w_out, norm_ffn, peer_wq, peer_keys, peer_u, peer_v, norm_final)
    y_sample = trunk(x_sample, norm_mix, w_in, rw_mu_prev, rw_mu_next, rw_w0, rw_w2, rw_a0, rw_a2, rw_g2,
                     rw_k_k, rw_k_a, rw_r_k, rw_ln_w, rw_ln_b, ml_b_i, ml_b_f, ml_norm_w, p_a, p_b, b_gate,
                     w_out, norm_ffn, peer_wq, peer_keys, peer_u, peer_v, norm_final)
    return (y_prompt, y_sample)
```

```python
import functools
import math

import jax
import jax.numpy as jnp
from jax import lax
from jax.experimental import pallas as pl
from jax.experimental.pallas import tpu as pltpu

F32 = jnp.float32
BF16 = jnp.bfloat16

D_MODEL = 2048
RMS_EPS = 1e-6
RW_W = 1024
RW_HEAD = 64
RW_HEADS = 16
RW_DECAY_LORA = 64
RW_A_LORA = 64
RW_G_LORA = 160
RW_GN_EPS = 64e-5
RW_COLS = 3 * RW_W + RW_DECAY_LORA + RW_A_LORA + RW_G_LORA
RW_CHUNK = 64
ML_HEADS = 4
ML_QK = 128
ML_V = 256
ML_QKW = ML_HEADS * ML_QK
ML_VW = ML_HEADS * ML_V
ML_CHUNK = 128
ML_NORM_EPS = 1e-6
ML_COLS = 2 * ML_QKW + 2 * ML_VW + 4 * ML_HEADS
PEER_HEADS = 8
PEER_NKEYS = 128
PEER_N = PEER_NKEYS * PEER_NKEYS
PEER_HALF = 128
PEER_TOPK = 16
PEER_HK = PEER_HEADS * PEER_TOPK
LANES = 128
SUBLANES = 8

NN = ((1,), (0,))
NT = ((1,), (1,))


def _mm(a, b, dims=NN):
    return lax.dot_general(a, b, (dims, ((), ())), preferred_element_type=F32)


def _split2(a):
    hi = a.astype(BF16)
    lo = (a - hi.astype(F32)).astype(BF16)
    return hi, lo


def _split3(a):
    hi = a.astype(BF16)
    r1 = a - hi.astype(F32)
    mid = r1.astype(BF16)
    lo = (r1 - mid.astype(F32)).astype(BF16)
    return hi, mid, lo


def _dot1(a, b, dims=NN):
    return _mm(a.astype(BF16), b.astype(BF16), dims)


def _dot3(a, b, dims=NN):
    ah, al = _split2(a)
    bh, bl = _split2(b)
    return _mm(ah, bh, dims) + (_mm(ah, bl, dims) + _mm(al, bh, dims))


def _dot_sel(sel_bf16, x, dims=NN):
    h, m, l = _split3(x)
    return _mm(sel_bf16, h, dims) + (_mm(sel_bf16, m, dims) + _mm(sel_bf16, l, dims))


def _dot_xsel(x, sel_bf16, dims=NN):
    h, m, l = _split3(x)
    return _mm(h, sel_bf16, dims) + (_mm(m, sel_bf16, dims) + _mm(l, sel_bf16, dims))


def _params(sem, vmem_mb):
    return pltpu.CompilerParams(dimension_semantics=sem, vmem_limit_bytes=vmem_mb << 20)


def _sigmoid(x):
    return 1.0 / (1.0 + jnp.exp(-x))


def _norm_mm_body(x_ref, g_ref, *rest, n_parts):
    w_refs = rest[:n_parts]
    o_ref = rest[n_parts]
    h_ref = rest[n_parts + 1]

    @pl.when(pl.program_id(1) == 0)
    def _():
        xf = x_ref[...]
        h = xf * lax.rsqrt(jnp.mean(xf * xf, axis=-1, keepdims=True) + RMS_EPS) * g_ref[...]
        hi = h.astype(BF16)
        h_ref[0] = hi
        if n_parts == 2:
            h_ref[1] = (h - hi.astype(F32)).astype(BF16)

    acc = _mm(h_ref[0], w_refs[0][...])
    if n_parts == 2:
        acc = acc + (_mm(h_ref[0], w_refs[1][...]) + _mm(h_ref[1], w_refs[0][...]))
    o_ref[...] = acc.astype(o_ref.dtype)


def _norm_mm(x, g, w_parts, out_dtype=F32):
    n, d = x.shape
    c = w_parts[0].shape[1]
    tm = min(512, n)
    tn = min(512, c)
    n_parts = len(w_parts)
    return pl.pallas_call(
        functools.partial(_norm_mm_body, n_parts=n_parts),
        grid=(n // tm, c // tn),
        in_specs=[pl.BlockSpec((tm, d), lambda i, j: (i, 0)),
                  pl.BlockSpec((1, d), lambda i, j: (0, 0))]
        + [pl.BlockSpec((d, tn), lambda i, j: (0, j))] * n_parts,
        out_specs=pl.BlockSpec((tm, tn), lambda i, j: (i, j)),
        out_shape=jax.ShapeDtypeStruct((n, c), out_dtype),
        scratch_shapes=[pltpu.VMEM((n_parts, tm, d), BF16)],
        compiler_params=_params(("parallel", "arbitrary"), 40),
    )(x, g, *w_parts)


def _shift_rows(p, prev_row, next_row, mu_prev, mu_next):
    n = p.shape[0]
    rid = lax.broadcasted_iota(jnp.int32, p.shape, 0)
    prev = jnp.where(rid == 0, prev_row, pltpu.roll(p, 1, 0))
    nxt = jnp.where(rid == n - 1, next_row, pltpu.roll(p, n - 1, 0))
    return p + mu_prev * (prev - p) + mu_next * (nxt - p)


def _rw_prep_body(m_ref, mp_ref, mn_ref, l_ref, lp_ref, ln_ref, mum_ref, mul_ref, w0_ref, w2_ref,
                  a0_ref, a2_ref, g2_ref, kk_ref, ka_ref, rk_ref, e_ref, et_ref,
                  r_o, k_o, v_o, kn_o, ab_o, lw0_o, lw1_o, g_o, bo_o):
    i = pl.program_id(1)
    first = i == 0
    last = i == pl.num_programs(1) - 1
    zero_m = jnp.zeros((1, m_ref.shape[2]), F32)
    zero_l = jnp.zeros((1, l_ref.shape[2]), F32)
    pm = _shift_rows(m_ref[0],
                     jnp.where(first, zero_m, mp_ref[0, SUBLANES - 1:SUBLANES, :]),
                     jnp.where(last, zero_m, mn_ref[0, 0:1, :]),
                     mum_ref[0:1, :], mum_ref[1:2, :])
    plo = _shift_rows(l_ref[0],
                      jnp.where(first, zero_l, lp_ref[0, SUBLANES - 1:SUBLANES, :]),
                      jnp.where(last, zero_l, ln_ref[0, 0:1, :]),
                      mul_ref[0:1, :], mul_ref[1:2, :])
    r = pm[:, 0:RW_W]
    k = pm[:, RW_W:2 * RW_W]
    v = pm[:, 2 * RW_W:3 * RW_W]
    wd = plo[:, 0:128]
    ad = plo[:, 128:256]
    gd = plo[:, 256:512]

    a = _sigmoid(a0_ref[...] + _dot3(ad, a2_ref[...]))
    g = _dot3(_sigmoid(gd), g2_ref[...])
    e = e_ref[...]
    et = et_ref[...]
    kk = k * kk_ref[...]
    ss = _dot_xsel(_dot_xsel(kk * kk, e), et)
    kn = kk / jnp.maximum(jnp.sqrt(ss), 1e-12)
    kmod = k * (1.0 + (a - 1.0) * ka_ref[...])
    wl = jnp.tanh(wd)
    c = math.exp(-0.5)
    lw0 = -_sigmoid(w0_ref[0:1, :] + _dot3(wl, w2_ref[0])) * c
    lw1 = -_sigmoid(w0_ref[1:2, :] + _dot3(wl, w2_ref[1])) * c
    rks = _dot_xsel(_dot_xsel(r * kmod * rk_ref[...], e), et)
    r_o[0] = r
    k_o[0] = kmod
    v_o[0] = v
    kn_o[0] = kn
    ab_o[0] = kn * a
    lw0_o[0] = lw0
    lw1_o[0] = lw1
    g_o[0] = g
    bo_o[0] = rks * v


def _rw_prep(p_main, p_lora, mu_main, mu_lora, w0, w2p, a0, a2p, g2p, k_k, k_a, r_k, e, et):
    b, t, cm = p_main.shape
    cl = p_lora.shape[2]
    tt = min(128, t)
    nb = tt // SUBLANES
    nt8 = t // SUBLANES

    def main_spec(c):
        return pl.BlockSpec((1, tt, c), lambda bi, i: (bi, i, 0))

    def prev_spec(c):
        return pl.BlockSpec((1, SUBLANES, c), lambda bi, i: (bi, jnp.maximum(i * nb - 1, 0), 0))

    def next_spec(c):
        return pl.BlockSpec((1, SUBLANES, c), lambda bi, i: (bi, jnp.minimum((i + 1) * nb, nt8 - 1), 0))

    def full(x):
        nd = x.ndim
        return pl.BlockSpec(x.shape, lambda bi, i: (0,) * nd)

    consts = (mu_main, mu_lora, w0, w2p, a0, a2p, g2p, k_k, k_a, r_k, e, et)
    out_sds = jax.ShapeDtypeStruct((b, t, RW_W), F32)
    return pl.pallas_call(
        _rw_prep_body,
        grid=(b, t // tt),
        in_specs=[main_spec(cm), prev_spec(cm), next_spec(cm), main_spec(cl), prev_spec(cl), next_spec(cl)]
        + [full(x) for x in consts],
        out_specs=[pl.BlockSpec((1, tt, RW_W), lambda bi, i: (bi, i, 0))] * 9,
        out_shape=[out_sds] * 9,
        compiler_params=_params(("parallel", "parallel"), 56),
    )(p_main, p_main, p_main, p_lora, p_lora, p_lora, *consts)


def _rw_scan_body(r_ref, k_ref, v_ref, kn_ref, ab_ref, lw_ref, y_ref, s_ref, *, rev):
    L = RW_CHUNK

    @pl.when(pl.program_id(1) == 0)
    def _():
        s_ref[...] = jnp.zeros_like(s_ref)

    row = lax.broadcasted_iota(jnp.int32, (L, L), 0)
    col = lax.broadcasted_iota(jnp.int32, (L, L), 1)
    if rev:
        incl = col >= row
        strict = col > row
    else:
        incl = col <= row
        strict = col < row
    tri = jnp.where(incl, 1.0, 0.0).astype(BF16)
    eye = jnp.where(row == col, 1.0, 0.0).astype(F32)
    lane = lax.broadcasted_iota(jnp.int32, (L, LANES), 1)
    head_masks = (lane < RW_HEAD, lane >= RW_HEAD)
    sr = lax.broadcasted_iota(jnp.int32, (LANES, LANES), 0)
    sc = lax.broadcasted_iota(jnp.int32, (LANES, LANES), 1)
    bd = (sr < RW_HEAD) == (sc < RW_HEAD)

    lw = lw_ref[0]
    cum = _dot_sel(tri, lw)
    p_in = jnp.exp(cum)
    p_ex = jnp.exp(cum - lw)
    p_inv = jnp.exp(-cum)
    tot = cum[0:1, :] if rev else cum[L - 1:L, :]
    p_end = jnp.exp(tot - cum)
    p_all = jnp.exp(tot)
    at = -kn_ref[0] * p_ex
    rt = r_ref[0] * p_in
    bt = ab_ref[0] * p_inv
    kt = k_ref[0] * p_inv
    bt_end = ab_ref[0] * p_end
    kt_end = k_ref[0] * p_end
    v = v_ref[0]

    for hp in range(RW_HEADS // 2):
        sl = slice(hp * LANES, (hp + 1) * LANES)
        at_p, rt_p, bt_p, kt_p, v_p = at[:, sl], rt[:, sl], bt[:, sl], kt[:, sl], v[:, sl]
        w_pair = jnp.zeros((L, LANES), F32)
        u0_pair = jnp.zeros((L, LANES), F32)
        y0_pair = jnp.zeros((L, LANES), F32)
        a_rb = []
        for h in range(2):
            msk = head_masks[h]
            at_h = jnp.where(msk, at_p, 0.0)
            lhs = jnp.concatenate([at_h, jnp.where(msk, rt_p, 0.0)], axis=0)
            ab_blk = _dot3(lhs, bt_p, NT)
            ak_blk = _dot3(lhs, kt_p, NT)
            a_ab = jnp.where(strict, ab_blk[:L], 0.0)
            a_ak = jnp.where(strict, ak_blk[:L], 0.0)
            a_rb.append(jnp.where(incl, ab_blk[L:], 0.0))
            a_rk = jnp.where(incl, ak_blk[L:], 0.0)
            tm = eye + a_ab
            pw = a_ab
            for _ in range(5):
                pw = _dot3(pw, pw)
                tm = tm + _dot3(tm, pw)
            av = _dot3(jnp.concatenate([a_ak, a_rk], axis=0), v_p)
            tx = _dot3(tm, jnp.concatenate([at_h, jnp.where(msk, av[:L], 0.0)], axis=1))
            w_pair = w_pair + tx[:, :LANES]
            u0_pair = u0_pair + tx[:, LANES:]
            y0_pair = y0_pair + jnp.where(msk, av[L:], 0.0)
        s = s_ref[hp]
        ws = _dot3(jnp.concatenate([w_pair, rt_p], axis=0), s, NT)
        u = ws[:L] + u0_pair
        y = ws[L:] + y0_pair
        for h in range(2):
            y = y + jnp.where(head_masks[h], _dot3(a_rb[h], u), 0.0)
        uv_t = jnp.concatenate([u, v_p], axis=0).T
        upd = _dot3(uv_t, jnp.concatenate([bt_end[:, sl], kt_end[:, sl]], axis=0))
        s_ref[hp] = s * p_all[:, sl] + jnp.where(bd, upd, 0.0)
        y_ref[0, :, sl] = y


def _rw_scan(r, k, v, kn, ab, lw, rev):
    b, t, w = r.shape
    L = RW_CHUNK
    nc = t // L
    if rev:
        idx = lambda bi, c: (bi, nc - 1 - c, 0)
    else:
        idx = lambda bi, c: (bi, c, 0)
    spec = pl.BlockSpec((1, L, w), idx)
    return pl.pallas_call(
        functools.partial(_rw_scan_body, rev=rev),
        grid=(b, nc),
        in_specs=[spec] * 6,
        out_specs=spec,
        out_shape=jax.ShapeDtypeStruct((b, t, w), F32),
        scratch_shapes=[pltpu.VMEM((RW_HEADS // 2, LANES, LANES), F32)],
        compiler_params=_params(("parallel", "arbitrary"), 32),
    )(r, k, v, kn, ab, lw)


def _ml_scan_body(q_ref, k_ref, v_ref, g_ref, b_ref, h_ref, c_ref, n_ref, m_ref, *, rev, direction):
    L = ML_CHUNK

    @pl.when(pl.program_id(1) == 0)
    def _():
        c_ref[...] = jnp.zeros_like(c_ref)
        n_ref[...] = jnp.zeros_like(n_ref)
        m_ref[...] = jnp.zeros_like(m_ref)

    row = lax.broadcasted_iota(jnp.int32, (L, L), 0)
    col = lax.broadcasted_iota(jnp.int32, (L, L), 1)
    incl = (col >= row) if rev else (col <= row)
    tri = jnp.where(incl, 1.0, 0.0).astype(BF16)
    lane = lax.broadcasted_iota(jnp.int32, (L, LANES), 1)
    sub = lax.broadcasted_iota(jnp.int32, (LANES, L), 0)

    g = g_ref[0] + b_ref[...]
    ls = jnp.minimum(g, 0.0) - jnp.log(1.0 + jnp.exp(-jnp.abs(g)))
    bc = _dot_sel(tri, ls)
    g_t = g.T
    bc_t = bc.T
    scale = ML_QK ** -0.5
    for h in range(ML_HEADS):
        ci = direction * ML_HEADS + h
        cf = 2 * ML_HEADS + direction * ML_HEADS + h
        b_col = jnp.sum(jnp.where(lane == cf, bc, 0.0), axis=1, keepdims=True)
        i_col = jnp.sum(jnp.where(lane == ci, g, 0.0), axis=1, keepdims=True)
        b_row = jnp.sum(jnp.where(sub == cf, bc_t, 0.0), axis=0, keepdims=True)
        i_row = jnp.sum(jnp.where(sub == ci, g_t, 0.0), axis=0, keepdims=True)
        m_prev = jnp.max(m_ref[h:h + 1, :], axis=1, keepdims=True)
        qs = q_ref[0, :, h * ML_QK:(h + 1) * ML_QK] * scale
        kc = k_ref[0, :, h * ML_QK:(h + 1) * ML_QK]
        vc = v_ref[0, :, h * ML_V:(h + 1) * ML_V]
        dmat = jnp.where(incl, b_col - b_row + i_row, -jnp.inf)
        inter = b_col + m_prev
        m_t = jnp.maximum(inter, jnp.max(dmat, axis=1, keepdims=True))
        s = _dot1(qs, kc, NT) * jnp.exp(dmat - m_t)
        e_inter = jnp.exp(inter - m_t)
        num = _dot1(s, vc) + e_inter * _dot1(qs, c_ref[h])
        den = jnp.sum(s, axis=1, keepdims=True) + e_inter * jnp.sum(qs * n_ref[h:h + 1, :], axis=1, keepdims=True)
        h_ref[0, :, h * ML_V:(h + 1) * ML_V] = num / jnp.maximum(jnp.abs(den), jnp.exp(-m_t))
        b_last = jnp.min(b_row, axis=1, keepdims=True)
        g_row = b_last - b_row + i_row
        g_col = b_last - b_col + i_col
        m_new = jnp.maximum(b_last + m_prev, jnp.max(g_row, axis=1, keepdims=True))
        w_col = jnp.exp(g_col - m_new)
        dec = jnp.exp(b_last + m_prev - m_new)
        c_ref[h] = dec * c_ref[h] + _dot1(kc.T, vc * w_col)
        n_ref[h:h + 1, :] = dec * n_ref[h:h + 1, :] + jnp.sum(kc * w_col, axis=0, keepdims=True)
        m_ref[h:h + 1, :] = jnp.broadcast_to(m_new, (1, LANES))


def _ml_scan(qk3, vo3, mlg3, bias, rev):
    b, t, _ = qk3.shape
    L = ML_CHUNK
    nc = t // L
    tix = (lambda c: nc - 1 - c) if rev else (lambda c: c)
    return pl.pallas_call(
        functools.partial(_ml_scan_body, rev=rev, direction=1 if rev else 0),
        grid=(b, nc),
        in_specs=[pl.BlockSpec((1, L, ML_QKW), lambda bi, c: (bi, tix(c), 0)),
                  pl.BlockSpec((1, L, ML_QKW), lambda bi, c: (bi, tix(c), 1)),
                  pl.BlockSpec((1, L, ML_VW), lambda bi, c: (bi, tix(c), 0)),
                  pl.BlockSpec((1, L, LANES), lambda bi, c: (bi, tix(c), 0)),
                  pl.BlockSpec((1, LANES), lambda bi, c: (0, 0))],
        out_specs=pl.BlockSpec((1, L, ML_VW), lambda bi, c: (bi, tix(c), 0)),
        out_shape=jax.ShapeDtypeStruct((b, t, ML_VW), F32),
        scratch_shapes=[pltpu.VMEM((ML_HEADS, ML_QK, ML_V), F32),
                        pltpu.VMEM((SUBLANES, LANES), F32),
                        pltpu.VMEM((SUBLANES, LANES), F32)],
        compiler_params=_params(("parallel", "arbitrary"), 32),
    )(qk3, qk3, vo3, mlg3, bias)


def _post_body(yf_ref, yb_ref, bo_ref, g_ref, lnw_ref, lnb_ref, e_ref, et_ref,
               hf_ref, hb_ref, o_ref, nw_ref, ya_ref, yb_o_ref):
    e = e_ref[...]
    et = et_ref[...]
    y = yf_ref[...] + yb_ref[...]
    inv = 1.0 / RW_HEAD
    mu = _dot_xsel(_dot_xsel(y, e), et) * inv
    yc = y - mu
    var = _dot_xsel(_dot_xsel(yc * yc, e), et) * inv
    yn = yc * lax.rsqrt(var + RW_GN_EPS) * lnw_ref[...] + lnb_ref[...]
    ya_ref[...] = ((yn + bo_ref[...]) * g_ref[...]).astype(ya_ref.dtype)

    hsum = hf_ref[...] + hb_ref[...]
    parts = []
    for h in range(ML_HEADS):
        hh = hsum[:, h * ML_V:(h + 1) * ML_V]
        parts.append(hh * lax.rsqrt(jnp.mean(hh * hh, axis=-1, keepdims=True) + ML_NORM_EPS))
    hn = jnp.concatenate(parts, axis=1) * nw_ref[...]
    yb_o_ref[...] = (hn * _sigmoid(o_ref[...])).astype(yb_o_ref.dtype)


def _post(yf, yb, bonus, g, ln_w, ln_b, e, et, hf, hb, vo, norm_w):
    n = yf.shape[0]
    tm = min(256, n)
    tok = lambda c: pl.BlockSpec((tm, c), lambda i: (i, 0))
    const = lambda x: pl.BlockSpec(x.shape, lambda i: (0, 0))
    return pl.pallas_call(
        _post_body,
        grid=(n // tm,),
        in_specs=[tok(RW_W), tok(RW_W), tok(RW_W), tok(RW_W), const(ln_w), const(ln_b), const(e), const(et),
                  tok(ML_VW), tok(ML_VW), pl.BlockSpec((tm, ML_VW), lambda i: (i, 1)), const(norm_w)],
        out_specs=[tok(RW_W), tok(ML_VW)],
        out_shape=[jax.ShapeDtypeStruct((n, RW_W), BF16), jax.ShapeDtypeStruct((n, ML_VW), BF16)],
        compiler_params=_params(("parallel",), 40),
    )(yf, yb, bonus, g, ln_w, ln_b, e, et, hf, hb, vo, norm_w)


def _merge_body(ya_ref, yb_ref, pa_ref, pb_ref, ga_ref, gb_ref, ba_ref, bb_ref, o_ref):
    pa = _mm(ya_ref[...], pa_ref[...])
    pb = _mm(yb_ref[...], pb_ref[...])
    o_ref[...] = (_sigmoid(ga_ref[...] + ba_ref[...]) * pa + _sigmoid(gb_ref[...] + bb_ref[...]) * pb).astype(o_ref.dtype)


def _merge(ya, yb, p_a, p_b, gates, b_gate):
    n = ya.shape[0]
    tm = min(512, n)
    tn = 512
    nj = D_MODEL // tn
    return pl.pallas_call(
        _merge_body,
        grid=(n // tm, nj),
        in_specs=[pl.BlockSpec((tm, RW_W), lambda i, j: (i, 0)), pl.BlockSpec((tm, ML_VW), lambda i, j: (i, 0)),
                  pl.BlockSpec((RW_W, tn), lambda i, j: (0, j)), pl.BlockSpec((ML_VW, tn), lambda i, j: (0, j)),
                  pl.BlockSpec((tm, tn), lambda i, j: (i, j)), pl.BlockSpec((tm, tn), lambda i, j: (i, j + nj)),
                  pl.BlockSpec((1, tn), lambda i, j: (0, j)), pl.BlockSpec((1, tn), lambda i, j: (0, j + nj))],
        out_specs=pl.BlockSpec((tm, tn), lambda i, j: (i, j)),
        out_shape=jax.ShapeDtypeStruct((n, D_MODEL), BF16),
        compiler_params=_params(("parallel", "parallel"), 32),
    )(ya, yb, p_a, p_b, gates, gates, b_gate, b_gate)


def _resid_mm_body(a_ref, w_ref, x_ref, o_ref):
    o_ref[...] = x_ref[...] + _mm(a_ref[...], w_ref[...])


def _resid_mm(a, w, x):
    n, kdim = a.shape
    c = w.shape[1]
    tm = min(512, n)
    tn = 512
    return pl.pallas_call(
        _resid_mm_body,
        grid=(n // tm, c // tn),
        in_specs=[pl.BlockSpec((tm, kdim), lambda i, j: (i, 0)), pl.BlockSpec((kdim, tn), lambda i, j: (0, j)),
                  pl.BlockSpec((tm, tn), lambda i, j: (i, j))],
        out_specs=pl.BlockSpec((tm, tn), lambda i, j: (i, j)),
        out_shape=jax.ShapeDtypeStruct((n, c), F32),
        compiler_params=_params(("parallel", "parallel"), 32),
    )(a, w, x)


def _topk_rows(s, k, payload=None):
    nrow, t = s.shape
    rid = lax.broadcasted_iota(jnp.int32, (nrow, t), 0).astype(F32)
    kid = lax.broadcasted_iota(jnp.int32, (k, t), 0)
    vals = jnp.zeros((k, t), F32)
    sel = jnp.zeros((k, t), F32)
    for j in range(k):
        m = jnp.max(s, axis=0, keepdims=True)
        pos = jnp.min(jnp.where(s == m, rid, float(nrow)), axis=0, keepdims=True)
        hit = rid == pos
        if payload is None:
            picked = pos
        else:
            picked = jnp.max(jnp.where(hit, payload, -1.0), axis=0, keepdims=True)
        vals = jnp.where(kid == j, m, vals)
        sel = jnp.where(kid == j, picked, sel)
        s = jnp.where(hit, -jnp.inf, s)
    return vals, sel


def _router_body(q_ref, keys_ref, a_ref, b_ref, gw_ref):
    K = PEER_TOPK
    a_parts, b_parts, w_parts = [], [], []
    for h in range(PEER_HEADS):
        sv, si = [], []
        for p in range(2):
            c0 = (h * 2 + p) * PEER_HALF
            st = _dot3(keys_ref[h * 2 + p], q_ref[:, c0:c0 + PEER_HALF], NT)
            vals, idx = _topk_rows(st, K)
            sv.append(vals)
            si.append(idx)
        cand = jnp.concatenate([sv[0][i:i + 1, :] + sv[1] for i in range(K)], axis=0)
        cidx = jnp.concatenate([si[0][i:i + 1, :] * float(PEER_NKEYS) + si[1] for i in range(K)], axis=0)
        best, eidx = _topk_rows(cand, K, payload=cidx)
        ex = jnp.exp(best - best[0:1, :])
        w_parts.append(ex / jnp.sum(ex, axis=0, keepdims=True))
        hi = jnp.floor(eidx * (1.0 / PEER_NKEYS))
        a_parts.append(hi)
        b_parts.append(eidx - hi * float(PEER_NKEYS))
    a_ref[...] = jnp.concatenate(a_parts, axis=0).T
    b_ref[...] = jnp.concatenate(b_parts, axis=0).T
    gw_ref[...] = jnp.concatenate(w_parts, axis=0).T


def _router(q, keys):
    n = q.shape[0]
    tq = min(256, n)
    tok = pl.BlockSpec((tq, PEER_HK), lambda i: (i, 0))
    sds = jax.ShapeDtypeStruct((n, PEER_HK), F32)
    return pl.pallas_call(
        _router_body,
        grid=(n // tq,),
        in_specs=[pl.BlockSpec((tq, q.shape[1]), lambda i: (i, 0)),
                  pl.BlockSpec(keys.shape, lambda i: (0, 0, 0))],
        out_specs=[tok, tok, tok],
        out_shape=[sds, sds, sds],
        compiler_params=_params(("parallel",), 32),
    )(q, keys)


PEER_CI = 4
PEER_CE = PEER_CI * PEER_NKEYS


def _peer_body(x_ref, gn_ref, a_ref, b_ref, w_ref, ut_ref, v_ref, gf_ref, o_ref, h_ref, g2_ref, *,
               pitch, final_norm):
    j = pl.program_id(1)
    t = x_ref.shape[0]

    @pl.when(j == 0)
    def _():
        xf = x_ref[...]
        h = xf * lax.rsqrt(jnp.mean(xf * xf, axis=-1, keepdims=True) + RMS_EPS) * gn_ref[...]
        h_ref[...] = h.astype(BF16)
        o_ref[...] = xf
        kid = lax.broadcasted_iota(jnp.int32, (PEER_NKEYS, PEER_HK), 0).astype(F32)

        def per_token(ti, carry):
            arow = a_ref[pl.ds(ti, 1), :]
            brow = b_ref[pl.ds(ti, 1), :]
            wrow = w_ref[pl.ds(ti, 1), :]
            w_hi = wrow.astype(BF16).astype(F32)
            w_lo = wrow - w_hi
            hit_a = kid == arow
            lhs = jnp.concatenate([jnp.where(hit_a, w_hi, 0.0), jnp.where(hit_a, w_lo, 0.0)], axis=1).astype(BF16)
            ohb = jnp.where(kid == brow, 1.0, 0.0).astype(BF16)
            rhs = jnp.concatenate([ohb, ohb], axis=1)
            g_t = _mm(lhs, rhs, NT)
            g2_ref[pl.ds(ti, PEER_NKEYS, stride=pitch), :] = g_t
            return carry

        lax.fori_loop(0, t, per_token, 0)

    z = _mm(h_ref[...], ut_ref[...])
    parts = []
    for ii in range(PEER_CI):
        start = pl.multiple_of((j * PEER_CI + ii) * pitch, SUBLANES)
        gblk = g2_ref[pl.ds(start, t), :]
        zz = z[:, ii * PEER_NKEYS:(ii + 1) * PEER_NKEYS]
        act = 0.5 * zz * (1.0 + lax.erf(zz * (2.0 ** -0.5)))
        parts.append((act * gblk).astype(BF16))
    o_ref[...] += _mm(jnp.concatenate(parts, axis=1), v_ref[...])

    if final_norm:
        @pl.when(j == pl.num_programs(1) - 1)
        def _():
            y = o_ref[...]
            o_ref[...] = y * lax.rsqrt(jnp.mean(y * y, axis=-1, keepdims=True) + RMS_EPS) * gf_ref[...]


def _peer(x, g_ffn, a_idx, b_idx, gw, u_t, v_tab, g_final, final_norm):
    n, d = x.shape
    t = min(256, n)
    pitch = t + SUBLANES
    nch = PEER_N // PEER_CE
    tok = lambda c: pl.BlockSpec((t, c), lambda i, j: (i, 0))
    return pl.pallas_call(
        functools.partial(_peer_body, pitch=pitch, final_norm=final_norm),
        grid=(n // t, nch),
        in_specs=[tok(d), pl.BlockSpec((1, d), lambda i, j: (0, 0)), tok(PEER_HK), tok(PEER_HK), tok(PEER_HK),
                  pl.BlockSpec((d, PEER_CE), lambda i, j: (0, j)), pl.BlockSpec((PEER_CE, d), lambda i, j: (j, 0)),
                  pl.BlockSpec((1, d), lambda i, j: (0, 0))],
        out_specs=tok(d),
        out_shape=jax.ShapeDtypeStruct((n, d), F32),
        scratch_shapes=[pltpu.VMEM((t, d), BF16), pltpu.VMEM((PEER_NKEYS * pitch, LANES), F32)],
        compiler_params=_params(("parallel", "arbitrary"), 56),
    )(x, g_ffn, a_idx, b_idx, gw, u_t, v_tab, g_final)


def _pad_cols(pieces, width):
    rows = pieces[0][0].shape[0]
    out = jnp.zeros((rows, width), pieces[0][0].dtype)
    for arr, off in pieces:
        out = out.at[:, off:off + arr.shape[1]].set(arr)
    return out


def _pad_rows(x, rows):
    return jnp.zeros((rows,) + x.shape[1:], x.dtype).at[:x.shape[0]].set(x)


def _lora_layout(x):
    o = 3 * RW_W
    return _pad_cols([(x[:, o:o + 64], 0), (x[:, o + 64:o + 128], 128), (x[:, o + 128:o + 288], 256)], 512)


def _prep_layer(l, w):
    f = {}
    w_in = w["w_in"][l]
    rw = w_in[:, :RW_COLS]
    ml = w_in[:, RW_COLS:RW_COLS + ML_COLS]
    gt = w_in[:, RW_COLS + ML_COLS:]
    f["norm_mix"] = w["norm_mix"][l][None, :]
    f["w_rkv"] = rw[:, :3 * RW_W].astype(BF16)
    f["w_lora"] = _lora_layout(rw).astype(BF16)
    f["w_mlqk"] = ml[:, :2 * ML_QKW].astype(BF16)
    f["w_mlvo"] = ml[:, 2 * ML_QKW:2 * ML_QKW + 2 * ML_VW].astype(BF16)
    f["w_mlg"] = _pad_cols([(ml[:, 2 * ML_QKW + 2 * ML_VW:], 0)], LANES).astype(BF16)
    f["w_gate"] = gt.astype(BF16)
    mu = jnp.stack([w["rw_mu_prev"][l], w["rw_mu_next"][l]])
    f["mu_main"] = mu[:, :3 * RW_W]
    f["mu_lora"] = _lora_layout(mu)
    f["w0"] = w["rw_w0"][l]
    f["w2p"] = jnp.stack([_pad_rows(w["rw_w2"][l, 0], 128), _pad_rows(w["rw_w2"][l, 1], 128)])
    f["a0"] = w["rw_a0"][l][None, :]
    f["a2p"] = _pad_rows(w["rw_a2"][l], 128)
    f["g2p"] = _pad_rows(w["rw_g2"][l], 256)
    f["k_k"] = w["rw_k_k"][l][None, :]
    f["k_a"] = w["rw_k_a"][l][None, :]
    f["r_k"] = w["rw_r_k"][l].reshape(1, RW_W)
    f["ln_w"] = w["rw_ln_w"][l][None, :]
    f["ln_b"] = w["rw_ln_b"][l][None, :]
    f["ml_bias"] = _pad_cols([(w["ml_b_i"][l].reshape(1, -1), 0), (w["ml_b_f"][l].reshape(1, -1), 2 * ML_HEADS)], LANES)
    f["ml_norm_w"] = w["ml_norm_w"][l][None, :]
    f["p_a"] = w["p_a"][l].astype(BF16)
    f["p_b"] = w["p_b"][l].astype(BF16)
    f["b_gate"] = w["b_gate"][l][None, :]
    f["w_out"] = w["w_out"][l].astype(BF16)
    f["norm_ffn"] = w["norm_ffn"][l][None, :]
    wq = w["peer_wq"][l]
    wq_hi = wq.astype(BF16)
    f["wq"] = (wq_hi, (wq - wq_hi.astype(F32)).astype(BF16))
    f["keys"] = w["peer_keys"][l].reshape(PEER_HEADS * 2, PEER_NKEYS, PEER_HALF)
    f["u_t"] = w["peer_u"][l].astype(BF16).T
    f["v"] = w["peer_v"][l].astype(BF16)
    return f


def _head_selectors():
    ch = jnp.arange(RW_W)[:, None] // RW_HEAD
    e = (ch == jnp.arange(LANES)[None, :]).astype(BF16)
    return e, e.T


def _layer(x, f, e, et, g_final, final_norm):
    b, t, d = x.shape
    n = b * t
    x2 = x.reshape(n, d)
    g = f["norm_mix"]
    p_rkv = _norm_mm(x2, g, (f["w_rkv"],))
    p_lora = _norm_mm(x2, g, (f["w_lora"],))
    p_qk = _norm_mm(x2, g, (f["w_mlqk"],))
    p_vo = _norm_mm(x2, g, (f["w_mlvo"],))
    p_mlg = _norm_mm(x2, g, (f["w_mlg"],))
    p_gate = _norm_mm(x2, g, (f["w_gate"],))

    r, k, v, kn, ab, lw0, lw1, gg, bonus = _rw_prep(
        p_rkv.reshape(b, t, -1), p_lora.reshape(b, t, -1), f["mu_main"], f["mu_lora"], f["w0"], f["w2p"],
        f["a0"], f["a2p"], f["g2p"], f["k_k"], f["k_a"], f["r_k"], e, et)
    y_f = _rw_scan(r, k, v, kn, ab, lw0, rev=False)
    y_b = _rw_scan(r, k, v, kn, ab, lw1, rev=True)

    qk3 = p_qk.reshape(b, t, -1)
    vo3 = p_vo.reshape(b, t, -1)
    mlg3 = p_mlg.reshape(b, t, -1)
    h_f = _ml_scan(qk3, vo3, mlg3, f["ml_bias"], rev=False)
    h_b = _ml_scan(qk3, vo3, mlg3, f["ml_bias"], rev=True)

    ya, yb = _post(y_f.reshape(n, -1), y_b.reshape(n, -1), bonus.reshape(n, -1), gg.reshape(n, -1),
                   f["ln_w"], f["ln_b"], e, et, h_f.reshape(n, -1), h_b.reshape(n, -1), p_vo, f["ml_norm_w"])
    merged = _merge(ya, yb, f["p_a"], f["p_b"], p_gate, f["b_gate"])
    x1 = _resid_mm(merged, f["w_out"], x2)

    q = _norm_mm(x1, f["norm_ffn"], f["wq"])
    a_idx, b_idx, gw = _router(q, f["keys"])
    x_out = _peer(x1, f["norm_ffn"], a_idx, b_idx, gw, f["u_t"], f["v"], g_final, final_norm)
    return x_out.reshape(b, t, d)


def _trunk(x, layers, e, et, g_final):
    for l, f in enumerate(layers):
        x = _layer(x, f, e, et, g_final, final_norm=(l == len(layers) - 1))
    return x


def kernel(x_prompt, x_sample, norm_mix, w_in, rw_mu_prev, rw_mu_next, rw_w0, rw_w2, rw_a0, rw_a2, rw_g2, rw_k_k, rw_k_a, rw_r_k, rw_ln_w, rw_ln_b, ml_b_i, ml_b_f, ml_norm_w, p_a, p_b, b_gate, w_out, norm_ffn, peer_wq, peer_keys, peer_u, peer_v, norm_final):
    w = dict(norm_mix=norm_mix, w_in=w_in, rw_mu_prev=rw_mu_prev, rw_mu_next=rw_mu_next, rw_w0=rw_w0, rw_w2=rw_w2,
             rw_a0=rw_a0, rw_a2=rw_a2, rw_g2=rw_g2, rw_k_k=rw_k_k, rw_k_a=rw_k_a, rw_r_k=rw_r_k, rw_ln_w=rw_ln_w,
             rw_ln_b=rw_ln_b, ml_b_i=ml_b_i, ml_b_f=ml_b_f, ml_norm_w=ml_norm_w, p_a=p_a, p_b=p_b, b_gate=b_gate,
             w_out=w_out, norm_ffn=norm_ffn, peer_wq=peer_wq, peer_keys=peer_keys, peer_u=peer_u, peer_v=peer_v)
    depth = w_in.shape[0]
    layers = [_prep_layer(l, w) for l in range(depth)]
    e, et = _head_selectors()
    g_final = norm_final[None, :]
    return (_trunk(x_prompt, layers, e, et, g_final), _trunk(x_sample, layers, e, et, g_final))
```

```python
import functools
import math

import jax
import jax.numpy as jnp
from jax import lax
from jax.experimental import pallas as pl
from jax.experimental.pallas import tpu as pltpu

F32 = jnp.float32
BF16 = jnp.bfloat16

D_MODEL = 2048
RMS_EPS = 1e-6
RW_W = 1024
RW_HEAD = 64
RW_HEADS = 16
RW_DECAY_LORA = 64
RW_A_LORA = 64
RW_G_LORA = 160
RW_GN_EPS = 64e-5
RW_COLS = 3 * RW_W + RW_DECAY_LORA + RW_A_LORA + RW_G_LORA
RW_CHUNK = 64
RW_PASSES_A = 1
RW_PASSES_INV = 1
RW_PASSES_S = 1
ML_HEADS = 4
ML_QK = 128
ML_V = 256
ML_QKW = ML_HEADS * ML_QK
ML_VW = ML_HEADS * ML_V
ML_CHUNK = 128
ML_NORM_EPS = 1e-6
ML_COLS = 2 * ML_QKW + 2 * ML_VW + 4 * ML_HEADS
PF_LORA_OFF = 3 * RW_W
PF_LORA_W = 512
PF_MLG_OFF = PF_LORA_OFF + PF_LORA_W
PF_COLS = 4096
PB_GATE_OFF = 2 * ML_QKW + 2 * ML_VW
PB_COLS = PB_GATE_OFF + 2 * D_MODEL
PEER_HEADS = 8
PEER_NKEYS = 128
PEER_N = PEER_NKEYS * PEER_NKEYS
PEER_HALF = 128
PEER_TOPK = 16
PEER_HK = PEER_HEADS * PEER_TOPK
LANES = 128
SUBLANES = 8

NN = ((1,), (0,))
NT = ((1,), (1,))


def _mm(a, b, dims=NN):
    return lax.dot_general(a, b, (dims, ((), ())), preferred_element_type=F32)


def _split2(a):
    hi = a.astype(BF16)
    lo = (a - hi.astype(F32)).astype(BF16)
    return hi, lo


def _split3(a):
    hi = a.astype(BF16)
    r1 = a - hi.astype(F32)
    mid = r1.astype(BF16)
    lo = (r1 - mid.astype(F32)).astype(BF16)
    return hi, mid, lo


def _dot1(a, b, dims=NN):
    return _mm(a.astype(BF16), b.astype(BF16), dims)


def _dot3(a, b, dims=NN):
    ah, al = _split2(a)
    bh, bl = _split2(b)
    return _mm(ah, bh, dims) + (_mm(ah, bl, dims) + _mm(al, bh, dims))


def _dotp(a, b, dims, passes):
    return _dot3(a, b, dims) if passes == 3 else _dot1(a, b, dims)


def _dot_sel(sel_bf16, x, dims=NN):
    h, m, l = _split3(x)
    return _mm(sel_bf16, h, dims) + (_mm(sel_bf16, m, dims) + _mm(sel_bf16, l, dims))


def _dot_xsel(x, sel_bf16, dims=NN):
    h, m, l = _split3(x)
    return _mm(h, sel_bf16, dims) + (_mm(m, sel_bf16, dims) + _mm(l, sel_bf16, dims))


def _params(sem, vmem_mb):
    return pltpu.CompilerParams(dimension_semantics=sem, vmem_limit_bytes=vmem_mb << 20)


def _sigmoid(x):
    return 1.0 / (1.0 + jnp.exp(-x))


def _norm_cast_body(x_ref, g_ref, *o_refs):
    xf = x_ref[...]
    h = xf * lax.rsqrt(jnp.mean(xf * xf, axis=-1, keepdims=True) + RMS_EPS) * g_ref[...]
    hi = h.astype(BF16)
    o_refs[0][...] = hi
    if len(o_refs) == 2:
        o_refs[1][...] = (h - hi.astype(F32)).astype(BF16)


def _norm_cast(x, g, n_parts):
    n, d = x.shape
    tm = min(512, n)
    spec = pl.BlockSpec((tm, d), lambda i: (i, 0))
    return pl.pallas_call(
        _norm_cast_body,
        grid=(n // tm,),
        in_specs=[spec, pl.BlockSpec((1, d), lambda i: (0, 0))],
        out_specs=[spec] * n_parts,
        out_shape=[jax.ShapeDtypeStruct((n, d), BF16)] * n_parts,
        compiler_params=_params(("parallel",), 32),
    )(x, g)


def _matmul_body(*refs, n_parts):
    a_refs = refs[:n_parts]
    w_refs = refs[n_parts:2 * n_parts]
    o_ref = refs[2 * n_parts]
    acc = _mm(a_refs[0][...], w_refs[0][...])
    if n_parts == 2:
        acc = acc + (_mm(a_refs[0][...], w_refs[1][...]) + _mm(a_refs[1][...], w_refs[0][...]))
    o_ref[...] = acc.astype(o_ref.dtype)


def _matmul(a_parts, w_parts, out_dtype):
    n, d = a_parts[0].shape
    c = w_parts[0].shape[1]
    n_parts = len(a_parts)
    tm = min(1024, n)
    tn = min(1024 // n_parts, c)
    return pl.pallas_call(
        functools.partial(_matmul_body, n_parts=n_parts),
        grid=(n // tm, c // tn),
        in_specs=[pl.BlockSpec((tm, d), lambda i, j: (i, 0))] * n_parts
        + [pl.BlockSpec((d, tn), lambda i, j: (0, j))] * n_parts,
        out_specs=pl.BlockSpec((tm, tn), lambda i, j: (i, j)),
        out_shape=jax.ShapeDtypeStruct((n, c), out_dtype),
        compiler_params=_params(("parallel", "parallel"), 48),
    )(*a_parts, *w_parts)


def _shift_rows(p, prev_row, next_row, mu_prev, mu_next):
    n = p.shape[0]
    rid = lax.broadcasted_iota(jnp.int32, p.shape, 0)
    prev = jnp.where(rid == 0, prev_row, pltpu.roll(p, 1, 0))
    nxt = jnp.where(rid == n - 1, next_row, pltpu.roll(p, n - 1, 0))
    return p + mu_prev * (prev - p) + mu_next * (nxt - p)


def _rw_prep_body(m_ref, mp_ref, mn_ref, l_ref, lp_ref, ln_ref, mum_ref, mul_ref, w0_ref, w2_ref,
                  a0_ref, a2_ref, g2_ref, kk_ref, ka_ref, rk_ref, e_ref, et_ref,
                  r_o, k_o, v_o, kn_o, ab_o, lw0_o, lw1_o, g_o, bo_o):
    i = pl.program_id(1)
    first = i == 0
    last = i == pl.num_programs(1) - 1
    zero_m = jnp.zeros((1, m_ref.shape[2]), F32)
    zero_l = jnp.zeros((1, l_ref.shape[2]), F32)
    pm = _shift_rows(m_ref[0],
                     jnp.where(first, zero_m, mp_ref[0, SUBLANES - 1:SUBLANES, :]),
                     jnp.where(last, zero_m, mn_ref[0, 0:1, :]),
                     mum_ref[0:1, :], mum_ref[1:2, :])
    plo = _shift_rows(l_ref[0],
                      jnp.where(first, zero_l, lp_ref[0, SUBLANES - 1:SUBLANES, :]),
                      jnp.where(last, zero_l, ln_ref[0, 0:1, :]),
                      mul_ref[0:1, :], mul_ref[1:2, :])
    r = pm[:, 0:RW_W]
    k = pm[:, RW_W:2 * RW_W]
    v = pm[:, 2 * RW_W:3 * RW_W]
    wd = plo[:, 0:128]
    ad = plo[:, 128:256]
    gd = plo[:, 256:512]

    a = _sigmoid(a0_ref[...] + _dot3(ad, a2_ref[...]))
    g = _dot3(_sigmoid(gd), g2_ref[...])
    e = e_ref[...]
    et = et_ref[...]
    kk = k * kk_ref[...]
    ss = _dot_xsel(_dot_xsel(kk * kk, e), et)
    kn = kk / jnp.maximum(jnp.sqrt(ss), 1e-12)
    kmod = k * (1.0 + (a - 1.0) * ka_ref[...])
    wl = jnp.tanh(wd)
    c = math.exp(-0.5)
    lw0 = -_sigmoid(w0_ref[0:1, :] + _dot3(wl, w2_ref[0])) * c
    lw1 = -_sigmoid(w0_ref[1:2, :] + _dot3(wl, w2_ref[1])) * c
    rks = _dot_xsel(_dot_xsel(r * kmod * rk_ref[...], e), et)
    r_o[0] = r
    k_o[0] = kmod
    v_o[0] = v
    kn_o[0] = kn
    ab_o[0] = kn * a
    lw0_o[0] = lw0
    lw1_o[0] = lw1
    g_o[0] = g
    bo_o[0] = rks * v


def _rw_prep(p_f, mu_main, mu_lora, w0, w2p, a0, a2p, g2p, k_k, k_a, r_k, e, et):
    b, t, _ = p_f.shape
    cm, cl = 3 * RW_W, PF_LORA_W
    col = {cm: 0, cl: PF_LORA_OFF // PF_LORA_W}
    tt = min(128, t)
    nb = tt // SUBLANES
    nt8 = t // SUBLANES

    def main_spec(c):
        return pl.BlockSpec((1, tt, c), lambda bi, i: (bi, i, col[c]))

    def prev_spec(c):
        return pl.BlockSpec((1, SUBLANES, c), lambda bi, i: (bi, jnp.maximum(i * nb - 1, 0), col[c]))

    def next_spec(c):
        return pl.BlockSpec((1, SUBLANES, c), lambda bi, i: (bi, jnp.minimum((i + 1) * nb, nt8 - 1), col[c]))

    def full(x):
        nd = x.ndim
        return pl.BlockSpec(x.shape, lambda bi, i: (0,) * nd)

    consts = (mu_main, mu_lora, w0, w2p, a0, a2p, g2p, k_k, k_a, r_k, e, et)
    out_sds = jax.ShapeDtypeStruct((b, t, RW_W), F32)
    return pl.pallas_call(
        _rw_prep_body,
        grid=(b, t // tt),
        in_specs=[main_spec(cm), prev_spec(cm), next_spec(cm), main_spec(cl), prev_spec(cl), next_spec(cl)]
        + [full(x) for x in consts],
        out_specs=[pl.BlockSpec((1, tt, RW_W), lambda bi, i: (bi, i, 0))] * 9,
        out_shape=[out_sds] * 9,
        compiler_params=_params(("parallel", "parallel"), 56),
    )(p_f, p_f, p_f, p_f, p_f, p_f, *consts)


def _rw_scan_body(r_ref, k_ref, v_ref, kn_ref, ab_ref, lw_ref, y_ref, s_ref, *, rev):
    L = RW_CHUNK

    @pl.when(pl.program_id(1) == 0)
    def _():
        s_ref[...] = jnp.zeros_like(s_ref)

    row = lax.broadcasted_iota(jnp.int32, (L, L), 0)
    col = lax.broadcasted_iota(jnp.int32, (L, L), 1)
    if rev:
        incl = col >= row
        strict = col > row
    else:
        incl = col <= row
        strict = col < row
    tri = jnp.where(incl, 1.0, 0.0).astype(BF16)
    eye = jnp.where(row == col, 1.0, 0.0).astype(F32)
    lane = lax.broadcasted_iota(jnp.int32, (L, LANES), 1)
    head_masks = (lane < RW_HEAD, lane >= RW_HEAD)
    sr = lax.broadcasted_iota(jnp.int32, (LANES, LANES), 0)
    sc = lax.broadcasted_iota(jnp.int32, (LANES, LANES), 1)
    bd = (sr < RW_HEAD) == (sc < RW_HEAD)

    lw = lw_ref[0]
    cum = _dot_sel(tri, lw)
    p_in = jnp.exp(cum)
    p_ex = jnp.exp(cum - lw)
    p_inv = jnp.exp(-cum)
    tot = cum[0:1, :] if rev else cum[L - 1:L, :]
    p_end = jnp.exp(tot - cum)
    p_all = jnp.exp(tot)
    at = -kn_ref[0] * p_ex
    rt = r_ref[0] * p_in
    bt = ab_ref[0] * p_inv
    kt = k_ref[0] * p_inv
    bt_end = ab_ref[0] * p_end
    kt_end = k_ref[0] * p_end
    v = v_ref[0]

    heads = [(hp, h) for hp in range(RW_HEADS // 2) for h in range(2)]
    sls = [slice(hp * LANES, (hp + 1) * LANES) for hp in range(RW_HEADS // 2)]
    at_h, a_ab, a_ak, a_rb, a_rk = [], [], [], [], []
    for hp, h in heads:
        sl, msk = sls[hp], head_masks[h]
        at_m = jnp.where(msk, at[:, sl], 0.0)
        lhs = jnp.concatenate([at_m, jnp.where(msk, rt[:, sl], 0.0)], axis=0)
        ab_blk = _dotp(lhs, bt[:, sl], NT, RW_PASSES_A)
        ak_blk = _dotp(lhs, kt[:, sl], NT, RW_PASSES_A)
        at_h.append(at_m)
        a_ab.append(jnp.where(strict, ab_blk[:L], 0.0))
        a_ak.append(jnp.where(strict, ak_blk[:L], 0.0))
        a_rb.append(jnp.where(incl, ab_blk[L:], 0.0))
        a_rk.append(jnp.where(incl, ak_blk[L:], 0.0))
    tm = [eye + a for a in a_ab]
    pw = a_ab
    for _ in range(5):
        pw = [_dotp(p, p, NN, RW_PASSES_INV) for p in pw]
        tm = [t_ + _dotp(t_, p, NN, RW_PASSES_INV) for t_, p in zip(tm, pw)]
    av = [_dotp(jnp.concatenate([a_ak[i], a_rk[i]], axis=0), v[:, sls[hp]], NN, RW_PASSES_A)
          for i, (hp, h) in enumerate(heads)]
    tx = [_dotp(tm[i], jnp.concatenate([at_h[i], jnp.where(head_masks[h], av[i][:L], 0.0)], axis=1), NN, RW_PASSES_A)
          for i, (hp, h) in enumerate(heads)]
    for hp in range(RW_HEADS // 2):
        sl = sls[hp]
        i0, i1 = 2 * hp, 2 * hp + 1
        w_pair = tx[i0][:, :LANES] + tx[i1][:, :LANES]
        u0_pair = tx[i0][:, LANES:] + tx[i1][:, LANES:]
        y0_pair = jnp.where(head_masks[0], av[i0][L:], av[i1][L:])
        s = s_ref[hp]
        ws = _dotp(jnp.concatenate([w_pair, rt[:, sl]], axis=0), s, NT, RW_PASSES_S)
        u = ws[:L] + u0_pair
        y = ws[L:] + y0_pair + jnp.where(head_masks[0], _dotp(a_rb[i0], u, NN, RW_PASSES_A),
                                         _dotp(a_rb[i1], u, NN, RW_PASSES_A))
        uv_t = jnp.concatenate([u, v[:, sl]], axis=0).T
        upd = _dotp(uv_t, jnp.concatenate([bt_end[:, sl], kt_end[:, sl]], axis=0), NN, RW_PASSES_S)
        s_ref[hp] = s * p_all[:, sl] + jnp.where(bd, upd, 0.0)
        y_ref[0, :, sl] = y


def _rw_scan(r, k, v, kn, ab, lw, rev):
    b, t, w = r.shape
    L = RW_CHUNK
    nc = t // L
    if rev:
        idx = lambda bi, c: (bi, nc - 1 - c, 0)
    else:
        idx = lambda bi, c: (bi, c, 0)
    spec = pl.BlockSpec((1, L, w), idx)
    return pl.pallas_call(
        functools.partial(_rw_scan_body, rev=rev),
        grid=(b, nc),
        in_specs=[spec] * 6,
        out_specs=spec,
        out_shape=jax.ShapeDtypeStruct((b, t, w), F32),
        scratch_shapes=[pltpu.VMEM((RW_HEADS // 2, LANES, LANES), F32)],
        compiler_params=_params(("parallel", "arbitrary"), 32),
    )(r, k, v, kn, ab, lw)


def _ml_scan_body(q_ref, k_ref, v_ref, g_ref, b_ref, h_ref, c_ref, n_ref, m_ref, *, rev, direction):
    L = ML_CHUNK

    @pl.when(pl.program_id(1) == 0)
    def _():
        c_ref[...] = jnp.zeros_like(c_ref)
        n_ref[...] = jnp.zeros_like(n_ref)
        m_ref[...] = jnp.zeros_like(m_ref)

    row = lax.broadcasted_iota(jnp.int32, (L, L), 0)
    col = lax.broadcasted_iota(jnp.int32, (L, L), 1)
    incl = (col >= row) if rev else (col <= row)
    tri = jnp.where(incl, 1.0, 0.0).astype(BF16)
    lane = lax.broadcasted_iota(jnp.int32, (L, LANES), 1)
    sub = lax.broadcasted_iota(jnp.int32, (LANES, L), 0)

    g = g_ref[0] + b_ref[...]
    ls = jnp.minimum(g, 0.0) - jnp.log(1.0 + jnp.exp(-jnp.abs(g)))
    bc = _dot_sel(tri, ls)
    g_t = g.T
    bc_t = bc.T
    scale = ML_QK ** -0.5
    for h in range(ML_HEADS):
        ci = direction * ML_HEADS + h
        cf = 2 * ML_HEADS + direction * ML_HEADS + h
        b_col = jnp.sum(jnp.where(lane == cf, bc, 0.0), axis=1, keepdims=True)
        i_col = jnp.sum(jnp.where(lane == ci, g, 0.0), axis=1, keepdims=True)
        b_row = jnp.sum(jnp.where(sub == cf, bc_t, 0.0), axis=0, keepdims=True)
        i_row = jnp.sum(jnp.where(sub == ci, g_t, 0.0), axis=0, keepdims=True)
        m_prev = jnp.max(m_ref[h:h + 1, :], axis=1, keepdims=True)
        qs = q_ref[0, :, h * ML_QK:(h + 1) * ML_QK].astype(F32) * scale
        kc = k_ref[0, :, h * ML_QK:(h + 1) * ML_QK].astype(F32)
        vc = v_ref[0, :, h * ML_V:(h + 1) * ML_V].astype(F32)
        dmat = jnp.where(incl, b_col - b_row + i_row, -jnp.inf)
        inter = b_col + m_prev
        m_t = jnp.maximum(inter, jnp.max(dmat, axis=1, keepdims=True))
        s = _dot1(qs, kc, NT) * jnp.exp(dmat - m_t)
        e_inter = jnp.exp(inter - m_t)
        num = _dot1(s, vc) + e_inter * _dot1(qs, c_ref[h])
        den = jnp.sum(s, axis=1, keepdims=True) + e_inter * jnp.sum(qs * n_ref[h:h + 1, :], axis=1, keepdims=True)
        h_ref[0, :, h * ML_V:(h + 1) * ML_V] = num / jnp.maximum(jnp.abs(den), jnp.exp(-m_t))
        b_last = jnp.min(b_row, axis=1, keepdims=True)
        g_row = b_last - b_row + i_row
        g_col = b_last - b_col + i_col
        m_new = jnp.maximum(b_last + m_prev, jnp.max(g_row, axis=1, keepdims=True))
        w_col = jnp.exp(g_col - m_new)
        dec = jnp.exp(b_last + m_prev - m_new)
        c_ref[h] = dec * c_ref[h] + _dot1(kc.T, vc * w_col)
        n_ref[h:h + 1, :] = dec * n_ref[h:h + 1, :] + jnp.sum(kc * w_col, axis=0, keepdims=True)
        m_ref[h:h + 1, :] = jnp.broadcast_to(m_new, (1, LANES))


def _ml_scan(p_b, p_f, bias, rev):
    b, t, _ = p_b.shape
    L = ML_CHUNK
    nc = t // L
    tix = (lambda c: nc - 1 - c) if rev else (lambda c: c)
    return pl.pallas_call(
        functools.partial(_ml_scan_body, rev=rev, direction=1 if rev else 0),
        grid=(b, nc),
        in_specs=[pl.BlockSpec((1, L, ML_QKW), lambda bi, c: (bi, tix(c), 0)),
                  pl.BlockSpec((1, L, ML_QKW), lambda bi, c: (bi, tix(c), 1)),
                  pl.BlockSpec((1, L, ML_VW), lambda bi, c: (bi, tix(c), 1)),
                  pl.BlockSpec((1, L, LANES), lambda bi, c: (bi, tix(c), PF_MLG_OFF // LANES)),
                  pl.BlockSpec((1, LANES), lambda bi, c: (0, 0))],
        out_specs=pl.BlockSpec((1, L, ML_VW), lambda bi, c: (bi, tix(c), 0)),
        out_shape=jax.ShapeDtypeStruct((b, t, ML_VW), F32),
        scratch_shapes=[pltpu.VMEM((ML_HEADS, ML_QK, ML_V), F32),
                        pltpu.VMEM((SUBLANES, LANES), F32),
                        pltpu.VMEM((SUBLANES, LANES), F32)],
        compiler_params=_params(("parallel", "arbitrary"), 32),
    )(p_b, p_b, p_b, p_f, bias)


def _post_body(yf_ref, yb_ref, bo_ref, g_ref, lnw_ref, lnb_ref, e_ref, et_ref,
               hf_ref, hb_ref, o_ref, nw_ref, ya_ref, yb_o_ref):
    e = e_ref[...]
    et = et_ref[...]
    y = yf_ref[...] + yb_ref[...]
    inv = 1.0 / RW_HEAD
    mu = _dot_xsel(_dot_xsel(y, e), et) * inv
    yc = y - mu
    var = _dot_xsel(_dot_xsel(yc * yc, e), et) * inv
    yn = yc * lax.rsqrt(var + RW_GN_EPS) * lnw_ref[...] + lnb_ref[...]
    ya_ref[...] = ((yn + bo_ref[...]) * g_ref[...]).astype(ya_ref.dtype)

    hsum = hf_ref[...] + hb_ref[...]
    parts = []
    for h in range(ML_HEADS):
        hh = hsum[:, h * ML_V:(h + 1) * ML_V]
        parts.append(hh * lax.rsqrt(jnp.mean(hh * hh, axis=-1, keepdims=True) + ML_NORM_EPS))
    hn = jnp.concatenate(parts, axis=1) * nw_ref[...]
    yb_o_ref[...] = (hn * _sigmoid(o_ref[...].astype(F32))).astype(yb_o_ref.dtype)


def _post(yf, yb, bonus, g, ln_w, ln_b, e, et, hf, hb, vo, norm_w):
    n = yf.shape[0]
    tm = min(256, n)
    tok = lambda c: pl.BlockSpec((tm, c), lambda i: (i, 0))
    const = lambda x: pl.BlockSpec(x.shape, lambda i: (0, 0))
    return pl.pallas_call(
        _post_body,
        grid=(n // tm,),
        in_specs=[tok(RW_W), tok(RW_W), tok(RW_W), tok(RW_W), const(ln_w), const(ln_b), const(e), const(et),
                  tok(ML_VW), tok(ML_VW), pl.BlockSpec((tm, ML_VW), lambda i: (i, (2 * ML_QKW + ML_VW) // ML_VW)),
                  const(norm_w)],
        out_specs=[tok(RW_W), tok(ML_VW)],
        out_shape=[jax.ShapeDtypeStruct((n, RW_W), BF16), jax.ShapeDtypeStruct((n, ML_VW), BF16)],
        compiler_params=_params(("parallel",), 40),
    )(yf, yb, bonus, g, ln_w, ln_b, e, et, hf, hb, vo, norm_w)


def _merge_body(ya_ref, yb_ref, pa_ref, pb_ref, ga_ref, gb_ref, ba_ref, bb_ref, o_ref):
    pa = _mm(ya_ref[...], pa_ref[...])
    pb = _mm(yb_ref[...], pb_ref[...])
    ga = ga_ref[...].astype(F32) + ba_ref[...]
    gb = gb_ref[...].astype(F32) + bb_ref[...]
    o_ref[...] = (_sigmoid(ga) * pa + _sigmoid(gb) * pb).astype(o_ref.dtype)


def _merge(ya, yb, p_a, p_b, proj_b, b_gate):
    n = ya.shape[0]
    tm = min(1024, n)
    tn = 1024
    nj = D_MODEL // tn
    g0 = PB_GATE_OFF // tn
    return pl.pallas_call(
        _merge_body,
        grid=(n // tm, nj),
        in_specs=[pl.BlockSpec((tm, RW_W), lambda i, j: (i, 0)), pl.BlockSpec((tm, ML_VW), lambda i, j: (i, 0)),
                  pl.BlockSpec((RW_W, tn), lambda i, j: (0, j)), pl.BlockSpec((ML_VW, tn), lambda i, j: (0, j)),
                  pl.BlockSpec((tm, tn), lambda i, j: (i, g0 + j)), pl.BlockSpec((tm, tn), lambda i, j: (i, g0 + nj + j)),
                  pl.BlockSpec((1, tn), lambda i, j: (0, j)), pl.BlockSpec((1, tn), lambda i, j: (0, j + nj))],
        out_specs=pl.BlockSpec((tm, tn), lambda i, j: (i, j)),
        out_shape=jax.ShapeDtypeStruct((n, D_MODEL), BF16),
        compiler_params=_params(("parallel", "parallel"), 40),
    )(ya, yb, p_a, p_b, proj_b, proj_b, b_gate, b_gate)


def _resid_mm_body(a_ref, w_ref, x_ref, o_ref):
    o_ref[...] = x_ref[...] + _mm(a_ref[...], w_ref[...])


def _resid_mm(a, w, x):
    n, kdim = a.shape
    c = w.shape[1]
    tm = min(1024, n)
    tn = 1024
    return pl.pallas_call(
        _resid_mm_body,
        grid=(n // tm, c // tn),
        in_specs=[pl.BlockSpec((tm, kdim), lambda i, j: (i, 0)), pl.BlockSpec((kdim, tn), lambda i, j: (0, j)),
                  pl.BlockSpec((tm, tn), lambda i, j: (i, j))],
        out_specs=pl.BlockSpec((tm, tn), lambda i, j: (i, j)),
        out_shape=jax.ShapeDtypeStruct((n, c), F32),
        compiler_params=_params(("parallel", "parallel"), 48),
    )(a, w, x)


def _topk_rows(s, k, payload=None):
    nrow, t = s.shape
    rid = lax.broadcasted_iota(jnp.int32, (nrow, t), 0).astype(F32)
    kid = lax.broadcasted_iota(jnp.int32, (k, t), 0)
    vals = jnp.zeros((k, t), F32)
    sel = jnp.zeros((k, t), F32)
    for j in range(k):
        m = jnp.max(s, axis=0, keepdims=True)
        pos = jnp.min(jnp.where(s == m, rid, float(nrow)), axis=0, keepdims=True)
        hit = rid == pos
        if payload is None:
            picked = pos
        else:
            picked = jnp.max(jnp.where(hit, payload, -1.0), axis=0, keepdims=True)
        vals = jnp.where(kid == j, m, vals)
        sel = jnp.where(kid == j, picked, sel)
        s = jnp.where(hit, -jnp.inf, s)
    return vals, sel


def _router_body(q_ref, keys_ref, a_ref, b_ref, gw_ref):
    K = PEER_TOPK
    a_parts, b_parts, w_parts = [], [], []
    for h in range(PEER_HEADS):
        sv, si = [], []
        for p in range(2):
            c0 = (h * 2 + p) * PEER_HALF
            st = _dot3(keys_ref[h * 2 + p], q_ref[:, c0:c0 + PEER_HALF], NT)
            vals, idx = _topk_rows(st, K)
            sv.append(vals)
            si.append(idx)
        jid = lax.broadcasted_iota(jnp.int32, (SUBLANES, sv[0].shape[1]), 0)
        c_parts = [sv[0][0:1, :] + sv[1]]
        i_parts = [si[0][0:1, :] * float(PEER_NKEYS) + si[1]]
        for i in range(1, SUBLANES):
            keep = jid < K // (i + 1)
            c_parts.append(jnp.where(keep, sv[0][i:i + 1, :] + sv[1][0:SUBLANES, :], -jnp.inf))
            i_parts.append(si[0][i:i + 1, :] * float(PEER_NKEYS) + si[1][0:SUBLANES, :])
        c_parts.append(sv[0][SUBLANES:K, :] + sv[1][0:1, :])
        i_parts.append(si[0][SUBLANES:K, :] * float(PEER_NKEYS) + si[1][0:1, :])
        cand = jnp.concatenate(c_parts, axis=0)
        cidx = jnp.concatenate(i_parts, axis=0)
        best, eidx = _topk_rows(cand, K, payload=cidx)
        ex = jnp.exp(best - best[0:1, :])
        w_parts.append(ex / jnp.sum(ex, axis=0, keepdims=True))
        hi = jnp.floor(eidx * (1.0 / PEER_NKEYS))
        a_parts.append(hi)
        b_parts.append(eidx - hi * float(PEER_NKEYS))
    a_ref[...] = jnp.concatenate(a_parts, axis=0).T
    b_ref[...] = jnp.concatenate(b_parts, axis=0).T
    gw_ref[...] = jnp.concatenate(w_parts, axis=0).T


def _router(q, keys):
    n = q.shape[0]
    tq = min(256, n)
    tok = pl.BlockSpec((tq, PEER_HK), lambda i: (i, 0))
    sds = jax.ShapeDtypeStruct((n, PEER_HK), F32)
    return pl.pallas_call(
        _router_body,
        grid=(n // tq,),
        in_specs=[pl.BlockSpec((tq, q.shape[1]), lambda i: (i, 0)),
                  pl.BlockSpec(keys.shape, lambda i: (0, 0, 0))],
        out_specs=[tok, tok, tok],
        out_shape=[sds, sds, sds],
        compiler_params=_params(("parallel",), 32),
    )(q, keys)


PEER_CI = 4
PEER_CE = PEER_CI * PEER_NKEYS


def _peer_body(x_ref, gn_ref, a_ref, b_ref, w_ref, ut0_ref, utn_ref, v_ref, gf_ref, o_ref, h_ref, g2_ref, z_ref, *,
               pitch, final_norm):
    j = pl.program_id(1)
    last = pl.num_programs(1) - 1
    t = x_ref.shape[0]
    cur = j % 2

    @pl.when(j == 0)
    def _():
        xf = x_ref[...]
        h = xf * lax.rsqrt(jnp.mean(xf * xf, axis=-1, keepdims=True) + RMS_EPS) * gn_ref[...]
        h_ref[...] = h.astype(BF16)
        o_ref[...] = xf
        kid = lax.broadcasted_iota(jnp.int32, (PEER_NKEYS, PEER_HK), 0).astype(F32)

        def per_token(ti, carry):
            arow = a_ref[pl.ds(ti, 1), :]
            brow = b_ref[pl.ds(ti, 1), :]
            wrow = w_ref[pl.ds(ti, 1), :]
            w_hi = wrow.astype(BF16).astype(F32)
            w_lo = wrow - w_hi
            hit_a = kid == arow
            lhs = jnp.concatenate([jnp.where(hit_a, w_hi, 0.0), jnp.where(hit_a, w_lo, 0.0)], axis=1).astype(BF16)
            ohb = jnp.where(kid == brow, 1.0, 0.0).astype(BF16)
            rhs = jnp.concatenate([ohb, ohb], axis=1)
            g_t = _mm(lhs, rhs, NT)
            g2_ref[pl.ds(ti, PEER_NKEYS, stride=pitch), :] = g_t
            return carry

        lax.fori_loop(0, t, per_token, 0, unroll=8)
        z_ref[0] = _mm(h_ref[...], ut0_ref[...])

    z_ref[1 - cur] = _mm(h_ref[...], utn_ref[...])
    z = z_ref[cur]
    parts = []
    for ii in range(PEER_CI):
        start = pl.multiple_of((j * PEER_CI + ii) * pitch, SUBLANES)
        gblk = g2_ref[pl.ds(start, t), :]
        zz = z[:, ii * PEER_NKEYS:(ii + 1) * PEER_NKEYS]
        act = 0.5 * zz * (1.0 + lax.erf(zz * (2.0 ** -0.5)))
        parts.append((act * gblk).astype(BF16))
    o_ref[...] += _mm(jnp.concatenate(parts, axis=1), v_ref[...])

    if final_norm:
        @pl.when(j == last)
        def _():
            y = o_ref[...]
            o_ref[...] = y * lax.rsqrt(jnp.mean(y * y, axis=-1, keepdims=True) + RMS_EPS) * gf_ref[...]


def _peer(x, g_ffn, a_idx, b_idx, gw, u_t, v_tab, g_final, final_norm):
    n, d = x.shape
    t = min(256, n)
    pitch = t + SUBLANES
    nch = PEER_N // PEER_CE
    tok = lambda c: pl.BlockSpec((t, c), lambda i, j: (i, 0))
    return pl.pallas_call(
        functools.partial(_peer_body, pitch=pitch, final_norm=final_norm),
        grid=(n // t, nch),
        in_specs=[tok(d), pl.BlockSpec((1, d), lambda i, j: (0, 0)), tok(PEER_HK), tok(PEER_HK), tok(PEER_HK),
                  pl.BlockSpec((d, PEER_CE), lambda i, j: (0, 0)),
                  pl.BlockSpec((d, PEER_CE), lambda i, j: (0, jnp.minimum(j + 1, nch - 1))),
                  pl.BlockSpec((PEER_CE, d), lambda i, j: (j, 0)),
                  pl.BlockSpec((1, d), lambda i, j: (0, 0))],
        out_specs=tok(d),
        out_shape=jax.ShapeDtypeStruct((n, d), F32),
        scratch_shapes=[pltpu.VMEM((t, d), BF16), pltpu.VMEM((PEER_NKEYS * pitch, LANES), F32),
                        pltpu.VMEM((2, t, PEER_CE), F32)],
        compiler_params=_params(("parallel", "arbitrary"), 56),
    )(x, g_ffn, a_idx, b_idx, gw, u_t, u_t, v_tab, g_final)


def _pad_cols(pieces, width):
    rows = pieces[0][0].shape[0]
    out = jnp.zeros((rows, width), pieces[0][0].dtype)
    for arr, off in pieces:
        out = out.at[:, off:off + arr.shape[1]].set(arr)
    return out


def _pad_rows(x, rows):
    return jnp.zeros((rows,) + x.shape[1:], x.dtype).at[:x.shape[0]].set(x)


def _lora_layout(x):
    o = 3 * RW_W
    return _pad_cols([(x[:, o:o + 64], 0), (x[:, o + 64:o + 128], 128), (x[:, o + 128:o + 288], 256)], 512)


def _prep_layer(l, w):
    f = {}
    w_in = w["w_in"][l]
    rw = w_in[:, :RW_COLS]
    ml = w_in[:, RW_COLS:RW_COLS + ML_COLS]
    gt = w_in[:, RW_COLS + ML_COLS:]
    f["norm_mix"] = w["norm_mix"][l][None, :]
    f["w_f"] = _pad_cols([(rw[:, :3 * RW_W], 0), (_lora_layout(rw), PF_LORA_OFF),
                          (ml[:, 2 * ML_QKW + 2 * ML_VW:], PF_MLG_OFF)], PF_COLS).astype(BF16)
    f["w_b"] = jnp.concatenate([ml[:, :2 * ML_QKW + 2 * ML_VW], gt], axis=1).astype(BF16)
    mu = jnp.stack([w["rw_mu_prev"][l], w["rw_mu_next"][l]])
    f["mu_main"] = mu[:, :3 * RW_W]
    f["mu_lora"] = _lora_layout(mu)
    f["w0"] = w["rw_w0"][l]
    f["w2p"] = jnp.stack([_pad_rows(w["rw_w2"][l, 0], 128), _pad_rows(w["rw_w2"][l, 1], 128)])
    f["a0"] = w["rw_a0"][l][None, :]
    f["a2p"] = _pad_rows(w["rw_a2"][l], 128)
    f["g2p"] = _pad_rows(w["rw_g2"][l], 256)
    f["k_k"] = w["rw_k_k"][l][None, :]
    f["k_a"] = w["rw_k_a"][l][None, :]
    f["r_k"] = w["rw_r_k"][l].reshape(1, RW_W)
    f["ln_w"] = w["rw_ln_w"][l][None, :]
    f["ln_b"] = w["rw_ln_b"][l][None, :]
    f["ml_bias"] = _pad_cols([(w["ml_b_i"][l].reshape(1, -1), 0), (w["ml_b_f"][l].reshape(1, -1), 2 * ML_HEADS)], LANES)
    f["ml_norm_w"] = w["ml_norm_w"][l][None, :]
    f["p_a"] = w["p_a"][l].astype(BF16)
    f["p_b"] = w["p_b"][l].astype(BF16)
    f["b_gate"] = w["b_gate"][l][None, :]
    f["w_out"] = w["w_out"][l].astype(BF16)
    f["norm_ffn"] = w["norm_ffn"][l][None, :]
    wq = w["peer_wq"][l]
    wq_hi = wq.astype(BF16)
    f["wq"] = (wq_hi, (wq - wq_hi.astype(F32)).astype(BF16))
    f["keys"] = w["peer_keys"][l].reshape(PEER_HEADS * 2, PEER_NKEYS, PEER_HALF)
    f["u_t"] = w["peer_u"][l].astype(BF16).T
    f["v"] = w["peer_v"][l].astype(BF16)
    return f


def _head_selectors():
    ch = jnp.arange(RW_W)[:, None] // RW_HEAD
    e = (ch == jnp.arange(LANES)[None, :]).astype(BF16)
    return e, e.T


def _layer(x, f, e, et, g_final, final_norm):
    b, t, d = x.shape
    n = b * t
    x2 = x.reshape(n, d)
    (hn,) = _norm_cast(x2, f["norm_mix"], 1)
    p_f = _matmul((hn,), (f["w_f"],), F32)
    p_b = _matmul((hn,), (f["w_b"],), BF16)
    p_f3 = p_f.reshape(b, t, -1)
    p_b3 = p_b.reshape(b, t, -1)

    r, k, v, kn, ab, lw0, lw1, gg, bonus = _rw_prep(
        p_f3, f["mu_main"], f["mu_lora"], f["w0"], f["w2p"], f["a0"], f["a2p"], f["g2p"], f["k_k"], f["k_a"],
        f["r_k"], e, et)
    y_f = _rw_scan(r, k, v, kn, ab, lw0, rev=False)
    y_b = _rw_scan(r, k, v, kn, ab, lw1, rev=True)

    h_f = _ml_scan(p_b3, p_f3, f["ml_bias"], rev=False)
    h_b = _ml_scan(p_b3, p_f3, f["ml_bias"], rev=True)

    ya, yb = _post(y_f.reshape(n, -1), y_b.reshape(n, -1), bonus.reshape(n, -1), gg.reshape(n, -1),
                   f["ln_w"], f["ln_b"], e, et, h_f.reshape(n, -1), h_b.reshape(n, -1), p_b, f["ml_norm_w"])
    merged = _merge(ya, yb, f["p_a"], f["p_b"], p_b, f["b_gate"])
    x1 = _resid_mm(merged, f["w_out"], x2)

    q = _matmul(_norm_cast(x1, f["norm_ffn"], 2), f["wq"], F32)
    a_idx, b_idx, gw = _router(q, f["keys"])
    x_out = _peer(x1, f["norm_ffn"], a_idx, b_idx, gw, f["u_t"], f["v"], g_final, final_norm)
    return x_out.reshape(b, t, d)


def _trunk(x, layers, e, et, g_final):
    for l, f in enumerate(layers):
        x = _layer(x, f, e, et, g_final, final_norm=(l == len(layers) - 1))
    return x


def kernel(x_prompt, x_sample, norm_mix, w_in, rw_mu_prev, rw_mu_next, rw_w0, rw_w2, rw_a0, rw_a2, rw_g2, rw_k_k, rw_k_a, rw_r_k, rw_ln_w, rw_ln_b, ml_b_i, ml_b_f, ml_norm_w, p_a, p_b, b_gate, w_out, norm_ffn, peer_wq, peer_keys, peer_u, peer_v, norm_final):
    w = dict(norm_mix=norm_mix, w_in=w_in, rw_mu_prev=rw_mu_prev, rw_mu_next=rw_mu_next, rw_w0=rw_w0, rw_w2=rw_w2,
             rw_a0=rw_a0, rw_a2=rw_a2, rw_g2=rw_g2, rw_k_k=rw_k_k, rw_k_a=rw_k_a, rw_r_k=rw_r_k, rw_ln_w=rw_ln_w,
             rw_ln_b=rw_ln_b, ml_b_i=ml_b_i, ml_b_f=ml_b_f, ml_norm_w=ml_norm_w, p_a=p_a, p_b=p_b, b_gate=b_gate,
             w_out=w_out, norm_ffn=norm_ffn, peer_wq=peer_wq, peer_keys=peer_keys, peer_u=peer_u, peer_v=peer_v)
    depth = w_in.shape[0]
    layers = [_prep_layer(l, w) for l in range(depth)]
    e, et = _head_selectors()
    g_final = norm_final[None, :]
    return (_trunk(x_prompt, layers, e, et, g_final), _trunk(x_sample, layers, e, et, g_final))
```

```python
import functools
import math

import jax
import jax.numpy as jnp
from jax import lax
from jax.experimental import pallas as pl
from jax.experimental.pallas import tpu as pltpu

F32 = jnp.float32
BF16 = jnp.bfloat16

D_MODEL = 2048
RMS_EPS = 1e-6
RW_W = 1024
RW_HEAD = 64
RW_HEADS = 16
RW_DECAY_LORA = 64
RW_A_LORA = 64
RW_G_LORA = 160
RW_GN_EPS = 64e-5
RW_COLS = 3 * RW_W + RW_DECAY_LORA + RW_A_LORA + RW_G_LORA
RW_CHUNK = 64
RW_SUB = 2
RW_PASSES_A = 1
RW_PASSES_INV = 1
RW_PASSES_S = 1
ML_HEADS = 4
ML_QK = 128
ML_V = 256
ML_QKW = ML_HEADS * ML_QK
ML_VW = ML_HEADS * ML_V
ML_CHUNK = 128
ML_NORM_EPS = 1e-6
ML_COLS = 2 * ML_QKW + 2 * ML_VW + 4 * ML_HEADS
PF_LORA_OFF = 3 * RW_W
PF_LORA_W = 512
PF_MLG_OFF = PF_LORA_OFF + PF_LORA_W
PF_COLS = 4096
PB_GATE_OFF = 2 * ML_QKW + 2 * ML_VW
PB_COLS = PB_GATE_OFF + 2 * D_MODEL
PEER_HEADS = 8
PEER_NKEYS = 128
PEER_N = PEER_NKEYS * PEER_NKEYS
PEER_HALF = 128
PEER_TOPK = 16
PEER_HK = PEER_HEADS * PEER_TOPK
LANES = 128
SUBLANES = 8

NN = ((1,), (0,))
NT = ((1,), (1,))


def _mm(a, b, dims=NN):
    return lax.dot_general(a, b, (dims, ((), ())), preferred_element_type=F32)


def _split2(a):
    hi = a.astype(BF16)
    lo = (a - hi.astype(F32)).astype(BF16)
    return hi, lo


def _split3(a):
    hi = a.astype(BF16)
    r1 = a - hi.astype(F32)
    mid = r1.astype(BF16)
    lo = (r1 - mid.astype(F32)).astype(BF16)
    return hi, mid, lo


def _dot1(a, b, dims=NN):
    return _mm(a.astype(BF16), b.astype(BF16), dims)


def _dot3(a, b, dims=NN):
    ah, al = _split2(a)
    bh, bl = _split2(b)
    return _mm(ah, bh, dims) + (_mm(ah, bl, dims) + _mm(al, bh, dims))


def _dotp(a, b, dims, passes):
    return _dot3(a, b, dims) if passes == 3 else _dot1(a, b, dims)


def _dot_sel(sel_bf16, x, dims=NN):
    h, m, l = _split3(x)
    return _mm(sel_bf16, h, dims) + (_mm(sel_bf16, m, dims) + _mm(sel_bf16, l, dims))


def _dot_xsel(x, sel_bf16, dims=NN):
    h, m, l = _split3(x)
    return _mm(h, sel_bf16, dims) + (_mm(m, sel_bf16, dims) + _mm(l, sel_bf16, dims))


def _params(sem, vmem_mb):
    return pltpu.CompilerParams(dimension_semantics=sem, vmem_limit_bytes=vmem_mb << 20)


def _sigmoid(x):
    return 1.0 / (1.0 + jnp.exp(-x))


def _norm_cast_body(x_ref, g_ref, *o_refs):
    xf = x_ref[...]
    h = xf * lax.rsqrt(jnp.mean(xf * xf, axis=-1, keepdims=True) + RMS_EPS) * g_ref[...]
    hi = h.astype(BF16)
    o_refs[0][...] = hi
    if len(o_refs) == 2:
        o_refs[1][...] = (h - hi.astype(F32)).astype(BF16)


def _norm_cast(x, g, n_parts):
    n, d = x.shape
    tm = min(512, n)
    spec = pl.BlockSpec((tm, d), lambda i: (i, 0))
    return pl.pallas_call(
        _norm_cast_body,
        grid=(n // tm,),
        in_specs=[spec, pl.BlockSpec((1, d), lambda i: (0, 0))],
        out_specs=[spec] * n_parts,
        out_shape=[jax.ShapeDtypeStruct((n, d), BF16)] * n_parts,
        compiler_params=_params(("parallel",), 32),
    )(x, g)


def _matmul_body(*refs, n_parts):
    a_refs = refs[:n_parts]
    w_refs = refs[n_parts:2 * n_parts]
    o_ref = refs[2 * n_parts]
    acc = _mm(a_refs[0][...], w_refs[0][...])
    if n_parts == 2:
        acc = acc + (_mm(a_refs[0][...], w_refs[1][...]) + _mm(a_refs[1][...], w_refs[0][...]))
    o_ref[...] = acc.astype(o_ref.dtype)


def _matmul(a_parts, w_parts, out_dtype):
    n, d = a_parts[0].shape
    c = w_parts[0].shape[1]
    n_parts = len(a_parts)
    tm = min(1024, n)
    tn = min(1024 // n_parts, c)
    return pl.pallas_call(
        functools.partial(_matmul_body, n_parts=n_parts),
        grid=(n // tm, c // tn),
        in_specs=[pl.BlockSpec((tm, d), lambda i, j: (i, 0))] * n_parts
        + [pl.BlockSpec((d, tn), lambda i, j: (0, j))] * n_parts,
        out_specs=pl.BlockSpec((tm, tn), lambda i, j: (i, j)),
        out_shape=jax.ShapeDtypeStruct((n, c), out_dtype),
        compiler_params=_params(("parallel", "parallel"), 48),
    )(*a_parts, *w_parts)


def _shift_rows(p, prev_row, next_row, mu_prev, mu_next):
    n = p.shape[0]
    rid = lax.broadcasted_iota(jnp.int32, p.shape, 0)
    prev = jnp.where(rid == 0, prev_row, pltpu.roll(p, 1, 0))
    nxt = jnp.where(rid == n - 1, next_row, pltpu.roll(p, n - 1, 0))
    return p + mu_prev * (prev - p) + mu_next * (nxt - p)


def _rw_prep_body(m_ref, mp_ref, mn_ref, l_ref, lp_ref, ln_ref, mum_ref, mul_ref, w0_ref, w2_ref,
                  a0_ref, a2_ref, g2_ref, kk_ref, ka_ref, rk_ref, e_ref, et_ref,
                  r_o, k_o, v_o, kn_o, ab_o, lw0_o, lw1_o, g_o, bo_o):
    i = pl.program_id(1)
    first = i == 0
    last = i == pl.num_programs(1) - 1
    zero_m = jnp.zeros((1, m_ref.shape[2]), F32)
    zero_l = jnp.zeros((1, l_ref.shape[2]), F32)
    pm = _shift_rows(m_ref[0],
                     jnp.where(first, zero_m, mp_ref[0, SUBLANES - 1:SUBLANES, :]),
                     jnp.where(last, zero_m, mn_ref[0, 0:1, :]),
                     mum_ref[0:1, :], mum_ref[1:2, :])
    plo = _shift_rows(l_ref[0],
                      jnp.where(first, zero_l, lp_ref[0, SUBLANES - 1:SUBLANES, :]),
                      jnp.where(last, zero_l, ln_ref[0, 0:1, :]),
                      mul_ref[0:1, :], mul_ref[1:2, :])
    r = pm[:, 0:RW_W]
    k = pm[:, RW_W:2 * RW_W]
    v = pm[:, 2 * RW_W:3 * RW_W]
    wd = plo[:, 0:128]
    ad = plo[:, 128:256]
    gd = plo[:, 256:512]

    a = _sigmoid(a0_ref[...] + _dot3(ad, a2_ref[...]))
    g = _dot3(_sigmoid(gd), g2_ref[...])
    e = e_ref[...]
    et = et_ref[...]
    kk = k * kk_ref[...]
    ss = _dot_xsel(_dot_xsel(kk * kk, e), et)
    kn = kk / jnp.maximum(jnp.sqrt(ss), 1e-12)
    kmod = k * (1.0 + (a - 1.0) * ka_ref[...])
    wl = jnp.tanh(wd)
    c = math.exp(-0.5)
    lw0 = -_sigmoid(w0_ref[0:1, :] + _dot3(wl, w2_ref[0])) * c
    lw1 = -_sigmoid(w0_ref[1:2, :] + _dot3(wl, w2_ref[1])) * c
    rks = _dot_xsel(_dot_xsel(r * kmod * rk_ref[...], e), et)
    r_o[0] = r
    k_o[0] = kmod
    v_o[0] = v
    kn_o[0] = kn
    ab_o[0] = kn * a
    lw0_o[0] = lw0
    lw1_o[0] = lw1
    g_o[0] = g
    bo_o[0] = rks * v


def _rw_prep(p_f, mu_main, mu_lora, w0, w2p, a0, a2p, g2p, k_k, k_a, r_k, e, et):
    b, t, _ = p_f.shape
    cm, cl = 3 * RW_W, PF_LORA_W
    col = {cm: 0, cl: PF_LORA_OFF // PF_LORA_W}
    tt = min(128, t)
    nb = tt // SUBLANES
    nt8 = t // SUBLANES

    def main_spec(c):
        return pl.BlockSpec((1, tt, c), lambda bi, i: (bi, i, col[c]))

    def prev_spec(c):
        return pl.BlockSpec((1, SUBLANES, c), lambda bi, i: (bi, jnp.maximum(i * nb - 1, 0), col[c]))

    def next_spec(c):
        return pl.BlockSpec((1, SUBLANES, c), lambda bi, i: (bi, jnp.minimum((i + 1) * nb, nt8 - 1), col[c]))

    def full(x):
        nd = x.ndim
        return pl.BlockSpec(x.shape, lambda bi, i: (0,) * nd)

    consts = (mu_main, mu_lora, w0, w2p, a0, a2p, g2p, k_k, k_a, r_k, e, et)
    out_sds = jax.ShapeDtypeStruct((b, t, RW_W), F32)
    return pl.pallas_call(
        _rw_prep_body,
        grid=(b, t // tt),
        in_specs=[main_spec(cm), prev_spec(cm), next_spec(cm), main_spec(cl), prev_spec(cl), next_spec(cl)]
        + [full(x) for x in consts],
        out_specs=[pl.BlockSpec((1, tt, RW_W), lambda bi, i: (bi, i, 0))] * 9,
        out_shape=[out_sds] * 9,
        compiler_params=_params(("parallel", "parallel"), 56),
    )(p_f, p_f, p_f, p_f, p_f, p_f, *consts)


def _rw_scan_body(r_ref, k_ref, v_ref, kn_ref, ab_ref, lw_ref, y_ref, s_ref, *, rev):
    L = RW_CHUNK

    @pl.when(pl.program_id(1) == 0)
    def _():
        s_ref[...] = jnp.zeros_like(s_ref)

    row = lax.broadcasted_iota(jnp.int32, (L, L), 0)
    col = lax.broadcasted_iota(jnp.int32, (L, L), 1)
    if rev:
        incl = col >= row
        strict = col > row
    else:
        incl = col <= row
        strict = col < row
    tri = jnp.where(incl, 1.0, 0.0).astype(BF16)
    eye = jnp.where(row == col, 1.0, 0.0).astype(F32)
    lane = lax.broadcasted_iota(jnp.int32, (L, LANES), 1)
    head_masks = (lane < RW_HEAD, lane >= RW_HEAD)
    sr = lax.broadcasted_iota(jnp.int32, (LANES, LANES), 0)
    sc = lax.broadcasted_iota(jnp.int32, (LANES, LANES), 1)
    bd = (sr < RW_HEAD) == (sc < RW_HEAD)

    heads = [(hp, h) for hp in range(RW_HEADS // 2) for h in range(2)]
    sls = [slice(hp * LANES, (hp + 1) * LANES) for hp in range(RW_HEADS // 2)]

    def intra(off):
        rows = slice(off, off + L)
        lw = lw_ref[0, rows, :]
        cum = _dot_sel(tri, lw)
        p_in = jnp.exp(cum)
        p_ex = jnp.exp(cum - lw)
        p_inv = jnp.exp(-cum)
        tot = cum[0:1, :] if rev else cum[L - 1:L, :]
        p_end = jnp.exp(tot - cum)
        ab = ab_ref[0, rows, :]
        k = k_ref[0, rows, :]
        at = -kn_ref[0, rows, :] * p_ex
        rt = r_ref[0, rows, :] * p_in
        bt = ab * p_inv
        kt = k * p_inv
        v = v_ref[0, rows, :]
        at_h, a_ab, a_ak, a_rb, a_rk = [], [], [], [], []
        for hp, h in heads:
            sl, msk = sls[hp], head_masks[h]
            at_m = jnp.where(msk, at[:, sl], 0.0)
            lhs = jnp.concatenate([at_m, jnp.where(msk, rt[:, sl], 0.0)], axis=0)
            ab_blk = _dotp(lhs, bt[:, sl], NT, RW_PASSES_A)
            ak_blk = _dotp(lhs, kt[:, sl], NT, RW_PASSES_A)
            at_h.append(at_m)
            a_ab.append(jnp.where(strict, ab_blk[:L], 0.0))
            a_ak.append(jnp.where(strict, ak_blk[:L], 0.0))
            a_rb.append(jnp.where(incl, ab_blk[L:], 0.0))
            a_rk.append(jnp.where(incl, ak_blk[L:], 0.0))
        tm = [eye + a for a in a_ab]
        pw = a_ab
        for _ in range(5):
            pw = [_dotp(p, p, NN, RW_PASSES_INV) for p in pw]
            tm = [t_ + _dotp(t_, p, NN, RW_PASSES_INV) for t_, p in zip(tm, pw)]
        av = [_dotp(jnp.concatenate([a_ak[i], a_rk[i]], axis=0), v[:, sls[hp]], NN, RW_PASSES_A)
              for i, (hp, h) in enumerate(heads)]
        tx = [_dotp(tm[i], jnp.concatenate([at_h[i], jnp.where(head_masks[h], av[i][:L], 0.0)], axis=1), NN,
                    RW_PASSES_A) for i, (hp, h) in enumerate(heads)]
        return dict(tx=tx, av=av, a_rb=a_rb, rt=rt, v=v, bt_end=ab * p_end, kt_end=k * p_end, p_all=jnp.exp(tot))

    def sequential(c, off, states):
        tx, av, a_rb = c["tx"], c["av"], c["a_rb"]
        out = []
        for hp in range(RW_HEADS // 2):
            sl = sls[hp]
            i0, i1 = 2 * hp, 2 * hp + 1
            w_pair = tx[i0][:, :LANES] + tx[i1][:, :LANES]
            u0_pair = tx[i0][:, LANES:] + tx[i1][:, LANES:]
            y0_pair = jnp.where(head_masks[0], av[i0][L:], av[i1][L:])
            s = states[hp]
            ws = _dotp(jnp.concatenate([w_pair, c["rt"][:, sl]], axis=0), s, NT, RW_PASSES_S)
            u = ws[:L] + u0_pair
            y = ws[L:] + y0_pair + jnp.where(head_masks[0], _dotp(a_rb[i0], u, NN, RW_PASSES_A),
                                             _dotp(a_rb[i1], u, NN, RW_PASSES_A))
            uv_t = jnp.concatenate([u, c["v"][:, sl]], axis=0).T
            upd = _dotp(uv_t, jnp.concatenate([c["bt_end"][:, sl], c["kt_end"][:, sl]], axis=0), NN, RW_PASSES_S)
            out.append(s * c["p_all"][:, sl] + jnp.where(bd, upd, 0.0))
            y_ref[0, off:off + L, sl] = y
        return out

    offs = [c * L for c in range(RW_SUB)]
    if rev:
        offs = offs[::-1]
    chunks = [intra(off) for off in offs]
    states = [s_ref[hp] for hp in range(RW_HEADS // 2)]
    for c, off in zip(chunks, offs):
        states = sequential(c, off, states)
    for hp in range(RW_HEADS // 2):
        s_ref[hp] = states[hp]


def _rw_scan(r, k, v, kn, ab, lw, rev):
    b, t, w = r.shape
    tb = RW_CHUNK * RW_SUB
    nc = t // tb
    if rev:
        idx = lambda bi, c: (bi, nc - 1 - c, 0)
    else:
        idx = lambda bi, c: (bi, c, 0)
    spec = pl.BlockSpec((1, tb, w), idx)
    return pl.pallas_call(
        functools.partial(_rw_scan_body, rev=rev),
        grid=(b, nc),
        in_specs=[spec] * 6,
        out_specs=spec,
        out_shape=jax.ShapeDtypeStruct((b, t, w), F32),
        scratch_shapes=[pltpu.VMEM((RW_HEADS // 2, LANES, LANES), F32)],
        compiler_params=_params(("parallel", "arbitrary"), 32),
    )(r, k, v, kn, ab, lw)


def _ml_scan_body(q_ref, k_ref, v_ref, g_ref, b_ref, h_ref, c_ref, n_ref, m_ref, *, rev, direction):
    L = ML_CHUNK

    @pl.when(pl.program_id(1) == 0)
    def _():
        c_ref[...] = jnp.zeros_like(c_ref)
        n_ref[...] = jnp.zeros_like(n_ref)
        m_ref[...] = jnp.zeros_like(m_ref)

    row = lax.broadcasted_iota(jnp.int32, (L, L), 0)
    col = lax.broadcasted_iota(jnp.int32, (L, L), 1)
    incl = (col >= row) if rev else (col <= row)
    tri = jnp.where(incl, 1.0, 0.0).astype(BF16)
    lane = lax.broadcasted_iota(jnp.int32, (L, LANES), 1)
    sub = lax.broadcasted_iota(jnp.int32, (LANES, L), 0)

    g = g_ref[0] + b_ref[...]
    ls = jnp.minimum(g, 0.0) - jnp.log(1.0 + jnp.exp(-jnp.abs(g)))
    bc = _dot_sel(tri, ls)
    g_t = g.T
    bc_t = bc.T
    scale = ML_QK ** -0.5
    for h in range(ML_HEADS):
        ci = direction * ML_HEADS + h
        cf = 2 * ML_HEADS + direction * ML_HEADS + h
        b_col = jnp.sum(jnp.where(lane == cf, bc, 0.0), axis=1, keepdims=True)
        i_col = jnp.sum(jnp.where(lane == ci, g, 0.0), axis=1, keepdims=True)
        b_row = jnp.sum(jnp.where(sub == cf, bc_t, 0.0), axis=0, keepdims=True)
        i_row = jnp.sum(jnp.where(sub == ci, g_t, 0.0), axis=0, keepdims=True)
        m_prev = jnp.max(m_ref[h:h + 1, :], axis=1, keepdims=True)
        qs = q_ref[0, :, h * ML_QK:(h + 1) * ML_QK].astype(F32) * scale
        kc = k_ref[0, :, h * ML_QK:(h + 1) * ML_QK].astype(F32)
        vc = v_ref[0, :, h * ML_V:(h + 1) * ML_V].astype(F32)
        dmat = jnp.where(incl, b_col - b_row + i_row, -jnp.inf)
        inter = b_col + m_prev
        m_t = jnp.maximum(inter, jnp.max(dmat, axis=1, keepdims=True))
        s = _dot1(qs, kc, NT) * jnp.exp(dmat - m_t)
        e_inter = jnp.exp(inter - m_t)
        num = _dot1(s, vc) + e_inter * _dot1(qs, c_ref[h])
        den = jnp.sum(s, axis=1, keepdims=True) + e_inter * jnp.sum(qs * n_ref[h:h + 1, :], axis=1, keepdims=True)
        h_ref[0, :, h * ML_V:(h + 1) * ML_V] = num / jnp.maximum(jnp.abs(den), jnp.exp(-m_t))
        b_last = jnp.min(b_row, axis=1, keepdims=True)
        g_row = b_last - b_row + i_row
        g_col = b_last - b_col + i_col
        m_new = jnp.maximum(b_last + m_prev, jnp.max(g_row, axis=1, keepdims=True))
        w_col = jnp.exp(g_col - m_new)
        dec = jnp.exp(b_last + m_prev - m_new)
        c_ref[h] = dec * c_ref[h] + _dot1(kc.T, vc * w_col)
        n_ref[h:h + 1, :] = dec * n_ref[h:h + 1, :] + jnp.sum(kc * w_col, axis=0, keepdims=True)
        m_ref[h:h + 1, :] = jnp.broadcast_to(m_new, (1, LANES))


def _ml_scan(p_b, p_f, bias, rev):
    b, t, _ = p_b.shape
    L = ML_CHUNK
    nc = t // L
    tix = (lambda c: nc - 1 - c) if rev else (lambda c: c)
    return pl.pallas_call(
        functools.partial(_ml_scan_body, rev=rev, direction=1 if rev else 0),
        grid=(b, nc),
        in_specs=[pl.BlockSpec((1, L, ML_QKW), lambda bi, c: (bi, tix(c), 0)),
                  pl.BlockSpec((1, L, ML_QKW), lambda bi, c: (bi, tix(c), 1)),
                  pl.BlockSpec((1, L, ML_VW), lambda bi, c: (bi, tix(c), 1)),
                  pl.BlockSpec((1, L, LANES), lambda bi, c: (bi, tix(c), PF_MLG_OFF // LANES)),
                  pl.BlockSpec((1, LANES), lambda bi, c: (0, 0))],
        out_specs=pl.BlockSpec((1, L, ML_VW), lambda bi, c: (bi, tix(c), 0)),
        out_shape=jax.ShapeDtypeStruct((b, t, ML_VW), F32),
        scratch_shapes=[pltpu.VMEM((ML_HEADS, ML_QK, ML_V), F32),
                        pltpu.VMEM((SUBLANES, LANES), F32),
                        pltpu.VMEM((SUBLANES, LANES), F32)],
        compiler_params=_params(("parallel", "arbitrary"), 32),
    )(p_b, p_b, p_b, p_f, bias)


def _post_body(yf_ref, yb_ref, bo_ref, g_ref, lnw_ref, lnb_ref, e_ref, et_ref,
               hf_ref, hb_ref, o_ref, nw_ref, ya_ref, yb_o_ref):
    e = e_ref[...]
    et = et_ref[...]
    y = yf_ref[...] + yb_ref[...]
    inv = 1.0 / RW_HEAD
    mu = _dot_xsel(_dot_xsel(y, e), et) * inv
    yc = y - mu
    var = _dot_xsel(_dot_xsel(yc * yc, e), et) * inv
    yn = yc * lax.rsqrt(var + RW_GN_EPS) * lnw_ref[...] + lnb_ref[...]
    ya_ref[...] = ((yn + bo_ref[...]) * g_ref[...]).astype(ya_ref.dtype)

    hsum = hf_ref[...] + hb_ref[...]
    parts = []
    for h in range(ML_HEADS):
        hh = hsum[:, h * ML_V:(h + 1) * ML_V]
        parts.append(hh * lax.rsqrt(jnp.mean(hh * hh, axis=-1, keepdims=True) + ML_NORM_EPS))
    hn = jnp.concatenate(parts, axis=1) * nw_ref[...]
    yb_o_ref[...] = (hn * _sigmoid(o_ref[...].astype(F32))).astype(yb_o_ref.dtype)


def _post(yf, yb, bonus, g, ln_w, ln_b, e, et, hf, hb, vo, norm_w):
    n = yf.shape[0]
    tm = min(256, n)
    tok = lambda c: pl.BlockSpec((tm, c), lambda i: (i, 0))
    const = lambda x: pl.BlockSpec(x.shape, lambda i: (0, 0))
    return pl.pallas_call(
        _post_body,
        grid=(n // tm,),
        in_specs=[tok(RW_W), tok(RW_W), tok(RW_W), tok(RW_W), const(ln_w), const(ln_b), const(e), const(et),
                  tok(ML_VW), tok(ML_VW), pl.BlockSpec((tm, ML_VW), lambda i: (i, (2 * ML_QKW + ML_VW) // ML_VW)),
                  const(norm_w)],
        out_specs=[tok(RW_W), tok(ML_VW)],
        out_shape=[jax.ShapeDtypeStruct((n, RW_W), BF16), jax.ShapeDtypeStruct((n, ML_VW), BF16)],
        compiler_params=_params(("parallel",), 40),
    )(yf, yb, bonus, g, ln_w, ln_b, e, et, hf, hb, vo, norm_w)


def _merge_body(ya_ref, yb_ref, pa_ref, pb_ref, ga_ref, gb_ref, ba_ref, bb_ref, o_ref):
    pa = _mm(ya_ref[...], pa_ref[...])
    pb = _mm(yb_ref[...], pb_ref[...])
    ga = ga_ref[...].astype(F32) + ba_ref[...]
    gb = gb_ref[...].astype(F32) + bb_ref[...]
    o_ref[...] = (_sigmoid(ga) * pa + _sigmoid(gb) * pb).astype(o_ref.dtype)


def _merge(ya, yb, p_a, p_b, proj_b, b_gate):
    n = ya.shape[0]
    tm = min(1024, n)
    tn = 1024
    nj = D_MODEL // tn
    g0 = PB_GATE_OFF // tn
    return pl.pallas_call(
        _merge_body,
        grid=(n // tm, nj),
        in_specs=[pl.BlockSpec((tm, RW_W), lambda i, j: (i, 0)), pl.BlockSpec((tm, ML_VW), lambda i, j: (i, 0)),
                  pl.BlockSpec((RW_W, tn), lambda i, j: (0, j)), pl.BlockSpec((ML_VW, tn), lambda i, j: (0, j)),
                  pl.BlockSpec((tm, tn), lambda i, j: (i, g0 + j)), pl.BlockSpec((tm, tn), lambda i, j: (i, g0 + nj + j)),
                  pl.BlockSpec((1, tn), lambda i, j: (0, j)), pl.BlockSpec((1, tn), lambda i, j: (0, j + nj))],
        out_specs=pl.BlockSpec((tm, tn), lambda i, j: (i, j)),
        out_shape=jax.ShapeDtypeStruct((n, D_MODEL), BF16),
        compiler_params=_params(("parallel", "parallel"), 40),
    )(ya, yb, p_a, p_b, proj_b, proj_b, b_gate, b_gate)


def _resid_mm_body(a_ref, w_ref, x_ref, o_ref):
    o_ref[...] = x_ref[...] + _mm(a_ref[...], w_ref[...])


def _resid_mm(a, w, x):
    n, kdim = a.shape
    c = w.shape[1]
    tm = min(1024, n)
    tn = 1024
    return pl.pallas_call(
        _resid_mm_body,
        grid=(n // tm, c // tn),
        in_specs=[pl.BlockSpec((tm, kdim), lambda i, j: (i, 0)), pl.BlockSpec((kdim, tn), lambda i, j: (0, j)),
                  pl.BlockSpec((tm, tn), lambda i, j: (i, j))],
        out_specs=pl.BlockSpec((tm, tn), lambda i, j: (i, j)),
        out_shape=jax.ShapeDtypeStruct((n, c), F32),
        compiler_params=_params(("parallel", "parallel"), 48),
    )(a, w, x)


def _topk_rows(s, k, payload=None):
    nrow, t = s.shape
    rid = lax.broadcasted_iota(jnp.int32, (nrow, t), 0).astype(F32)
    kid = lax.broadcasted_iota(jnp.int32, (k, t), 0)
    vals = jnp.zeros((k, t), F32)
    sel = jnp.zeros((k, t), F32)
    for j in range(k):
        m = jnp.max(s, axis=0, keepdims=True)
        pos = jnp.min(jnp.where(s == m, rid, float(nrow)), axis=0, keepdims=True)
        hit = rid == pos
        if payload is None:
            picked = pos
        else:
            picked = jnp.max(jnp.where(hit, payload, -1.0), axis=0, keepdims=True)
        vals = jnp.where(kid == j, m, vals)
        sel = jnp.where(kid == j, picked, sel)
        s = jnp.where(hit, -jnp.inf, s)
    return vals, sel


def _router_body(q_ref, keys_ref, a_ref, b_ref, gw_ref):
    K = PEER_TOPK
    a_parts, b_parts, w_parts = [], [], []
    for h in range(PEER_HEADS):
        sv, si = [], []
        for p in range(2):
            c0 = (h * 2 + p) * PEER_HALF
            st = _dot3(keys_ref[h * 2 + p], q_ref[:, c0:c0 + PEER_HALF], NT)
            vals, idx = _topk_rows(st, K)
            sv.append(vals)
            si.append(idx)
        jid = lax.broadcasted_iota(jnp.int32, (SUBLANES, sv[0].shape[1]), 0)
        c_parts = [sv[0][0:1, :] + sv[1]]
        i_parts = [si[0][0:1, :] * float(PEER_NKEYS) + si[1]]
        for i in range(1, SUBLANES):
            keep = jid < K // (i + 1)
            c_parts.append(jnp.where(keep, sv[0][i:i + 1, :] + sv[1][0:SUBLANES, :], -jnp.inf))
            i_parts.append(si[0][i:i + 1, :] * float(PEER_NKEYS) + si[1][0:SUBLANES, :])
        c_parts.append(sv[0][SUBLANES:K, :] + sv[1][0:1, :])
        i_parts.append(si[0][SUBLANES:K, :] * float(PEER_NKEYS) + si[1][0:1, :])
        cand = jnp.concatenate(c_parts, axis=0)
        cidx = jnp.concatenate(i_parts, axis=0)
        best, eidx = _topk_rows(cand, K, payload=cidx)
        ex = jnp.exp(best - best[0:1, :])
        w_parts.append(ex / jnp.sum(ex, axis=0, keepdims=True))
        hi = jnp.floor(eidx * (1.0 / PEER_NKEYS))
        a_parts.append(hi)
        b_parts.append(eidx - hi * float(PEER_NKEYS))
    a_ref[...] = jnp.concatenate(a_parts, axis=0).T
    b_ref[...] = jnp.concatenate(b_parts, axis=0).T
    gw_ref[...] = jnp.concatenate(w_parts, axis=0).T


def _router(q, keys):
    n = q.shape[0]
    tq = min(256, n)
    tok = pl.BlockSpec((tq, PEER_HK), lambda i: (i, 0))
    sds = jax.ShapeDtypeStruct((n, PEER_HK), F32)
    return pl.pallas_call(
        _router_body,
        grid=(n // tq,),
        in_specs=[pl.BlockSpec((tq, q.shape[1]), lambda i: (i, 0)),
                  pl.BlockSpec(keys.shape, lambda i: (0, 0, 0))],
        out_specs=[tok, tok, tok],
        out_shape=[sds, sds, sds],
        compiler_params=_params(("parallel",), 32),
    )(q, keys)


PEER_CI = 4
PEER_CE = PEER_CI * PEER_NKEYS


PEER_SPLIT = 2
PEER_KEYS_PER_CALL = PEER_NKEYS // PEER_SPLIT


def _peer_body(x_ref, h_ref, a_ref, b_ref, w_ref, ut_ref, v_ref, gf_ref, o_ref, g2_ref, z_ref, *,
               pitch, key0, final_norm):
    j = pl.program_id(1)
    last = pl.num_programs(1) - 1
    t = x_ref.shape[0]
    cur = j % 2

    @pl.when(j == 0)
    def _():
        o_ref[...] = x_ref[...]
        z_ref[1] = jnp.zeros(z_ref.shape[1:], F32)
        kid = lax.broadcasted_iota(jnp.int32, (PEER_KEYS_PER_CALL, PEER_HK), 0).astype(F32) + float(key0)
        kid_b = lax.broadcasted_iota(jnp.int32, (PEER_NKEYS, PEER_HK), 0).astype(F32)

        def per_token(ti, carry):
            arow = a_ref[pl.ds(ti, 1), :]
            brow = b_ref[pl.ds(ti, 1), :]
            wrow = w_ref[pl.ds(ti, 1), :]
            lhs = jnp.where(kid == arow, wrow, 0.0).astype(BF16)
            rhs = jnp.where(kid_b == brow, 1.0, 0.0).astype(BF16)
            g_t = _mm(lhs, rhs, NT)
            g2_ref[pl.ds(ti, PEER_KEYS_PER_CALL, stride=pitch), :] = g_t
            return carry

        lax.fori_loop(0, t, per_token, 0, unroll=16)

    z = z_ref[1 - cur]
    z_ref[cur] = _mm(h_ref[...], ut_ref[...])
    jc = jnp.maximum(j - 1, 0)
    parts = []
    for ii in range(PEER_CI):
        start = pl.multiple_of((jc * PEER_CI + ii) * pitch, SUBLANES)
        gblk = g2_ref[pl.ds(start, t), :]
        zz = z[:, ii * PEER_NKEYS:(ii + 1) * PEER_NKEYS]
        act = 0.5 * zz * (1.0 + lax.erf(zz * (2.0 ** -0.5)))
        parts.append((act * gblk).astype(BF16))
    o_ref[...] += _mm(jnp.concatenate(parts, axis=1), v_ref[...])

    if final_norm:
        @pl.when(j == last)
        def _():
            y = o_ref[...]
            o_ref[...] = y * lax.rsqrt(jnp.mean(y * y, axis=-1, keepdims=True) + RMS_EPS) * gf_ref[...]


def _peer(x, h, a_idx, b_idx, gw, u_t, v_tab, g_final, final_norm):
    n, d = x.shape
    t = min(512, n)
    pitch = t + SUBLANES
    nch = PEER_N // PEER_CE // PEER_SPLIT
    once = pl.Buffered(1)
    tok = lambda c: pl.BlockSpec((t, c), lambda i, j: (i, 0), pipeline_mode=once)
    for part in range(PEER_SPLIT):
        c0 = part * nch
        is_last = part == PEER_SPLIT - 1
        x = pl.pallas_call(
            functools.partial(_peer_body, pitch=pitch, key0=part * PEER_KEYS_PER_CALL,
                              final_norm=final_norm and is_last),
            grid=(n // t, nch + 1),
            in_specs=[tok(d), tok(d), tok(PEER_HK), tok(PEER_HK), tok(PEER_HK),
                      pl.BlockSpec((d, PEER_CE), lambda i, j, c0=c0: (0, c0 + jnp.minimum(j, nch - 1))),
                      pl.BlockSpec((PEER_CE, d), lambda i, j, c0=c0: (c0 + jnp.maximum(j - 1, 0), 0)),
                      pl.BlockSpec((1, d), lambda i, j: (0, 0))],
            out_specs=pl.BlockSpec((t, d), lambda i, j: (i, 0)),
            out_shape=jax.ShapeDtypeStruct((n, d), F32),
            scratch_shapes=[pltpu.VMEM((PEER_KEYS_PER_CALL * pitch, LANES), F32),
                            pltpu.VMEM((2, t, PEER_CE), F32)],
            compiler_params=_params(("parallel", "arbitrary"), 56),
        )(x, h, a_idx, b_idx, gw, u_t, v_tab, g_final)
    return x


def _pad_cols(pieces, width):
    rows = pieces[0][0].shape[0]
    out = jnp.zeros((rows, width), pieces[0][0].dtype)
    for arr, off in pieces:
        out = out.at[:, off:off + arr.shape[1]].set(arr)
    return out


def _pad_rows(x, rows):
    return jnp.zeros((rows,) + x.shape[1:], x.dtype).at[:x.shape[0]].set(x)


def _lora_layout(x):
    o = 3 * RW_W
    return _pad_cols([(x[:, o:o + 64], 0), (x[:, o + 64:o + 128], 128), (x[:, o + 128:o + 288], 256)], 512)


def _prep_layer(l, w):
    f = {}
    w_in = w["w_in"][l]
    rw = w_in[:, :RW_COLS]
    ml = w_in[:, RW_COLS:RW_COLS + ML_COLS]
    gt = w_in[:, RW_COLS + ML_COLS:]
    f["norm_mix"] = w["norm_mix"][l][None, :]
    f["w_f"] = _pad_cols([(rw[:, :3 * RW_W], 0), (_lora_layout(rw), PF_LORA_OFF),
                          (ml[:, 2 * ML_QKW + 2 * ML_VW:], PF_MLG_OFF)], PF_COLS).astype(BF16)
    f["w_b"] = jnp.concatenate([ml[:, :2 * ML_QKW + 2 * ML_VW], gt], axis=1).astype(BF16)
    mu = jnp.stack([w["rw_mu_prev"][l], w["rw_mu_next"][l]])
    f["mu_main"] = mu[:, :3 * RW_W]
    f["mu_lora"] = _lora_layout(mu)
    f["w0"] = w["rw_w0"][l]
    f["w2p"] = jnp.stack([_pad_rows(w["rw_w2"][l, 0], 128), _pad_rows(w["rw_w2"][l, 1], 128)])
    f["a0"] = w["rw_a0"][l][None, :]
    f["a2p"] = _pad_rows(w["rw_a2"][l], 128)
    f["g2p"] = _pad_rows(w["rw_g2"][l], 256)
    f["k_k"] = w["rw_k_k"][l][None, :]
    f["k_a"] = w["rw_k_a"][l][None, :]
    f["r_k"] = w["rw_r_k"][l].reshape(1, RW_W)
    f["ln_w"] = w["rw_ln_w"][l][None, :]
    f["ln_b"] = w["rw_ln_b"][l][None, :]
    f["ml_bias"] = _pad_cols([(w["ml_b_i"][l].reshape(1, -1), 0), (w["ml_b_f"][l].reshape(1, -1), 2 * ML_HEADS)], LANES)
    f["ml_norm_w"] = w["ml_norm_w"][l][None, :]
    f["p_a"] = w["p_a"][l].astype(BF16)
    f["p_b"] = w["p_b"][l].astype(BF16)
    f["b_gate"] = w["b_gate"][l][None, :]
    f["w_out"] = w["w_out"][l].astype(BF16)
    f["norm_ffn"] = w["norm_ffn"][l][None, :]
    wq = w["peer_wq"][l]
    wq_hi = wq.astype(BF16)
    f["wq"] = (wq_hi, (wq - wq_hi.astype(F32)).astype(BF16))
    f["keys"] = w["peer_keys"][l].reshape(PEER_HEADS * 2, PEER_NKEYS, PEER_HALF)
    f["u_t"] = w["peer_u"][l].astype(BF16).T
    f["v"] = w["peer_v"][l].astype(BF16)
    return f


def _head_selectors():
    ch = jnp.arange(RW_W)[:, None] // RW_HEAD
    e = (ch == jnp.arange(LANES)[None, :]).astype(BF16)
    return e, e.T


def _layer(x, f, e, et, g_final, final_norm):
    b, t, d = x.shape
    n = b * t
    x2 = x.reshape(n, d)
    (hn,) = _norm_cast(x2, f["norm_mix"], 1)
    p_f = _matmul((hn,), (f["w_f"],), F32)
    p_b = _matmul((hn,), (f["w_b"],), BF16)
    p_f3 = p_f.reshape(b, t, -1)
    p_b3 = p_b.reshape(b, t, -1)

    r, k, v, kn, ab, lw0, lw1, gg, bonus = _rw_prep(
        p_f3, f["mu_main"], f["mu_lora"], f["w0"], f["w2p"], f["a0"], f["a2p"], f["g2p"], f["k_k"], f["k_a"],
        f["r_k"], e, et)
    y_f = _rw_scan(r, k, v, kn, ab, lw0, rev=False)
    y_b = _rw_scan(r, k, v, kn, ab, lw1, rev=True)

    h_f = _ml_scan(p_b3, p_f3, f["ml_bias"], rev=False)
    h_b = _ml_scan(p_b3, p_f3, f["ml_bias"], rev=True)

    ya, yb = _post(y_f.reshape(n, -1), y_b.reshape(n, -1), bonus.reshape(n, -1), gg.reshape(n, -1),
                   f["ln_w"], f["ln_b"], e, et, h_f.reshape(n, -1), h_b.reshape(n, -1), p_b, f["ml_norm_w"])
    merged = _merge(ya, yb, f["p_a"], f["p_b"], p_b, f["b_gate"])
    x1 = _resid_mm(merged, f["w_out"], x2)

    h_parts = _norm_cast(x1, f["norm_ffn"], 2)
    q = _matmul(h_parts, f["wq"], F32)
    a_idx, b_idx, gw = _router(q, f["keys"])
    x_out = _peer(x1, h_parts[0], a_idx, b_idx, gw, f["u_t"], f["v"], g_final, final_norm)
    return x_out.reshape(b, t, d)


def _trunk(x, layers, e, et, g_final):
    for l, f in enumerate(layers):
        x = _layer(x, f, e, et, g_final, final_norm=(l == len(layers) - 1))
    return x


def kernel(x_prompt, x_sample, norm_mix, w_in, rw_mu_prev, rw_mu_next, rw_w0, rw_w2, rw_a0, rw_a2, rw_g2, rw_k_k, rw_k_a, rw_r_k, rw_ln_w, rw_ln_b, ml_b_i, ml_b_f, ml_norm_w, p_a, p_b, b_gate, w_out, norm_ffn, peer_wq, peer_keys, peer_u, peer_v, norm_final):
    w = dict(norm_mix=norm_mix, w_in=w_in, rw_mu_prev=rw_mu_prev, rw_mu_next=rw_mu_next, rw_w0=rw_w0, rw_w2=rw_w2,
             rw_a0=rw_a0, rw_a2=rw_a2, rw_g2=rw_g2, rw_k_k=rw_k_k, rw_k_a=rw_k_a, rw_r_k=rw_r_k, rw_ln_w=rw_ln_w,
             rw_ln_b=rw_ln_b, ml_b_i=ml_b_i, ml_b_f=ml_b_f, ml_norm_w=ml_norm_w, p_a=p_a, p_b=p_b, b_gate=b_gate,
             w_out=w_out, norm_ffn=norm_ffn, peer_wq=peer_wq, peer_keys=peer_keys, peer_u=peer_u, peer_v=peer_v)
    depth = w_in.shape[0]
    layers = [_prep_layer(l, w) for l in range(depth)]
    e, et = _head_selectors()
    g_final = norm_final[None, :]
    return (_trunk(x_prompt, layers, e, et, g_final), _trunk(x_sample, layers, e, et, g_final))
```

```python
import functools
import math

import jax
import jax.numpy as jnp
from jax import lax
from jax.experimental import pallas as pl
from jax.experimental.pallas import tpu as pltpu

F32 = jnp.float32
BF16 = jnp.bfloat16

D_MODEL = 2048
RMS_EPS = 1e-6
RW_W = 1024
RW_HEAD = 64
RW_HEADS = 16
RW_DECAY_LORA = 64
RW_A_LORA = 64
RW_G_LORA = 160
RW_GN_EPS = 64e-5
RW_COLS = 3 * RW_W + RW_DECAY_LORA + RW_A_LORA + RW_G_LORA
RW_CHUNK = 64
RW_SUB = 2
RW_PASSES_A = 1
RW_PASSES_INV = 1
RW_PASSES_S = 1
ML_HEADS = 4
ML_QK = 128
ML_V = 256
ML_QKW = ML_HEADS * ML_QK
ML_VW = ML_HEADS * ML_V
ML_CHUNK = 128
ML_NORM_EPS = 1e-6
ML_COLS = 2 * ML_QKW + 2 * ML_VW + 4 * ML_HEADS
PF_LORA_OFF = 3 * RW_W
PF_LORA_W = 512
PF_MLG_OFF = PF_LORA_OFF + PF_LORA_W
PF_COLS = 4096
PB_GATE_OFF = 2 * ML_QKW + 2 * ML_VW
PB_COLS = PB_GATE_OFF + 2 * D_MODEL
PEER_HEADS = 8
PEER_NKEYS = 128
PEER_N = PEER_NKEYS * PEER_NKEYS
PEER_HALF = 128
PEER_TOPK = 16
PEER_HK = PEER_HEADS * PEER_TOPK
LANES = 128
SUBLANES = 8

NN = ((1,), (0,))
NT = ((1,), (1,))


def _mm(a, b, dims=NN):
    return lax.dot_general(a, b, (dims, ((), ())), preferred_element_type=F32)


def _split2(a):
    hi = a.astype(BF16)
    lo = (a - hi.astype(F32)).astype(BF16)
    return hi, lo


def _split3(a):
    hi = a.astype(BF16)
    r1 = a - hi.astype(F32)
    mid = r1.astype(BF16)
    lo = (r1 - mid.astype(F32)).astype(BF16)
    return hi, mid, lo


def _dot1(a, b, dims=NN):
    return _mm(a.astype(BF16), b.astype(BF16), dims)


def _dot3(a, b, dims=NN):
    ah, al = _split2(a)
    bh, bl = _split2(b)
    return _mm(ah, bh, dims) + (_mm(ah, bl, dims) + _mm(al, bh, dims))


def _dotp(a, b, dims, passes):
    return _dot3(a, b, dims) if passes == 3 else _dot1(a, b, dims)


def _dot_sel(sel_bf16, x, dims=NN):
    h, m, l = _split3(x)
    return _mm(sel_bf16, h, dims) + (_mm(sel_bf16, m, dims) + _mm(sel_bf16, l, dims))


def _dot_xsel(x, sel_bf16, dims=NN):
    h, m, l = _split3(x)
    return _mm(h, sel_bf16, dims) + (_mm(m, sel_bf16, dims) + _mm(l, sel_bf16, dims))


def _params(sem, vmem_mb):
    return pltpu.CompilerParams(dimension_semantics=sem, vmem_limit_bytes=vmem_mb << 20)


def _sigmoid(x):
    return 1.0 / (1.0 + jnp.exp(-x))


def _norm_cast_body(x_ref, g_ref, *o_refs):
    xf = x_ref[...]
    h = xf * lax.rsqrt(jnp.mean(xf * xf, axis=-1, keepdims=True) + RMS_EPS) * g_ref[...]
    hi = h.astype(BF16)
    o_refs[0][...] = hi
    if len(o_refs) == 2:
        o_refs[1][...] = (h - hi.astype(F32)).astype(BF16)


def _norm_cast(x, g, n_parts):
    n, d = x.shape
    tm = min(512, n)
    spec = pl.BlockSpec((tm, d), lambda i: (i, 0))
    return pl.pallas_call(
        _norm_cast_body,
        grid=(n // tm,),
        in_specs=[spec, pl.BlockSpec((1, d), lambda i: (0, 0))],
        out_specs=[spec] * n_parts,
        out_shape=[jax.ShapeDtypeStruct((n, d), BF16)] * n_parts,
        compiler_params=_params(("parallel",), 32),
    )(x, g)


def _matmul_body(*refs, n_parts):
    a_refs = refs[:n_parts]
    w_refs = refs[n_parts:2 * n_parts]
    o_ref = refs[2 * n_parts]
    acc = _mm(a_refs[0][...], w_refs[0][...])
    if n_parts == 2:
        acc = acc + (_mm(a_refs[0][...], w_refs[1][...]) + _mm(a_refs[1][...], w_refs[0][...]))
    o_ref[...] = acc.astype(o_ref.dtype)


def _matmul(a_parts, w_parts, out_dtype):
    n, d = a_parts[0].shape
    c = w_parts[0].shape[1]
    n_parts = len(a_parts)
    tm = min(1024, n)
    tn = min(1024 // n_parts, c)
    return pl.pallas_call(
        functools.partial(_matmul_body, n_parts=n_parts),
        grid=(n // tm, c // tn),
        in_specs=[pl.BlockSpec((tm, d), lambda i, j: (i, 0))] * n_parts
        + [pl.BlockSpec((d, tn), lambda i, j: (0, j))] * n_parts,
        out_specs=pl.BlockSpec((tm, tn), lambda i, j: (i, j)),
        out_shape=jax.ShapeDtypeStruct((n, c), out_dtype),
        compiler_params=_params(("parallel", "parallel"), 48),
    )(*a_parts, *w_parts)


def _shift_rows(p, prev_row, next_row, mu_prev, mu_next):
    n = p.shape[0]
    rid = lax.broadcasted_iota(jnp.int32, p.shape, 0)
    prev = jnp.where(rid == 0, prev_row, pltpu.roll(p, 1, 0))
    nxt = jnp.where(rid == n - 1, next_row, pltpu.roll(p, n - 1, 0))
    return p + mu_prev * (prev - p) + mu_next * (nxt - p)


def _rw_prep_body(m_ref, mp_ref, mn_ref, l_ref, lp_ref, ln_ref, mum_ref, mul_ref, w0_ref, w2_ref,
                  a0_ref, a2_ref, g2_ref, kk_ref, ka_ref, rk_ref, e_ref, et_ref,
                  r_o, k_o, v_o, kn_o, ab_o, lw0_o, lw1_o, g_o, bo_o):
    i = pl.program_id(1)
    first = i == 0
    last = i == pl.num_programs(1) - 1
    zero_m = jnp.zeros((1, m_ref.shape[2]), F32)
    zero_l = jnp.zeros((1, l_ref.shape[2]), F32)
    pm = _shift_rows(m_ref[0],
                     jnp.where(first, zero_m, mp_ref[0, SUBLANES - 1:SUBLANES, :]),
                     jnp.where(last, zero_m, mn_ref[0, 0:1, :]),
                     mum_ref[0:1, :], mum_ref[1:2, :])
    plo = _shift_rows(l_ref[0],
                      jnp.where(first, zero_l, lp_ref[0, SUBLANES - 1:SUBLANES, :]),
                      jnp.where(last, zero_l, ln_ref[0, 0:1, :]),
                      mul_ref[0:1, :], mul_ref[1:2, :])
    r = pm[:, 0:RW_W]
    k = pm[:, RW_W:2 * RW_W]
    v = pm[:, 2 * RW_W:3 * RW_W]
    wd = plo[:, 0:128]
    ad = plo[:, 128:256]
    gd = plo[:, 256:512]

    a = _sigmoid(a0_ref[...] + _dot1(ad, a2_ref[...]))
    g = _dot1(_sigmoid(gd), g2_ref[...])
    e = e_ref[...]
    et = et_ref[...]
    kk = k * kk_ref[...]
    ss = _dot1(_dot1(kk * kk, e), et)
    kn = kk / jnp.maximum(jnp.sqrt(ss), 1e-12)
    kmod = k * (1.0 + (a - 1.0) * ka_ref[...])
    wl = jnp.tanh(wd)
    c = math.exp(-0.5)
    lw0 = -_sigmoid(w0_ref[0:1, :] + _dot1(wl, w2_ref[0])) * c
    lw1 = -_sigmoid(w0_ref[1:2, :] + _dot1(wl, w2_ref[1])) * c
    rks = _dot1(_dot1(r * kmod * rk_ref[...], e), et)
    r_o[0] = r
    k_o[0] = kmod
    v_o[0] = v
    kn_o[0] = kn
    ab_o[0] = kn * a
    lw0_o[0] = lw0
    lw1_o[0] = lw1
    g_o[0] = g
    bo_o[0] = rks * v


def _rw_prep(p_f, mu_main, mu_lora, w0, w2p, a0, a2p, g2p, k_k, k_a, r_k, e, et):
    b, t, _ = p_f.shape
    cm, cl = 3 * RW_W, PF_LORA_W
    col = {cm: 0, cl: PF_LORA_OFF // PF_LORA_W}
    tt = min(128, t)
    nb = tt // SUBLANES
    nt8 = t // SUBLANES

    def main_spec(c):
        return pl.BlockSpec((1, tt, c), lambda bi, i: (bi, i, col[c]))

    def prev_spec(c):
        return pl.BlockSpec((1, SUBLANES, c), lambda bi, i: (bi, jnp.maximum(i * nb - 1, 0), col[c]))

    def next_spec(c):
        return pl.BlockSpec((1, SUBLANES, c), lambda bi, i: (bi, jnp.minimum((i + 1) * nb, nt8 - 1), col[c]))

    def full(x):
        nd = x.ndim
        return pl.BlockSpec(x.shape, lambda bi, i: (0,) * nd)

    consts = (mu_main, mu_lora, w0, w2p, a0, a2p, g2p, k_k, k_a, r_k, e, et)
    out_sds = jax.ShapeDtypeStruct((b, t, RW_W), F32)
    return pl.pallas_call(
        _rw_prep_body,
        grid=(b, t // tt),
        in_specs=[main_spec(cm), prev_spec(cm), next_spec(cm), main_spec(cl), prev_spec(cl), next_spec(cl)]
        + [full(x) for x in consts],
        out_specs=[pl.BlockSpec((1, tt, RW_W), lambda bi, i: (bi, i, 0))] * 9,
        out_shape=[out_sds] * 9,
        compiler_params=_params(("parallel", "parallel"), 56),
    )(p_f, p_f, p_f, p_f, p_f, p_f, *consts)


def _rw_scan_body(r_ref, k_ref, v_ref, kn_ref, ab_ref, lw_ref, y_ref, s_ref, *, rev):
    L = RW_CHUNK

    @pl.when(pl.program_id(1) == 0)
    def _():
        s_ref[...] = jnp.zeros_like(s_ref)

    row = lax.broadcasted_iota(jnp.int32, (L, L), 0)
    col = lax.broadcasted_iota(jnp.int32, (L, L), 1)
    if rev:
        incl = col >= row
        strict = col > row
    else:
        incl = col <= row
        strict = col < row
    tri = jnp.where(incl, 1.0, 0.0).astype(BF16)
    eye = jnp.where(row == col, 1.0, 0.0).astype(F32)
    lane = lax.broadcasted_iota(jnp.int32, (L, LANES), 1)
    head_masks = (lane < RW_HEAD, lane >= RW_HEAD)
    sr = lax.broadcasted_iota(jnp.int32, (LANES, LANES), 0)
    sc = lax.broadcasted_iota(jnp.int32, (LANES, LANES), 1)
    bd = (sr < RW_HEAD) == (sc < RW_HEAD)

    heads = [(hp, h) for hp in range(RW_HEADS // 2) for h in range(2)]
    sls = [slice(hp * LANES, (hp + 1) * LANES) for hp in range(RW_HEADS // 2)]

    def intra(off):
        rows = slice(off, off + L)
        lw = lw_ref[0, rows, :]
        cum = _dot_sel(tri, lw)
        p_in = jnp.exp(cum)
        p_ex = jnp.exp(cum - lw)
        p_inv = jnp.exp(-cum)
        tot = cum[0:1, :] if rev else cum[L - 1:L, :]
        p_end = jnp.exp(tot - cum)
        ab = ab_ref[0, rows, :]
        k = k_ref[0, rows, :]
        at = -kn_ref[0, rows, :] * p_ex
        rt = r_ref[0, rows, :] * p_in
        bt = ab * p_inv
        kt = k * p_inv
        v = v_ref[0, rows, :]
        at_h, a_ab, a_ak, a_rb, a_rk = [], [], [], [], []
        for hp, h in heads:
            sl, msk = sls[hp], head_masks[h]
            at_m = jnp.where(msk, at[:, sl], 0.0)
            lhs = jnp.concatenate([at_m, jnp.where(msk, rt[:, sl], 0.0)], axis=0)
            ab_blk = _dotp(lhs, bt[:, sl], NT, RW_PASSES_A)
            ak_blk = _dotp(lhs, kt[:, sl], NT, RW_PASSES_A)
            at_h.append(at_m)
            a_ab.append(jnp.where(strict, ab_blk[:L], 0.0))
            a_ak.append(jnp.where(strict, ak_blk[:L], 0.0))
            a_rb.append(jnp.where(incl, ab_blk[L:], 0.0))
            a_rk.append(jnp.where(incl, ak_blk[L:], 0.0))
        tm = [eye + a for a in a_ab]
        pw = [_dotp(a, a, NN, RW_PASSES_INV) for a in a_ab]
        for _ in range(4):
            st = [_dotp(jnp.concatenate([t_, p], axis=0), p, NN, RW_PASSES_INV) for t_, p in zip(tm, pw)]
            tm = [t_ + s_[:L] for t_, s_ in zip(tm, st)]
            pw = [s_[L:] for s_ in st]
        tm = [t_ + _dotp(t_, p, NN, RW_PASSES_INV) for t_, p in zip(tm, pw)]
        av = [_dotp(jnp.concatenate([a_ak[i], a_rk[i]], axis=0), v[:, sls[hp]], NN, RW_PASSES_A)
              for i, (hp, h) in enumerate(heads)]
        tx = [_dotp(tm[i], jnp.concatenate([at_h[i], jnp.where(head_masks[h], av[i][:L], 0.0)], axis=1), NN,
                    RW_PASSES_A) for i, (hp, h) in enumerate(heads)]
        return dict(tx=tx, av=av, a_rb=a_rb, rt=rt, v=v, bt_end=ab * p_end, kt_end=k * p_end, p_all=jnp.exp(tot))

    def sequential(c, off, states):
        tx, av, a_rb = c["tx"], c["av"], c["a_rb"]
        out = []
        for hp in range(RW_HEADS // 2):
            sl = sls[hp]
            i0, i1 = 2 * hp, 2 * hp + 1
            w_pair = tx[i0][:, :LANES] + tx[i1][:, :LANES]
            u0_pair = tx[i0][:, LANES:] + tx[i1][:, LANES:]
            y0_pair = jnp.where(head_masks[0], av[i0][L:], av[i1][L:])
            s = states[hp]
            ws = _dotp(jnp.concatenate([w_pair, c["rt"][:, sl]], axis=0), s, NT, RW_PASSES_S)
            u = ws[:L] + u0_pair
            y = ws[L:] + y0_pair + jnp.where(head_masks[0], _dotp(a_rb[i0], u, NN, RW_PASSES_A),
                                             _dotp(a_rb[i1], u, NN, RW_PASSES_A))
            uv_t = jnp.concatenate([u, c["v"][:, sl]], axis=0).T
            upd = _dotp(uv_t, jnp.concatenate([c["bt_end"][:, sl], c["kt_end"][:, sl]], axis=0), NN, RW_PASSES_S)
            out.append(s * c["p_all"][:, sl] + jnp.where(bd, upd, 0.0))
            y_ref[0, off:off + L, sl] = y
        return out

    offs = [c * L for c in range(RW_SUB)]
    if rev:
        offs = offs[::-1]
    chunks = [intra(off) for off in offs]
    states = [s_ref[hp] for hp in range(RW_HEADS // 2)]
    for c, off in zip(chunks, offs):
        states = sequential(c, off, states)
    for hp in range(RW_HEADS // 2):
        s_ref[hp] = states[hp]


def _rw_scan(r, k, v, kn, ab, lw, rev):
    b, t, w = r.shape
    tb = RW_CHUNK * RW_SUB
    nc = t // tb
    if rev:
        idx = lambda bi, c: (bi, nc - 1 - c, 0)
    else:
        idx = lambda bi, c: (bi, c, 0)
    spec = pl.BlockSpec((1, tb, w), idx)
    return pl.pallas_call(
        functools.partial(_rw_scan_body, rev=rev),
        grid=(b, nc),
        in_specs=[spec] * 6,
        out_specs=spec,
        out_shape=jax.ShapeDtypeStruct((b, t, w), F32),
        scratch_shapes=[pltpu.VMEM((RW_HEADS // 2, LANES, LANES), F32)],
        compiler_params=_params(("parallel", "arbitrary"), 32),
    )(r, k, v, kn, ab, lw)


def _ml_scan_body(q_ref, k_ref, v_ref, g_ref, b_ref, h_ref, c_ref, n_ref, m_ref, *, rev, direction):
    L = ML_CHUNK

    @pl.when(pl.program_id(1) == 0)
    def _():
        c_ref[...] = jnp.zeros_like(c_ref)
        n_ref[...] = jnp.zeros_like(n_ref)
        m_ref[...] = jnp.zeros_like(m_ref)

    row = lax.broadcasted_iota(jnp.int32, (L, L), 0)
    col = lax.broadcasted_iota(jnp.int32, (L, L), 1)
    incl = (col >= row) if rev else (col <= row)
    tri = jnp.where(incl, 1.0, 0.0).astype(BF16)
    lane = lax.broadcasted_iota(jnp.int32, (L, LANES), 1)
    sub = lax.broadcasted_iota(jnp.int32, (LANES, L), 0)

    g = g_ref[0] + b_ref[...]
    ls = jnp.minimum(g, 0.0) - jnp.log(1.0 + jnp.exp(-jnp.abs(g)))
    bc = _dot_sel(tri, ls)
    g_t = g.T
    bc_t = bc.T
    scale = ML_QK ** -0.5
    for h in range(ML_HEADS):
        ci = direction * ML_HEADS + h
        cf = 2 * ML_HEADS + direction * ML_HEADS + h
        b_col = jnp.sum(jnp.where(lane == cf, bc, 0.0), axis=1, keepdims=True)
        i_col = jnp.sum(jnp.where(lane == ci, g, 0.0), axis=1, keepdims=True)
        b_row = jnp.sum(jnp.where(sub == cf, bc_t, 0.0), axis=0, keepdims=True)
        i_row = jnp.sum(jnp.where(sub == ci, g_t, 0.0), axis=0, keepdims=True)
        m_prev = jnp.max(m_ref[h:h + 1, :], axis=1, keepdims=True)
        qs = q_ref[0, :, h * ML_QK:(h + 1) * ML_QK].astype(F32) * scale
        kc = k_ref[0, :, h * ML_QK:(h + 1) * ML_QK].astype(F32)
        vc = v_ref[0, :, h * ML_V:(h + 1) * ML_V].astype(F32)
        dmat = jnp.where(incl, b_col - b_row + i_row, -jnp.inf)
        inter = b_col + m_prev
        m_t = jnp.maximum(inter, jnp.max(dmat, axis=1, keepdims=True))
        s = _dot1(qs, kc, NT) * jnp.exp(dmat - m_t)
        e_inter = jnp.exp(inter - m_t)
        num = _dot1(s, vc) + e_inter * _dot1(qs, c_ref[h])
        den = jnp.sum(s, axis=1, keepdims=True) + e_inter * jnp.sum(qs * n_ref[h:h + 1, :], axis=1, keepdims=True)
        h_ref[0, :, h * ML_V:(h + 1) * ML_V] = num / jnp.maximum(jnp.abs(den), jnp.exp(-m_t))
        b_last = jnp.min(b_row, axis=1, keepdims=True)
        g_row = b_last - b_row + i_row
        g_col = b_last - b_col + i_col
        m_new = jnp.maximum(b_last + m_prev, jnp.max(g_row, axis=1, keepdims=True))
        w_col = jnp.exp(g_col - m_new)
        dec = jnp.exp(b_last + m_prev - m_new)
        c_ref[h] = dec * c_ref[h] + _dot1(kc.T, vc * w_col)
        n_ref[h:h + 1, :] = dec * n_ref[h:h + 1, :] + jnp.sum(kc * w_col, axis=0, keepdims=True)
        m_ref[h:h + 1, :] = jnp.broadcast_to(m_new, (1, LANES))


def _ml_scan(p_b, p_f, bias, rev):
    b, t, _ = p_b.shape
    L = ML_CHUNK
    nc = t // L
    tix = (lambda c: nc - 1 - c) if rev else (lambda c: c)
    return pl.pallas_call(
        functools.partial(_ml_scan_body, rev=rev, direction=1 if rev else 0),
        grid=(b, nc),
        in_specs=[pl.BlockSpec((1, L, ML_QKW), lambda bi, c: (bi, tix(c), 0)),
                  pl.BlockSpec((1, L, ML_QKW), lambda bi, c: (bi, tix(c), 1)),
                  pl.BlockSpec((1, L, ML_VW), lambda bi, c: (bi, tix(c), 1)),
                  pl.BlockSpec((1, L, LANES), lambda bi, c: (bi, tix(c), PF_MLG_OFF // LANES)),
                  pl.BlockSpec((1, LANES), lambda bi, c: (0, 0))],
        out_specs=pl.BlockSpec((1, L, ML_VW), lambda bi, c: (bi, tix(c), 0)),
        out_shape=jax.ShapeDtypeStruct((b, t, ML_VW), F32),
        scratch_shapes=[pltpu.VMEM((ML_HEADS, ML_QK, ML_V), F32),
                        pltpu.VMEM((SUBLANES, LANES), F32),
                        pltpu.VMEM((SUBLANES, LANES), F32)],
        compiler_params=_params(("parallel", "arbitrary"), 32),
    )(p_b, p_b, p_b, p_f, bias)


def _post_body(yf_ref, yb_ref, bo_ref, g_ref, lnw_ref, lnb_ref, e_ref, et_ref,
               hf_ref, hb_ref, o_ref, nw_ref, ya_ref, yb_o_ref):
    e = e_ref[...]
    et = et_ref[...]
    y = yf_ref[...] + yb_ref[...]
    inv = 1.0 / RW_HEAD
    mu = _dot1(_dot1(y, e), et) * inv
    yc = y - mu
    var = _dot1(_dot1(yc * yc, e), et) * inv
    yn = yc * lax.rsqrt(var + RW_GN_EPS) * lnw_ref[...] + lnb_ref[...]
    ya_ref[...] = ((yn + bo_ref[...]) * g_ref[...]).astype(ya_ref.dtype)

    hsum = hf_ref[...] + hb_ref[...]
    parts = []
    for h in range(ML_HEADS):
        hh = hsum[:, h * ML_V:(h + 1) * ML_V]
        parts.append(hh * lax.rsqrt(jnp.mean(hh * hh, axis=-1, keepdims=True) + ML_NORM_EPS))
    hn = jnp.concatenate(parts, axis=1) * nw_ref[...]
    yb_o_ref[...] = (hn * _sigmoid(o_ref[...].astype(F32))).astype(yb_o_ref.dtype)


def _post(yf, yb, bonus, g, ln_w, ln_b, e, et, hf, hb, vo, norm_w):
    n = yf.shape[0]
    tm = min(256, n)
    tok = lambda c: pl.BlockSpec((tm, c), lambda i: (i, 0))
    const = lambda x: pl.BlockSpec(x.shape, lambda i: (0, 0))
    return pl.pallas_call(
        _post_body,
        grid=(n // tm,),
        in_specs=[tok(RW_W), tok(RW_W), tok(RW_W), tok(RW_W), const(ln_w), const(ln_b), const(e), const(et),
                  tok(ML_VW), tok(ML_VW), pl.BlockSpec((tm, ML_VW), lambda i: (i, (2 * ML_QKW + ML_VW) // ML_VW)),
                  const(norm_w)],
        out_specs=[tok(RW_W), tok(ML_VW)],
        out_shape=[jax.ShapeDtypeStruct((n, RW_W), BF16), jax.ShapeDtypeStruct((n, ML_VW), BF16)],
        compiler_params=_params(("parallel",), 40),
    )(yf, yb, bonus, g, ln_w, ln_b, e, et, hf, hb, vo, norm_w)


def _merge_body(ya_ref, yb_ref, pa_ref, pb_ref, ga_ref, gb_ref, ba_ref, bb_ref, o_ref):
    pa = _mm(ya_ref[...], pa_ref[...])
    pb = _mm(yb_ref[...], pb_ref[...])
    ga = ga_ref[...].astype(F32) + ba_ref[...]
    gb = gb_ref[...].astype(F32) + bb_ref[...]
    o_ref[...] = (_sigmoid(ga) * pa + _sigmoid(gb) * pb).astype(o_ref.dtype)


def _merge(ya, yb, p_a, p_b, proj_b, b_gate):
    n = ya.shape[0]
    tm = min(1024, n)
    tn = 1024
    nj = D_MODEL // tn
    g0 = PB_GATE_OFF // tn
    return pl.pallas_call(
        _merge_body,
        grid=(n // tm, nj),
        in_specs=[pl.BlockSpec((tm, RW_W), lambda i, j: (i, 0)), pl.BlockSpec((tm, ML_VW), lambda i, j: (i, 0)),
                  pl.BlockSpec((RW_W, tn), lambda i, j: (0, j)), pl.BlockSpec((ML_VW, tn), lambda i, j: (0, j)),
                  pl.BlockSpec((tm, tn), lambda i, j: (i, g0 + j)), pl.BlockSpec((tm, tn), lambda i, j: (i, g0 + nj + j)),
                  pl.BlockSpec((1, tn), lambda i, j: (0, j)), pl.BlockSpec((1, tn), lambda i, j: (0, j + nj))],
        out_specs=pl.BlockSpec((tm, tn), lambda i, j: (i, j)),
        out_shape=jax.ShapeDtypeStruct((n, D_MODEL), BF16),
        compiler_params=_params(("parallel", "parallel"), 40),
    )(ya, yb, p_a, p_b, proj_b, proj_b, b_gate, b_gate)


def _resid_mm_body(a_ref, w_ref, x_ref, o_ref):
    o_ref[...] = x_ref[...] + _mm(a_ref[...], w_ref[...])


def _resid_mm(a, w, x):
    n, kdim = a.shape
    c = w.shape[1]
    tm = min(1024, n)
    tn = 1024
    return pl.pallas_call(
        _resid_mm_body,
        grid=(n // tm, c // tn),
        in_specs=[pl.BlockSpec((tm, kdim), lambda i, j: (i, 0)), pl.BlockSpec((kdim, tn), lambda i, j: (0, j)),
                  pl.BlockSpec((tm, tn), lambda i, j: (i, j))],
        out_specs=pl.BlockSpec((tm, tn), lambda i, j: (i, j)),
        out_shape=jax.ShapeDtypeStruct((n, c), F32),
        compiler_params=_params(("parallel", "parallel"), 48),
    )(a, w, x)


def _topk_rows(s, k, payload=None):
    nrow, t = s.shape
    rid = lax.broadcasted_iota(jnp.int32, (nrow, t), 0).astype(F32)
    kid = lax.broadcasted_iota(jnp.int32, (k, t), 0)
    vals = jnp.zeros((k, t), F32)
    sel = jnp.zeros((k, t), F32)
    for j in range(k):
        m = jnp.max(s, axis=0, keepdims=True)
        pos = jnp.min(jnp.where(s == m, rid, float(nrow)), axis=0, keepdims=True)
        hit = rid == pos
        if payload is None:
            picked = pos
        else:
            picked = jnp.max(jnp.where(hit, payload, -1.0), axis=0, keepdims=True)
        vals = jnp.where(kid == j, m, vals)
        sel = jnp.where(kid == j, picked, sel)
        s = jnp.where(hit, -jnp.inf, s)
    return vals, sel


def _router_body(q_ref, keys_ref, a_ref, b_ref, gw_ref):
    K = PEER_TOPK
    a_parts, b_parts, w_parts = [], [], []
    for h in range(PEER_HEADS):
        sv, si = [], []
        for p in range(2):
            c0 = (h * 2 + p) * PEER_HALF
            st = _dot3(keys_ref[h * 2 + p], q_ref[:, c0:c0 + PEER_HALF], NT)
            vals, idx = _topk_rows(st, K)
            sv.append(vals)
            si.append(idx)
        jid = lax.broadcasted_iota(jnp.int32, (SUBLANES, sv[0].shape[1]), 0)
        c_parts = [sv[0][0:1, :] + sv[1]]
        i_parts = [si[0][0:1, :] * float(PEER_NKEYS) + si[1]]
        for i in range(1, SUBLANES):
            keep = jid < K // (i + 1)
            c_parts.append(jnp.where(keep, sv[0][i:i + 1, :] + sv[1][0:SUBLANES, :], -jnp.inf))
            i_parts.append(si[0][i:i + 1, :] * float(PEER_NKEYS) + si[1][0:SUBLANES, :])
        c_parts.append(sv[0][SUBLANES:K, :] + sv[1][0:1, :])
        i_parts.append(si[0][SUBLANES:K, :] * float(PEER_NKEYS) + si[1][0:1, :])
        cand = jnp.concatenate(c_parts, axis=0)
        cidx = jnp.concatenate(i_parts, axis=0)
        best, eidx = _topk_rows(cand, K, payload=cidx)
        ex = jnp.exp(best - best[0:1, :])
        w_parts.append(ex / jnp.sum(ex, axis=0, keepdims=True))
        hi = jnp.floor(eidx * (1.0 / PEER_NKEYS))
        a_parts.append(hi)
        b_parts.append(eidx - hi * float(PEER_NKEYS))
    a_ref[...] = jnp.concatenate(a_parts, axis=0).T
    b_ref[...] = jnp.concatenate(b_parts, axis=0).T
    gw_ref[...] = jnp.concatenate(w_parts, axis=0).T


def _router(q, keys):
    n = q.shape[0]
    tq = min(256, n)
    tok = pl.BlockSpec((tq, PEER_HK), lambda i: (i, 0))
    sds = jax.ShapeDtypeStruct((n, PEER_HK), F32)
    return pl.pallas_call(
        _router_body,
        grid=(n // tq,),
        in_specs=[pl.BlockSpec((tq, q.shape[1]), lambda i: (i, 0)),
                  pl.BlockSpec(keys.shape, lambda i: (0, 0, 0))],
        out_specs=[tok, tok, tok],
        out_shape=[sds, sds, sds],
        compiler_params=_params(("parallel",), 32),
    )(q, keys)


PEER_CI = 4
PEER_CE = PEER_CI * PEER_NKEYS


PEER_SPLIT = 2
PEER_KEYS_PER_CALL = PEER_NKEYS // PEER_SPLIT


def _peer_body(x_ref, h_ref, a_ref, b_ref, w_ref, ut_ref, v_ref, gf_ref, o_ref, g2_ref, z_ref, *,
               pitch, key0, final_norm):
    j = pl.program_id(1)
    last = pl.num_programs(1) - 1
    t = x_ref.shape[0]
    cur = j % 2

    @pl.when(j == 0)
    def _():
        o_ref[...] = x_ref[...]
        z_ref[1] = jnp.zeros(z_ref.shape[1:], F32)
        kid = lax.broadcasted_iota(jnp.int32, (PEER_KEYS_PER_CALL, PEER_HK), 0).astype(F32) + float(key0)
        kid_b = lax.broadcasted_iota(jnp.int32, (PEER_NKEYS, PEER_HK), 0).astype(F32)

        def per_token(ti, carry):
            arow = a_ref[pl.ds(ti, 1), :]
            brow = b_ref[pl.ds(ti, 1), :]
            wrow = w_ref[pl.ds(ti, 1), :]
            lhs = jnp.where(kid == arow, wrow, 0.0).astype(BF16)
            rhs = jnp.where(kid_b == brow, 1.0, 0.0).astype(BF16)
            g_t = _mm(lhs, rhs, NT)
            g2_ref[pl.ds(ti, PEER_KEYS_PER_CALL, stride=pitch), :] = g_t
            return carry

        lax.fori_loop(0, t, per_token, 0, unroll=16)

    z = z_ref[1 - cur]
    z_ref[cur] = _mm(h_ref[...], ut_ref[...])
    jc = jnp.maximum(j - 1, 0)
    parts = []
    for ii in range(PEER_CI):
        start = pl.multiple_of((jc * PEER_CI + ii) * pitch, SUBLANES)
        gblk = g2_ref[pl.ds(start, t), :]
        zz = z[:, ii * PEER_NKEYS:(ii + 1) * PEER_NKEYS]
        act = 0.5 * zz * (1.0 + lax.erf(zz * (2.0 ** -0.5)))
        parts.append((act * gblk).astype(BF16))
    o_ref[...] += _mm(jnp.concatenate(parts, axis=1), v_ref[...])

    if final_norm:
        @pl.when(j == last)
        def _():
            y = o_ref[...]
            o_ref[...] = y * lax.rsqrt(jnp.mean(y * y, axis=-1, keepdims=True) + RMS_EPS) * gf_ref[...]


def _peer(x, h, a_idx, b_idx, gw, u_t, v_tab, g_final, final_norm):
    n, d = x.shape
    t = min(512, n)
    pitch = t + SUBLANES
    nch = PEER_N // PEER_CE // PEER_SPLIT
    once = pl.Buffered(1)
    tok = lambda c: pl.BlockSpec((t, c), lambda i, j: (i, 0), pipeline_mode=once)
    for part in range(PEER_SPLIT):
        c0 = part * nch
        is_last = part == PEER_SPLIT - 1
        x = pl.pallas_call(
            functools.partial(_peer_body, pitch=pitch, key0=part * PEER_KEYS_PER_CALL,
                              final_norm=final_norm and is_last),
            grid=(n // t, nch + 1),
            in_specs=[tok(d), tok(d), tok(PEER_HK), tok(PEER_HK), tok(PEER_HK),
                      pl.BlockSpec((d, PEER_CE), lambda i, j, c0=c0: (0, c0 + jnp.minimum(j, nch - 1))),
                      pl.BlockSpec((PEER_CE, d), lambda i, j, c0=c0: (c0 + jnp.maximum(j - 1, 0), 0)),
                      pl.BlockSpec((1, d), lambda i, j: (0, 0))],
            out_specs=pl.BlockSpec((t, d), lambda i, j: (i, 0)),
            out_shape=jax.ShapeDtypeStruct((n, d), F32),
            scratch_shapes=[pltpu.VMEM((PEER_KEYS_PER_CALL * pitch, LANES), F32),
                            pltpu.VMEM((2, t, PEER_CE), F32)],
            compiler_params=_params(("parallel", "arbitrary"), 56),
        )(x, h, a_idx, b_idx, gw, u_t, v_tab, g_final)
    return x


def _pad_cols(pieces, width):
    rows = pieces[0][0].shape[0]
    out = jnp.zeros((rows, width), pieces[0][0].dtype)
    for arr, off in pieces:
        out = out.at[:, off:off + arr.shape[1]].set(arr)
    return out


def _pad_rows(x, rows):
    return jnp.zeros((rows,) + x.shape[1:], x.dtype).at[:x.shape[0]].set(x)


def _lora_layout(x):
    o = 3 * RW_W
    return _pad_cols([(x[:, o:o + 64], 0), (x[:, o + 64:o + 128], 128), (x[:, o + 128:o + 288], 256)], 512)


def _prep_layer(l, w):
    f = {}
    w_in = w["w_in"][l]
    rw = w_in[:, :RW_COLS]
    ml = w_in[:, RW_COLS:RW_COLS + ML_COLS]
    gt = w_in[:, RW_COLS + ML_COLS:]
    f["norm_mix"] = w["norm_mix"][l][None, :]
    f["w_f"] = _pad_cols([(rw[:, :3 * RW_W], 0), (_lora_layout(rw), PF_LORA_OFF),
                          (ml[:, 2 * ML_QKW + 2 * ML_VW:], PF_MLG_OFF)], PF_COLS).astype(BF16)
    f["w_b"] = jnp.concatenate([ml[:, :2 * ML_QKW + 2 * ML_VW], gt], axis=1).astype(BF16)
    mu = jnp.stack([w["rw_mu_prev"][l], w["rw_mu_next"][l]])
    f["mu_main"] = mu[:, :3 * RW_W]
    f["mu_lora"] = _lora_layout(mu)
    f["w0"] = w["rw_w0"][l]
    f["w2p"] = jnp.stack([_pad_rows(w["rw_w2"][l, 0], 128), _pad_rows(w["rw_w2"][l, 1], 128)]).astype(BF16)
    f["a0"] = w["rw_a0"][l][None, :]
    f["a2p"] = _pad_rows(w["rw_a2"][l], 128).astype(BF16)
    f["g2p"] = _pad_rows(w["rw_g2"][l], 256).astype(BF16)
    f["k_k"] = w["rw_k_k"][l][None, :]
    f["k_a"] = w["rw_k_a"][l][None, :]
    f["r_k"] = w["rw_r_k"][l].reshape(1, RW_W)
    f["ln_w"] = w["rw_ln_w"][l][None, :]
    f["ln_b"] = w["rw_ln_b"][l][None, :]
    f["ml_bias"] = _pad_cols([(w["ml_b_i"][l].reshape(1, -1), 0), (w["ml_b_f"][l].reshape(1, -1), 2 * ML_HEADS)], LANES)
    f["ml_norm_w"] = w["ml_norm_w"][l][None, :]
    f["p_a"] = w["p_a"][l].astype(BF16)
    f["p_b"] = w["p_b"][l].astype(BF16)
    f["b_gate"] = w["b_gate"][l][None, :]
    f["w_out"] = w["w_out"][l].astype(BF16)
    f["norm_ffn"] = w["norm_ffn"][l][None, :]
    f["wq"] = (w["peer_wq"][l].astype(BF16),)
    f["keys"] = w["peer_keys"][l].reshape(PEER_HEADS * 2, PEER_NKEYS, PEER_HALF)
    f["u_t"] = w["peer_u"][l].astype(BF16).T
    f["v"] = w["peer_v"][l].astype(BF16)
    return f


def _head_selectors():
    ch = jnp.arange(RW_W)[:, None] // RW_HEAD
    e = (ch == jnp.arange(LANES)[None, :]).astype(BF16)
    return e, e.T


def _layer(x, f, e, et, g_final, final_norm):
    b, t, d = x.shape
    n = b * t
    x2 = x.reshape(n, d)
    (hn,) = _norm_cast(x2, f["norm_mix"], 1)
    p_f = _matmul((hn,), (f["w_f"],), F32)
    p_b = _matmul((hn,), (f["w_b"],), BF16)
    p_f3 = p_f.reshape(b, t, -1)
    p_b3 = p_b.reshape(b, t, -1)

    r, k, v, kn, ab, lw0, lw1, gg, bonus = _rw_prep(
        p_f3, f["mu_main"], f["mu_lora"], f["w0"], f["w2p"], f["a0"], f["a2p"], f["g2p"], f["k_k"], f["k_a"],
        f["r_k"], e, et)
    y_f = _rw_scan(r, k, v, kn, ab, lw0, rev=False)
    y_b = _rw_scan(r, k, v, kn, ab, lw1, rev=True)

    h_f = _ml_scan(p_b3, p_f3, f["ml_bias"], rev=False)
    h_b = _ml_scan(p_b3, p_f3, f["ml_bias"], rev=True)

    ya, yb = _post(y_f.reshape(n, -1), y_b.reshape(n, -1), bonus.reshape(n, -1), gg.reshape(n, -1),
                   f["ln_w"], f["ln_b"], e, et, h_f.reshape(n, -1), h_b.reshape(n, -1), p_b, f["ml_norm_w"])
    merged = _merge(ya, yb, f["p_a"], f["p_b"], p_b, f["b_gate"])
    x1 = _resid_mm(merged, f["w_out"], x2)

    h_parts = _norm_cast(x1, f["norm_ffn"], 1)
    q = _matmul(h_parts, f["wq"], F32)
    a_idx, b_idx, gw = _router(q, f["keys"])
    x_out = _peer(x1, h_parts[0], a_idx, b_idx, gw, f["u_t"], f["v"], g_final, final_norm)
    return x_out.reshape(b, t, d)


def _trunk(x, layers, e, et, g_final):
    for l, f in enumerate(layers):
        x = _layer(x, f, e, et, g_final, final_norm=(l == len(layers) - 1))
    return x


def kernel(x_prompt, x_sample, norm_mix, w_in, rw_mu_prev, rw_mu_next, rw_w0, rw_w2, rw_a0, rw_a2, rw_g2, rw_k_k, rw_k_a, rw_r_k, rw_ln_w, rw_ln_b, ml_b_i, ml_b_f, ml_norm_w, p_a, p_b, b_gate, w_out, norm_ffn, peer_wq, peer_keys, peer_u, peer_v, norm_final):
    w = dict(norm_mix=norm_mix, w_in=w_in, rw_mu_prev=rw_mu_prev, rw_mu_next=rw_mu_next, rw_w0=rw_w0, rw_w2=rw_w2,
             rw_a0=rw_a0, rw_a2=rw_a2, rw_g2=rw_g2, rw_k_k=rw_k_k, rw_k_a=rw_k_a, rw_r_k=rw_r_k, rw_ln_w=rw_ln_w,
             rw_ln_b=rw_ln_b, ml_b_i=ml_b_i, ml_b_f=ml_b_f, ml_norm_w=ml_norm_w, p_a=p_a, p_b=p_b, b_gate=b_gate,
             w_out=w_out, norm_ffn=norm_ffn, peer_wq=peer_wq, peer_keys=peer_keys, peer_u=peer_u, peer_v=peer_v)
    depth = w_in.shape[0]
    layers = [_prep_layer(l, w) for l in range(depth)]
    e, et = _head_selectors()
    g_final = norm_final[None, :]
    return (_trunk(x_prompt, layers, e, et, g_final), _trunk(x_sample, layers, e, et, g_final))
```

```python
import functools
import math

import jax
import jax.numpy as jnp
from jax import lax
from jax.experimental import pallas as pl
from jax.experimental.pallas import tpu as pltpu

F32 = jnp.float32
BF16 = jnp.bfloat16

D_MODEL = 2048
RMS_EPS = 1e-6
RW_W = 1024
RW_HEAD = 64
RW_HEADS = 16
RW_DECAY_LORA = 64
RW_A_LORA = 64
RW_G_LORA = 160
RW_GN_EPS = 64e-5
RW_COLS = 3 * RW_W + RW_DECAY_LORA + RW_A_LORA + RW_G_LORA
RW_CHUNK = 64
RW_SUB = 2
RW_PASSES_A = 1
RW_PASSES_INV = 1
RW_PASSES_S = 1
ML_HEADS = 4
ML_QK = 128
ML_V = 256
ML_QKW = ML_HEADS * ML_QK
ML_VW = ML_HEADS * ML_V
ML_CHUNK = 128
ML_NORM_EPS = 1e-6
ML_COLS = 2 * ML_QKW + 2 * ML_VW + 4 * ML_HEADS
PF_LORA_OFF = 3 * RW_W
PF_LORA_W = 512
PF_MLG_OFF = PF_LORA_OFF + PF_LORA_W
PF_COLS = 4096
PB_GATE_OFF = 2 * ML_QKW + 2 * ML_VW
PB_COLS = PB_GATE_OFF + 2 * D_MODEL
PEER_HEADS = 8
PEER_NKEYS = 128
PEER_N = PEER_NKEYS * PEER_NKEYS
PEER_HALF = 128
PEER_TOPK = 16
PEER_HK = PEER_HEADS * PEER_TOPK
LANES = 128
SUBLANES = 8

NN = ((1,), (0,))
NT = ((1,), (1,))


def _mm(a, b, dims=NN):
    return lax.dot_general(a, b, (dims, ((), ())), preferred_element_type=F32)


def _split2(a):
    hi = a.astype(BF16)
    lo = (a - hi.astype(F32)).astype(BF16)
    return hi, lo


def _split3(a):
    hi = a.astype(BF16)
    r1 = a - hi.astype(F32)
    mid = r1.astype(BF16)
    lo = (r1 - mid.astype(F32)).astype(BF16)
    return hi, mid, lo


def _dot1(a, b, dims=NN):
    return _mm(a.astype(BF16), b.astype(BF16), dims)


def _dot3(a, b, dims=NN):
    ah, al = _split2(a)
    bh, bl = _split2(b)
    return _mm(ah, bh, dims) + (_mm(ah, bl, dims) + _mm(al, bh, dims))


def _dotp(a, b, dims, passes):
    return _dot3(a, b, dims) if passes == 3 else _dot1(a, b, dims)


def _dot_sel(sel_bf16, x, dims=NN):
    h, m, l = _split3(x)
    return _mm(sel_bf16, h, dims) + (_mm(sel_bf16, m, dims) + _mm(sel_bf16, l, dims))


def _dot_xsel(x, sel_bf16, dims=NN):
    h, m, l = _split3(x)
    return _mm(h, sel_bf16, dims) + (_mm(m, sel_bf16, dims) + _mm(l, sel_bf16, dims))


def _params(sem, vmem_mb):
    return pltpu.CompilerParams(dimension_semantics=sem, vmem_limit_bytes=vmem_mb << 20)


def _sigmoid(x):
    return 1.0 / (1.0 + jnp.exp(-x))


def _norm_cast_body(x_ref, g_ref, *o_refs):
    xf = x_ref[...]
    h = xf * lax.rsqrt(jnp.mean(xf * xf, axis=-1, keepdims=True) + RMS_EPS) * g_ref[...]
    hi = h.astype(BF16)
    o_refs[0][...] = hi
    if len(o_refs) == 2:
        o_refs[1][...] = (h - hi.astype(F32)).astype(BF16)


def _norm_cast(x, g, n_parts):
    n, d = x.shape
    tm = min(512, n)
    spec = pl.BlockSpec((tm, d), lambda i: (i, 0))
    return pl.pallas_call(
        _norm_cast_body,
        grid=(n // tm,),
        in_specs=[spec, pl.BlockSpec((1, d), lambda i: (0, 0))],
        out_specs=[spec] * n_parts,
        out_shape=[jax.ShapeDtypeStruct((n, d), BF16)] * n_parts,
        compiler_params=_params(("parallel",), 32),
    )(x, g)


def _matmul_body(*refs, n_parts):
    a_refs = refs[:n_parts]
    w_refs = refs[n_parts:2 * n_parts]
    o_ref = refs[2 * n_parts]
    acc = _mm(a_refs[0][...], w_refs[0][...])
    if n_parts == 2:
        acc = acc + (_mm(a_refs[0][...], w_refs[1][...]) + _mm(a_refs[1][...], w_refs[0][...]))
    o_ref[...] = acc.astype(o_ref.dtype)


def _matmul(a_parts, w_parts, out_dtype):
    n, d = a_parts[0].shape
    c = w_parts[0].shape[1]
    n_parts = len(a_parts)
    tm = min(1024, n)
    tn = min(1024 // n_parts, c)
    return pl.pallas_call(
        functools.partial(_matmul_body, n_parts=n_parts),
        grid=(n // tm, c // tn),
        in_specs=[pl.BlockSpec((tm, d), lambda i, j: (i, 0))] * n_parts
        + [pl.BlockSpec((d, tn), lambda i, j: (0, j))] * n_parts,
        out_specs=pl.BlockSpec((tm, tn), lambda i, j: (i, j)),
        out_shape=jax.ShapeDtypeStruct((n, c), out_dtype),
        compiler_params=_params(("parallel", "parallel"), 48),
    )(*a_parts, *w_parts)


def _shift_rows(p, prev_row, next_row, mu_prev, mu_next):
    n = p.shape[0]
    rid = lax.broadcasted_iota(jnp.int32, p.shape, 0)
    prev = jnp.where(rid == 0, prev_row, pltpu.roll(p, 1, 0))
    nxt = jnp.where(rid == n - 1, next_row, pltpu.roll(p, n - 1, 0))
    return p + mu_prev * (prev - p) + mu_next * (nxt - p)


def _rw_prep_body(m_ref, mp_ref, mn_ref, l_ref, lp_ref, ln_ref, mum_ref, mul_ref, w0_ref, w2_ref,
                  a0_ref, a2_ref, g2_ref, kk_ref, ka_ref, rk_ref, e_ref, et_ref,
                  r_o, k_o, v_o, kn_o, ab_o, lw0_o, lw1_o, g_o, bo_o):
    i = pl.program_id(1)
    first = i == 0
    last = i == pl.num_programs(1) - 1
    zero_m = jnp.zeros((1, m_ref.shape[2]), F32)
    zero_l = jnp.zeros((1, l_ref.shape[2]), F32)
    pm = _shift_rows(m_ref[0],
                     jnp.where(first, zero_m, mp_ref[0, SUBLANES - 1:SUBLANES, :]),
                     jnp.where(last, zero_m, mn_ref[0, 0:1, :]),
                     mum_ref[0:1, :], mum_ref[1:2, :])
    plo = _shift_rows(l_ref[0],
                      jnp.where(first, zero_l, lp_ref[0, SUBLANES - 1:SUBLANES, :]),
                      jnp.where(last, zero_l, ln_ref[0, 0:1, :]),
                      mul_ref[0:1, :], mul_ref[1:2, :])
    r = pm[:, 0:RW_W]
    k = pm[:, RW_W:2 * RW_W]
    v = pm[:, 2 * RW_W:3 * RW_W]
    wd = plo[:, 0:128]
    ad = plo[:, 128:256]
    gd = plo[:, 256:512]

    a = _sigmoid(a0_ref[...] + _dot1(ad, a2_ref[...]))
    g = _dot1(_sigmoid(gd), g2_ref[...])
    e = e_ref[...]
    et = et_ref[...]
    kk = k * kk_ref[...]
    ss = _dot1(_dot1(kk * kk, e), et)
    kn = kk / jnp.maximum(jnp.sqrt(ss), 1e-12)
    kmod = k * (1.0 + (a - 1.0) * ka_ref[...])
    wl = jnp.tanh(wd)
    c = math.exp(-0.5)
    lw0 = -_sigmoid(w0_ref[0:1, :] + _dot1(wl, w2_ref[0])) * c
    lw1 = -_sigmoid(w0_ref[1:2, :] + _dot1(wl, w2_ref[1])) * c
    rks = _dot1(_dot1(r * kmod * rk_ref[...], e), et)
    r_o[0] = r
    k_o[0] = kmod
    v_o[0] = v
    kn_o[0] = kn
    ab_o[0] = kn * a
    lw0_o[0] = lw0
    lw1_o[0] = lw1
    g_o[0] = g
    bo_o[0] = rks * v


def _rw_prep(p_f, mu_main, mu_lora, w0, w2p, a0, a2p, g2p, k_k, k_a, r_k, e, et):
    b, t, _ = p_f.shape
    cm, cl = 3 * RW_W, PF_LORA_W
    col = {cm: 0, cl: PF_LORA_OFF // PF_LORA_W}
    tt = min(128, t)
    nb = tt // SUBLANES
    nt8 = t // SUBLANES

    def main_spec(c):
        return pl.BlockSpec((1, tt, c), lambda bi, i: (bi, i, col[c]))

    def prev_spec(c):
        return pl.BlockSpec((1, SUBLANES, c), lambda bi, i: (bi, jnp.maximum(i * nb - 1, 0), col[c]))

    def next_spec(c):
        return pl.BlockSpec((1, SUBLANES, c), lambda bi, i: (bi, jnp.minimum((i + 1) * nb, nt8 - 1), col[c]))

    def full(x):
        nd = x.ndim
        return pl.BlockSpec(x.shape, lambda bi, i: (0,) * nd)

    consts = (mu_main, mu_lora, w0, w2p, a0, a2p, g2p, k_k, k_a, r_k, e, et)
    out_sds = jax.ShapeDtypeStruct((b, t, RW_W), F32)
    return pl.pallas_call(
        _rw_prep_body,
        grid=(b, t // tt),
        in_specs=[main_spec(cm), prev_spec(cm), next_spec(cm), main_spec(cl), prev_spec(cl), next_spec(cl)]
        + [full(x) for x in consts],
        out_specs=[pl.BlockSpec((1, tt, RW_W), lambda bi, i: (bi, i, 0))] * 9,
        out_shape=[out_sds] * 9,
        compiler_params=_params(("parallel", "parallel"), 56),
    )(p_f, p_f, p_f, p_f, p_f, p_f, *consts)


def _rw_scan_body(r_ref, k_ref, v_ref, kn_ref, ab_ref, lw_ref, y_ref, s_ref, *, rev):
    L = RW_CHUNK

    @pl.when(pl.program_id(1) == 0)
    def _():
        s_ref[...] = jnp.zeros_like(s_ref)

    row = lax.broadcasted_iota(jnp.int32, (L, L), 0)
    col = lax.broadcasted_iota(jnp.int32, (L, L), 1)
    if rev:
        incl = col >= row
        strict = col > row
    else:
        incl = col <= row
        strict = col < row
    tri = jnp.where(incl, 1.0, 0.0).astype(BF16)
    eye = jnp.where(row == col, 1.0, 0.0).astype(F32)
    lane = lax.broadcasted_iota(jnp.int32, (L, LANES), 1)
    head_masks = (lane < RW_HEAD, lane >= RW_HEAD)
    sr = lax.broadcasted_iota(jnp.int32, (LANES, LANES), 0)
    sc = lax.broadcasted_iota(jnp.int32, (LANES, LANES), 1)
    bd = (sr < RW_HEAD) == (sc < RW_HEAD)
    r4 = lax.broadcasted_iota(jnp.int32, (4 * L, 2 * L), 0)
    c4 = lax.broadcasted_iota(jnp.int32, (4 * L, 2 * L), 1) & (L - 1)
    t4 = r4 & (L - 1)
    own = jnp.where((r4 & L) != 0, 1, 0)
    causal4 = (c4 > t4 - own) if rev else (c4 < t4 + own)

    heads =[(hp, h) for hp in range(RW_HEADS // 2) for h in range(2)]
    sls = [slice(hp * LANES, (hp + 1) * LANES) for hp in range(RW_HEADS // 2)]

    def intra(offs):
        pre = []
        for off in offs:
            rows = slice(off, off + L)
            lw = lw_ref[0, rows, :]
            cum = _dot_sel(tri, lw)
            tot = cum[0:1, :] if rev else cum[L - 1:L, :]
            p_inv = jnp.exp(-cum)
            p_end = jnp.exp(tot - cum)
            ab = ab_ref[0, rows, :]
            k = k_ref[0, rows, :]
            pre.append(dict(at=-kn_ref[0, rows, :] * jnp.exp(cum - lw), rt=r_ref[0, rows, :] * jnp.exp(cum),
                            bt=ab * p_inv, kt=k * p_inv, v=v_ref[0, rows, :], bt_end=ab * p_end, kt_end=k * p_end,
                            p_all=jnp.exp(tot)))
        zeros_lv = jnp.zeros((L, LANES), F32)
        nck = len(offs)
        at_h = [[] for _ in range(nck)]
        blks = [[] for _ in range(nck)]
        for ci, c in enumerate(pre):
            for hp in range(RW_HEADS // 2):
                sl = sls[hp]
                rows4 = []
                for h in range(2):
                    at_m = jnp.where(head_masks[h], c["at"][:, sl], 0.0)
                    at_h[ci].append(at_m)
                    rows4 += [at_m, jnp.where(head_masks[h], c["rt"][:, sl], 0.0)]
                blk = _dotp(jnp.concatenate(rows4, axis=0), jnp.concatenate([c["bt"][:, sl], c["kt"][:, sl]], axis=0),
                            NT, RW_PASSES_A)
                blks[ci].append(jnp.where(causal4, blk, 0.0))
        a_ab = [[blks[ci][hp][2 * h * L:(2 * h + 1) * L, :L] for hp, h in heads] for ci in range(nck)]
        a_rb = [[blks[ci][hp][(2 * h + 1) * L:(2 * h + 2) * L, :L] for hp, h in heads] for ci in range(nck)]
        av_pair = [[_dotp(blks[ci][hp], jnp.concatenate([zeros_lv, pre[ci]["v"][:, sls[hp]]], axis=0), NN, RW_PASSES_A)
                    for hp in range(RW_HEADS // 2)] for ci in range(nck)]
        av = [[av_pair[ci][hp][2 * h * L:(2 * h + 2) * L] for hp, h in heads] for ci in range(nck)]
        tm = [[eye + a for a in a_ab[ci]] for ci in range(nck)]
        pw = [[_dotp(a, a, NN, RW_PASSES_INV) for a in a_ab[ci]] for ci in range(nck)]
        for _ in range(4):
            st = [[_dotp(jnp.concatenate([t_, p], axis=0), p, NN, RW_PASSES_INV) for t_, p in zip(tm[ci], pw[ci])]
                  for ci in range(nck)]
            tm = [[t_ + s_[:L] for t_, s_ in zip(tm[ci], st[ci])] for ci in range(nck)]
            pw = [[s_[L:] for s_ in st[ci]] for ci in range(nck)]
        tm = [[t_ + _dotp(t_, p, NN, RW_PASSES_INV) for t_, p in zip(tm[ci], pw[ci])] for ci in range(nck)]
        out = []
        for ci, c in enumerate(pre):
            tx = [_dotp(tm[ci][i], jnp.concatenate([at_h[ci][i], jnp.where(head_masks[h], av[ci][i][:L], 0.0)], axis=1),
                        NN, RW_PASSES_A) for i, (hp, h) in enumerate(heads)]
            out.append(dict(tx=tx, av=av[ci], a_rb=a_rb[ci], rt=c["rt"], v=c["v"], bt_end=c["bt_end"],
                            kt_end=c["kt_end"], p_all=c["p_all"]))
        return out

    def sequential(c, off, states):
        tx, av, a_rb = c["tx"], c["av"], c["a_rb"]
        out = []
        for hp in range(RW_HEADS // 2):
            sl = sls[hp]
            i0, i1 = 2 * hp, 2 * hp + 1
            w_pair = tx[i0][:, :LANES] + tx[i1][:, :LANES]
            u0_pair = tx[i0][:, LANES:] + tx[i1][:, LANES:]
            y0_pair = jnp.where(head_masks[0], av[i0][L:], av[i1][L:])
            s = states[hp]
            ws = _dotp(jnp.concatenate([w_pair, c["rt"][:, sl]], axis=0), s, NT, RW_PASSES_S)
            u = ws[:L] + u0_pair
            y = ws[L:] + y0_pair + jnp.where(head_masks[0], _dotp(a_rb[i0], u, NN, RW_PASSES_A),
                                             _dotp(a_rb[i1], u, NN, RW_PASSES_A))
            uv_t = jnp.concatenate([u, c["v"][:, sl]], axis=0).T
            upd = _dotp(uv_t, jnp.concatenate([c["bt_end"][:, sl], c["kt_end"][:, sl]], axis=0), NN, RW_PASSES_S)
            out.append(s * c["p_all"][:, sl] + jnp.where(bd, upd, 0.0))
            y_ref[0, off:off + L, sl] = y
        return out

    offs = [c * L for c in range(RW_SUB)]
    if rev:
        offs = offs[::-1]
    chunks = intra(offs)
    states = [s_ref[hp] for hp in range(RW_HEADS // 2)]
    for c, off in zip(chunks, offs):
        states = sequential(c, off, states)
    for hp in range(RW_HEADS // 2):
        s_ref[hp] = states[hp]


def _rw_scan(r, k, v, kn, ab, lw, rev):
    b, t, w = r.shape
    tb = RW_CHUNK * RW_SUB
    nc = t // tb
    if rev:
        idx = lambda bi, c: (bi, nc - 1 - c, 0)
    else:
        idx = lambda bi, c: (bi, c, 0)
    spec = pl.BlockSpec((1, tb, w), idx)
    return pl.pallas_call(
        functools.partial(_rw_scan_body, rev=rev),
        grid=(b, nc),
        in_specs=[spec] * 6,
        out_specs=spec,
        out_shape=jax.ShapeDtypeStruct((b, t, w), F32),
        scratch_shapes=[pltpu.VMEM((RW_HEADS // 2, LANES, LANES), F32)],
        compiler_params=_params(("parallel", "arbitrary"), 32),
    )(r, k, v, kn, ab, lw)


def _ml_scan_body(q_ref, k_ref, v_ref, g_ref, b_ref, h_ref, c_ref, n_ref, m_ref, *, rev, direction):
    L = ML_CHUNK

    @pl.when(pl.program_id(1) == 0)
    def _():
        c_ref[...] = jnp.zeros_like(c_ref)
        n_ref[...] = jnp.zeros_like(n_ref)
        m_ref[...] = jnp.zeros_like(m_ref)

    row = lax.broadcasted_iota(jnp.int32, (L, L), 0)
    col = lax.broadcasted_iota(jnp.int32, (L, L), 1)
    incl = (col >= row) if rev else (col <= row)
    tri = jnp.where(incl, 1.0, 0.0).astype(BF16)
    lane = lax.broadcasted_iota(jnp.int32, (L, LANES), 1)
    sub = lax.broadcasted_iota(jnp.int32, (LANES, L), 0)

    g = g_ref[0] + b_ref[...]
    ls = jnp.minimum(g, 0.0) - jnp.log(1.0 + jnp.exp(-jnp.abs(g)))
    bc = _dot_sel(tri, ls)
    g_t = g.T
    bc_t = bc.T
    scale = ML_QK ** -0.5
    for h in range(ML_HEADS):
        ci = direction * ML_HEADS + h
        cf = 2 * ML_HEADS + direction * ML_HEADS + h
        b_col = jnp.sum(jnp.where(lane == cf, bc, 0.0), axis=1, keepdims=True)
        i_col = jnp.sum(jnp.where(lane == ci, g, 0.0), axis=1, keepdims=True)
        b_row = jnp.sum(jnp.where(sub == cf, bc_t, 0.0), axis=0, keepdims=True)
        i_row = jnp.sum(jnp.where(sub == ci, g_t, 0.0), axis=0, keepdims=True)
        m_prev = jnp.max(m_ref[h:h + 1, :], axis=1, keepdims=True)
        qs = q_ref[0, :, h * ML_QK:(h + 1) * ML_QK].astype(F32) * scale
        kc = k_ref[0, :, h * ML_QK:(h + 1) * ML_QK].astype(F32)
        vc = v_ref[0, :, h * ML_V:(h + 1) * ML_V].astype(F32)
        dmat = jnp.where(incl, b_col - b_row + i_row, -jnp.inf)
        inter = b_col + m_prev
        m_t = jnp.maximum(inter, jnp.max(dmat, axis=1, keepdims=True))
        s = _dot1(qs, kc, NT) * jnp.exp(dmat - m_t)
        e_inter = jnp.exp(inter - m_t)
        num = _dot1(s, vc) + e_inter * _dot1(qs, c_ref[h])
        den = jnp.sum(s, axis=1, keepdims=True) + e_inter * jnp.sum(qs * n_ref[h:h + 1, :], axis=1, keepdims=True)
        h_ref[0, :, h * ML_V:(h + 1) * ML_V] = num / jnp.maximum(jnp.abs(den), jnp.exp(-m_t))
        b_last = jnp.min(b_row, axis=1, keepdims=True)
        g_row = b_last - b_row + i_row
        g_col = b_last - b_col + i_col
        m_new = jnp.maximum(b_last + m_prev, jnp.max(g_row, axis=1, keepdims=True))
        w_col = jnp.exp(g_col - m_new)
        dec = jnp.exp(b_last + m_prev - m_new)
        c_ref[h] = dec * c_ref[h] + _dot1(kc.T, vc * w_col)
        n_ref[h:h + 1, :] = dec * n_ref[h:h + 1, :] + jnp.sum(kc * w_col, axis=0, keepdims=True)
        m_ref[h:h + 1, :] = jnp.broadcast_to(m_new, (1, LANES))


def _ml_scan(p_b, p_f, bias, rev):
    b, t, _ = p_b.shape
    L = ML_CHUNK
    nc = t // L
    tix = (lambda c: nc - 1 - c) if rev else (lambda c: c)
    return pl.pallas_call(
        functools.partial(_ml_scan_body, rev=rev, direction=1 if rev else 0),
        grid=(b, nc),
        in_specs=[pl.BlockSpec((1, L, ML_QKW), lambda bi, c: (bi, tix(c), 0)),
                  pl.BlockSpec((1, L, ML_QKW), lambda bi, c: (bi, tix(c), 1)),
                  pl.BlockSpec((1, L, ML_VW), lambda bi, c: (bi, tix(c), 1)),
                  pl.BlockSpec((1, L, LANES), lambda bi, c: (bi, tix(c), PF_MLG_OFF // LANES)),
                  pl.BlockSpec((1, LANES), lambda bi, c: (0, 0))],
        out_specs=pl.BlockSpec((1, L, ML_VW), lambda bi, c: (bi, tix(c), 0)),
        out_shape=jax.ShapeDtypeStruct((b, t, ML_VW), F32),
        scratch_shapes=[pltpu.VMEM((ML_HEADS, ML_QK, ML_V), F32),
                        pltpu.VMEM((SUBLANES, LANES), F32),
                        pltpu.VMEM((SUBLANES, LANES), F32)],
        compiler_params=_params(("parallel", "arbitrary"), 32),
    )(p_b, p_b, p_b, p_f, bias)


def _post_body(yf_ref, yb_ref, bo_ref, g_ref, lnw_ref, lnb_ref, e_ref, et_ref,
               hf_ref, hb_ref, o_ref, nw_ref, ya_ref, yb_o_ref):
    e = e_ref[...]
    et = et_ref[...]
    y = yf_ref[...] + yb_ref[...]
    inv = 1.0 / RW_HEAD
    mu = _dot1(_dot1(y, e), et) * inv
    yc = y - mu
    var = _dot1(_dot1(yc * yc, e), et) * inv
    yn = yc * lax.rsqrt(var + RW_GN_EPS) * lnw_ref[...] + lnb_ref[...]
    ya_ref[...] = ((yn + bo_ref[...]) * g_ref[...]).astype(ya_ref.dtype)

    hsum = hf_ref[...] + hb_ref[...]
    parts = []
    for h in range(ML_HEADS):
        hh = hsum[:, h * ML_V:(h + 1) * ML_V]
        parts.append(hh * lax.rsqrt(jnp.mean(hh * hh, axis=-1, keepdims=True) + ML_NORM_EPS))
    hn = jnp.concatenate(parts, axis=1) * nw_ref[...]
    yb_o_ref[...] = (hn * _sigmoid(o_ref[...].astype(F32))).astype(yb_o_ref.dtype)


def _post(yf, yb, bonus, g, ln_w, ln_b, e, et, hf, hb, vo, norm_w):
    n = yf.shape[0]
    tm = min(256, n)
    tok = lambda c: pl.BlockSpec((tm, c), lambda i: (i, 0))
    const = lambda x: pl.BlockSpec(x.shape, lambda i: (0, 0))
    return pl.pallas_call(
        _post_body,
        grid=(n // tm,),
        in_specs=[tok(RW_W), tok(RW_W), tok(RW_W), tok(RW_W), const(ln_w), const(ln_b), const(e), const(et),
                  tok(ML_VW), tok(ML_VW), pl.BlockSpec((tm, ML_VW), lambda i: (i, (2 * ML_QKW + ML_VW) // ML_VW)),
                  const(norm_w)],
        out_specs=[tok(RW_W), tok(ML_VW)],
        out_shape=[jax.ShapeDtypeStruct((n, RW_W), BF16), jax.ShapeDtypeStruct((n, ML_VW), BF16)],
        compiler_params=_params(("parallel",), 40),
    )(yf, yb, bonus, g, ln_w, ln_b, e, et, hf, hb, vo, norm_w)


def _merge_body(ya_ref, yb_ref, pa_ref, pb_ref, ga_ref, gb_ref, ba_ref, bb_ref, o_ref):
    pa = _mm(ya_ref[...], pa_ref[...])
    pb = _mm(yb_ref[...], pb_ref[...])
    ga = ga_ref[...].astype(F32) + ba_ref[...]
    gb = gb_ref[...].astype(F32) + bb_ref[...]
    o_ref[...] = (_sigmoid(ga) * pa + _sigmoid(gb) * pb).astype(o_ref.dtype)


def _merge(ya, yb, p_a, p_b, proj_b, b_gate):
    n = ya.shape[0]
    tm = min(1024, n)
    tn = 1024
    nj = D_MODEL // tn
    g0 = PB_GATE_OFF // tn
    return pl.pallas_call(
        _merge_body,
        grid=(n // tm, nj),
        in_specs=[pl.BlockSpec((tm, RW_W), lambda i, j: (i, 0)), pl.BlockSpec((tm, ML_VW), lambda i, j: (i, 0)),
                  pl.BlockSpec((RW_W, tn), lambda i, j: (0, j)), pl.BlockSpec((ML_VW, tn), lambda i, j: (0, j)),
                  pl.BlockSpec((tm, tn), lambda i, j: (i, g0 + j)), pl.BlockSpec((tm, tn), lambda i, j: (i, g0 + nj + j)),
                  pl.BlockSpec((1, tn), lambda i, j: (0, j)), pl.BlockSpec((1, tn), lambda i, j: (0, j + nj))],
        out_specs=pl.BlockSpec((tm, tn), lambda i, j: (i, j)),
        out_shape=jax.ShapeDtypeStruct((n, D_MODEL), BF16),
        compiler_params=_params(("parallel", "parallel"), 40),
    )(ya, yb, p_a, p_b, proj_b, proj_b, b_gate, b_gate)


def _resid_mm_body(a_ref, w_ref, x_ref, o_ref):
    o_ref[...] = x_ref[...] + _mm(a_ref[...], w_ref[...])


def _resid_mm(a, w, x):
    n, kdim = a.shape
    c = w.shape[1]
    tm = min(1024, n)
    tn = 1024
    return pl.pallas_call(
        _resid_mm_body,
        grid=(n // tm, c // tn),
        in_specs=[pl.BlockSpec((tm, kdim), lambda i, j: (i, 0)), pl.BlockSpec((kdim, tn), lambda i, j: (0, j)),
                  pl.BlockSpec((tm, tn), lambda i, j: (i, j))],
        out_specs=pl.BlockSpec((tm, tn), lambda i, j: (i, j)),
        out_shape=jax.ShapeDtypeStruct((n, c), F32),
        compiler_params=_params(("parallel", "parallel"), 48),
    )(a, w, x)


def _topk_rows(s, k, payload=None):
    nrow, t = s.shape
    rid = lax.broadcasted_iota(jnp.int32, (nrow, t), 0).astype(F32)
    kid = lax.broadcasted_iota(jnp.int32, (k, t), 0)
    vals = jnp.zeros((k, t), F32)
    sel = jnp.zeros((k, t), F32)
    for j in range(k):
        m = jnp.max(s, axis=0, keepdims=True)
        pos = jnp.min(jnp.where(s == m, rid, float(nrow)), axis=0, keepdims=True)
        hit = rid == pos
        if payload is None:
            picked = pos
        else:
            picked = jnp.max(jnp.where(hit, payload, -1.0), axis=0, keepdims=True)
        vals = jnp.where(kid == j, m, vals)
        sel = jnp.where(kid == j, picked, sel)
        s = jnp.where(hit, -jnp.inf, s)
    return vals, sel


def _router_body(q_ref, keys_ref, a_ref, b_ref, gw_ref):
    K = PEER_TOPK
    a_parts, b_parts, w_parts = [], [], []
    for h in range(PEER_HEADS):
        sv, si = [], []
        for p in range(2):
            c0 = (h * 2 + p) * PEER_HALF
            st = _dot3(keys_ref[h * 2 + p], q_ref[:, c0:c0 + PEER_HALF], NT)
            vals, idx = _topk_rows(st, K)
            sv.append(vals)
            si.append(idx)
        jid = lax.broadcasted_iota(jnp.int32, (SUBLANES, sv[0].shape[1]), 0)
        c_parts = [sv[0][0:1, :] + sv[1]]
        i_parts = [si[0][0:1, :] * float(PEER_NKEYS) + si[1]]
        for i in range(1, SUBLANES):
            keep = jid < K // (i + 1)
            c_parts.append(jnp.where(keep, sv[0][i:i + 1, :] + sv[1][0:SUBLANES, :], -jnp.inf))
            i_parts.append(si[0][i:i + 1, :] * float(PEER_NKEYS) + si[1][0:SUBLANES, :])
        c_parts.append(sv[0][SUBLANES:K, :] + sv[1][0:1, :])
        i_parts.append(si[0][SUBLANES:K, :] * float(PEER_NKEYS) + si[1][0:1, :])
        cand = jnp.concatenate(c_parts, axis=0)
        cidx = jnp.concatenate(i_parts, axis=0)
        best, eidx = _topk_rows(cand, K, payload=cidx)
        ex = jnp.exp(best - best[0:1, :])
        w_parts.append(ex / jnp.sum(ex, axis=0, keepdims=True))
        hi = jnp.floor(eidx * (1.0 / PEER_NKEYS))
        a_parts.append(hi)
        b_parts.append(eidx - hi * float(PEER_NKEYS))
    a_ref[...] = jnp.concatenate(a_parts, axis=0).T
    b_ref[...] = jnp.concatenate(b_parts, axis=0).T
    gw_ref[...] = jnp.concatenate(w_parts, axis=0).T


def _router(q, keys):
    n = q.shape[0]
    tq = min(256, n)
    tok = pl.BlockSpec((tq, PEER_HK), lambda i: (i, 0))
    sds = jax.ShapeDtypeStruct((n, PEER_HK), F32)
    return pl.pallas_call(
        _router_body,
        grid=(n // tq,),
        in_specs=[pl.BlockSpec((tq, q.shape[1]), lambda i: (i, 0)),
                  pl.BlockSpec(keys.shape, lambda i: (0, 0, 0))],
        out_specs=[tok, tok, tok],
        out_shape=[sds, sds, sds],
        compiler_params=_params(("parallel",), 32),
    )(q, keys)


PEER_CI = 4
PEER_CE = PEER_CI * PEER_NKEYS


PEER_SPLIT = 2
PEER_KEYS_PER_CALL = PEER_NKEYS // PEER_SPLIT


def _peer_body(x_ref, h_ref, a_ref, b_ref, w_ref, ut_ref, v_ref, gf_ref, o_ref, g2_ref, z_ref, *,
               pitch, key0, final_norm):
    j = pl.program_id(1)
    last = pl.num_programs(1) - 1
    t = x_ref.shape[0]
    cur = j % 2

    def produce():
        z_ref[cur] = _mm(h_ref[...], ut_ref[...])

    def consume():
        z = z_ref[1 - cur]
        parts = []
        for ii in range(PEER_CI):
            start = pl.multiple_of(((j - 1) * PEER_CI + ii) * pitch, SUBLANES)
            gblk = g2_ref[pl.ds(start, t), :]
            zz = z[:, ii * PEER_NKEYS:(ii + 1) * PEER_NKEYS]
            act = 0.5 * zz * (1.0 + lax.erf(zz * (2.0 ** -0.5)))
            parts.append((act * gblk).astype(BF16))
        o_ref[...] += _mm(jnp.concatenate(parts, axis=1), v_ref[...])

    @pl.when(j == 0)
    def _():
        o_ref[...] = x_ref[...]
        kid = lax.broadcasted_iota(jnp.int32, (PEER_KEYS_PER_CALL, PEER_HK), 0).astype(F32) + float(key0)
        kid_b = lax.broadcasted_iota(jnp.int32, (PEER_NKEYS, PEER_HK), 0).astype(F32)

        def per_token(ti, carry):
            arow = a_ref[pl.ds(ti, 1), :]
            brow = b_ref[pl.ds(ti, 1), :]
            wrow = w_ref[pl.ds(ti, 1), :]
            lhs = jnp.where(kid == arow, wrow, 0.0).astype(BF16)
            rhs = jnp.where(kid_b == brow, 1.0, 0.0).astype(BF16)
            g_t = _mm(lhs, rhs, NT)
            g2_ref[pl.ds(ti, PEER_KEYS_PER_CALL, stride=pitch), :] = g_t
            return carry

        lax.fori_loop(0, t, per_token, 0, unroll=16)
        produce()

    @pl.when(jnp.logical_and(j > 0, j < last))
    def _():
        produce()
        consume()

    @pl.when(j == last)
    def _():
        consume()

    if final_norm:
        @pl.when(j == last)
        def _():
            y = o_ref[...]
            o_ref[...] = y * lax.rsqrt(jnp.mean(y * y, axis=-1, keepdims=True) + RMS_EPS) * gf_ref[...]


def _peer(x, h, a_idx, b_idx, gw, u_t, v_tab, g_final, final_norm):
    n, d = x.shape
    t = min(512, n)
    pitch = t + SUBLANES
    nch = PEER_N // PEER_CE // PEER_SPLIT
    once = pl.Buffered(1)
    tok = lambda c: pl.BlockSpec((t, c), lambda i, j: (i, 0), pipeline_mode=once)
    for part in range(PEER_SPLIT):
        c0 = part * nch
        is_last = part == PEER_SPLIT - 1
        x = pl.pallas_call(
            functools.partial(_peer_body, pitch=pitch, key0=part * PEER_KEYS_PER_CALL,
                              final_norm=final_norm and is_last),
            grid=(n // t, nch + 1),
            in_specs=[tok(d), tok(d), tok(PEER_HK), tok(PEER_HK), tok(PEER_HK),
                      pl.BlockSpec((d, PEER_CE), lambda i, j, c0=c0: (0, c0 + jnp.minimum(j, nch - 1))),
                      pl.BlockSpec((PEER_CE, d), lambda i, j, c0=c0: (c0 + jnp.maximum(j - 1, 0), 0)),
                      pl.BlockSpec((1, d), lambda i, j: (0, 0))],
            out_specs=pl.BlockSpec((t, d), lambda i, j: (i, 0)),
            out_shape=jax.ShapeDtypeStruct((n, d), F32),
            scratch_shapes=[pltpu.VMEM((PEER_KEYS_PER_CALL * pitch, LANES), F32),
                            pltpu.VMEM((2, t, PEER_CE), F32)],
            compiler_params=_params(("parallel", "arbitrary"), 56),
        )(x, h, a_idx, b_idx, gw, u_t, v_tab, g_final)
    return x


def _pad_cols(pieces, width):
    rows = pieces[0][0].shape[0]
    out = jnp.zeros((rows, width), pieces[0][0].dtype)
    for arr, off in pieces:
        out = out.at[:, off:off + arr.shape[1]].set(arr)
    return out


def _pad_rows(x, rows):
    return jnp.zeros((rows,) + x.shape[1:], x.dtype).at[:x.shape[0]].set(x)


def _lora_layout(x):
    o = 3 * RW_W
    return _pad_cols([(x[:, o:o + 64], 0), (x[:, o + 64:o + 128], 128), (x[:, o + 128:o + 288], 256)], 512)


def _prep_layer(l, w):
    f = {}
    w_in = w["w_in"][l]
    rw = w_in[:, :RW_COLS]
    ml = w_in[:, RW_COLS:RW_COLS + ML_COLS]
    gt = w_in[:, RW_COLS + ML_COLS:]
    f["norm_mix"] = w["norm_mix"][l][None, :]
    f["w_f"] = _pad_cols([(rw[:, :3 * RW_W], 0), (_lora_layout(rw), PF_LORA_OFF),
                          (ml[:, 2 * ML_QKW + 2 * ML_VW:], PF_MLG_OFF)], PF_COLS).astype(BF16)
    f["w_b"] = jnp.concatenate([ml[:, :2 * ML_QKW + 2 * ML_VW], gt], axis=1).astype(BF16)
    mu = jnp.stack([w["rw_mu_prev"][l], w["rw_mu_next"][l]])
    f["mu_main"] = mu[:, :3 * RW_W]
    f["mu_lora"] = _lora_layout(mu)
    f["w0"] = w["rw_w0"][l]
    f["w2p"] = jnp.stack([_pad_rows(w["rw_w2"][l, 0], 128), _pad_rows(w["rw_w2"][l, 1], 128)]).astype(BF16)
    f["a0"] = w["rw_a0"][l][None, :]
    f["a2p"] = _pad_rows(w["rw_a2"][l], 128).astype(BF16)
    f["g2p"] = _pad_rows(w["rw_g2"][l], 256).astype(BF16)
    f["k_k"] = w["rw_k_k"][l][None, :]
    f["k_a"] = w["rw_k_a"][l][None, :]
    f["r_k"] = w["rw_r_k"][l].reshape(1, RW_W)
    f["ln_w"] = w["rw_ln_w"][l][None, :]
    f["ln_b"] = w["rw_ln_b"][l][None, :]
    f["ml_bias"] = _pad_cols([(w["ml_b_i"][l].reshape(1, -1), 0), (w["ml_b_f"][l].reshape(1, -1), 2 * ML_HEADS)], LANES)
    f["ml_norm_w"] = w["ml_norm_w"][l][None, :]
    f["p_a"] = w["p_a"][l].astype(BF16)
    f["p_b"] = w["p_b"][l].astype(BF16)
    f["b_gate"] = w["b_gate"][l][None, :]
    f["w_out"] = w["w_out"][l].astype(BF16)
    f["norm_ffn"] = w["norm_ffn"][l][None, :]
    f["wq"] = (w["peer_wq"][l].astype(BF16),)
    f["keys"] = w["peer_keys"][l].reshape(PEER_HEADS * 2, PEER_NKEYS, PEER_HALF)
    f["u_t"] = w["peer_u"][l].astype(BF16).T
    f["v"] = w["peer_v"][l].astype(BF16)
    return f


def _head_selectors():
    ch = jnp.arange(RW_W)[:, None] // RW_HEAD
    e = (ch == jnp.arange(LANES)[None, :]).astype(BF16)
    return e, e.T


def _layer(x, f, e, et, g_final, final_norm):
    b, t, d = x.shape
    n = b * t
    x2 = x.reshape(n, d)
    (hn,) = _norm_cast(x2, f["norm_mix"], 1)
    p_f = _matmul((hn,), (f["w_f"],), F32)
    p_b = _matmul((hn,), (f["w_b"],), BF16)
    p_f3 = p_f.reshape(b, t, -1)
    p_b3 = p_b.reshape(b, t, -1)

    r, k, v, kn, ab, lw0, lw1, gg, bonus = _rw_prep(
        p_f3, f["mu_main"], f["mu_lora"], f["w0"], f["w2p"], f["a0"], f["a2p"], f["g2p"], f["k_k"], f["k_a"],
        f["r_k"], e, et)
    y_f = _rw_scan(r, k, v, kn, ab, lw0, rev=False)
    y_b = _rw_scan(r, k, v, kn, ab, lw1, rev=True)

    h_f = _ml_scan(p_b3, p_f3, f["ml_bias"], rev=False)
    h_b = _ml_scan(p_b3, p_f3, f["ml_bias"], rev=True)

    ya, yb = _post(y_f.reshape(n, -1), y_b.reshape(n, -1), bonus.reshape(n, -1), gg.reshape(n, -1),
                   f["ln_w"], f["ln_b"], e, et, h_f.reshape(n, -1), h_b.reshape(n, -1), p_b, f["ml_norm_w"])
    merged = _merge(ya, yb, f["p_a"], f["p_b"], p_b, f["b_gate"])
    x1 = _resid_mm(merged, f["w_out"], x2)

    h_parts = _norm_cast(x1, f["norm_ffn"], 1)
    q = _matmul(h_parts, f["wq"], F32)
    a_idx, b_idx, gw = _router(q, f["keys"])
    x_out = _peer(x1, h_parts[0], a_idx, b_idx, gw, f["u_t"], f["v"], g_final, final_norm)
    return x_out.reshape(b, t, d)


def _trunk(x, layers, e, et, g_final):
    for l, f in enumerate(layers):
        x = _layer(x, f, e, et, g_final, final_norm=(l == len(layers) - 1))
    return x


def kernel(x_prompt, x_sample, norm_mix, w_in, rw_mu_prev, rw_mu_next, rw_w0, rw_w2, rw_a0, rw_a2, rw_g2, rw_k_k, rw_k_a, rw_r_k, rw_ln_w, rw_ln_b, ml_b_i, ml_b_f, ml_norm_w, p_a, p_b, b_gate, w_out, norm_ffn, peer_wq, peer_keys, peer_u, peer_v, norm_final):
    w = dict(norm_mix=norm_mix, w_in=w_in, rw_mu_prev=rw_mu_prev, rw_mu_next=rw_mu_next, rw_w0=rw_w0, rw_w2=rw_w2,
             rw_a0=rw_a0, rw_a2=rw_a2, rw_g2=rw_g2, rw_k_k=rw_k_k, rw_k_a=rw_k_a, rw_r_k=rw_r_k, rw_ln_w=rw_ln_w,
             rw_ln_b=rw_ln_b, ml_b_i=ml_b_i, ml_b_f=ml_b_f, ml_norm_w=ml_norm_w, p_a=p_a, p_b=p_b, b_gate=b_gate,
             w_out=w_out, norm_ffn=norm_ffn, peer_wq=peer_wq, peer_keys=peer_keys, peer_u=peer_u, peer_v=peer_v)
    depth = w_in.shape[0]
    layers = [_prep_layer(l, w) for l in range(depth)]
    e, et = _head_selectors()
    g_final = norm_final[None, :]
    return (_trunk(x_prompt, layers, e, et, g_final), _trunk(x_sample, layers, e, et, g_final))
```

```python
import functools
import math

import jax
import jax.numpy as jnp
from jax import lax
from jax.experimental import pallas as pl
from jax.experimental.pallas import tpu as pltpu

F32 = jnp.float32
BF16 = jnp.bfloat16

D_MODEL = 2048
RMS_EPS = 1e-6
RW_W = 1024
RW_HEAD = 64
RW_HEADS = 16
RW_DECAY_LORA = 64
RW_A_LORA = 64
RW_G_LORA = 160
RW_GN_EPS = 64e-5
RW_COLS = 3 * RW_W + RW_DECAY_LORA + RW_A_LORA + RW_G_LORA
RW_CHUNK = 64
RW_SUB = 2
RW_PASSES_A = 1
RW_PASSES_INV = 1
RW_PASSES_S = 1
ML_HEADS = 4
ML_QK = 128
ML_V = 256
ML_QKW = ML_HEADS * ML_QK
ML_VW = ML_HEADS * ML_V
ML_CHUNK = 128
ML_NORM_EPS = 1e-6
ML_COLS = 2 * ML_QKW + 2 * ML_VW + 4 * ML_HEADS
PF_LORA_OFF = 3 * RW_W
PF_LORA_W = 512
PF_MLG_OFF = PF_LORA_OFF + PF_LORA_W
PF_COLS = 4096
PB_GATE_OFF = 2 * ML_QKW + 2 * ML_VW
PB_COLS = PB_GATE_OFF + 2 * D_MODEL
PEER_HEADS = 8
PEER_NKEYS = 128
PEER_N = PEER_NKEYS * PEER_NKEYS
PEER_HALF = 128
PEER_TOPK = 16
PEER_HK = PEER_HEADS * PEER_TOPK
LANES = 128
SUBLANES = 8

NN = ((1,), (0,))
NT = ((1,), (1,))


def _mm(a, b, dims=NN):
    return lax.dot_general(a, b, (dims, ((), ())), preferred_element_type=F32)


def _split2(a):
    hi = a.astype(BF16)
    lo = (a - hi.astype(F32)).astype(BF16)
    return hi, lo


def _split3(a):
    hi = a.astype(BF16)
    r1 = a - hi.astype(F32)
    mid = r1.astype(BF16)
    lo = (r1 - mid.astype(F32)).astype(BF16)
    return hi, mid, lo


def _dot1(a, b, dims=NN):
    return _mm(a.astype(BF16), b.astype(BF16), dims)


def _dot3(a, b, dims=NN):
    ah, al = _split2(a)
    bh, bl = _split2(b)
    return _mm(ah, bh, dims) + (_mm(ah, bl, dims) + _mm(al, bh, dims))


def _dotp(a, b, dims, passes):
    return _dot3(a, b, dims) if passes == 3 else _dot1(a, b, dims)


def _dot_sel(sel_bf16, x, dims=NN):
    h, m, l = _split3(x)
    return _mm(sel_bf16, h, dims) + (_mm(sel_bf16, m, dims) + _mm(sel_bf16, l, dims))


def _dot_xsel(x, sel_bf16, dims=NN):
    h, m, l = _split3(x)
    return _mm(h, sel_bf16, dims) + (_mm(m, sel_bf16, dims) + _mm(l, sel_bf16, dims))


def _params(sem, vmem_mb):
    return pltpu.CompilerParams(dimension_semantics=sem, vmem_limit_bytes=vmem_mb << 20)


def _sigmoid(x):
    return 1.0 / (1.0 + jnp.exp(-x))


def _norm_cast_body(x_ref, g_ref, *o_refs):
    xf = x_ref[...]
    h = xf * lax.rsqrt(jnp.mean(xf * xf, axis=-1, keepdims=True) + RMS_EPS) * g_ref[...]
    hi = h.astype(BF16)
    o_refs[0][...] = hi
    if len(o_refs) == 2:
        o_refs[1][...] = (h - hi.astype(F32)).astype(BF16)


def _norm_cast(x, g, n_parts):
    n, d = x.shape
    tm = min(512, n)
    spec = pl.BlockSpec((tm, d), lambda i: (i, 0))
    return pl.pallas_call(
        _norm_cast_body,
        grid=(n // tm,),
        in_specs=[spec, pl.BlockSpec((1, d), lambda i: (0, 0))],
        out_specs=[spec] * n_parts,
        out_shape=[jax.ShapeDtypeStruct((n, d), BF16)] * n_parts,
        compiler_params=_params(("parallel",), 32),
    )(x, g)


def _matmul_body(*refs, n_parts):
    a_refs = refs[:n_parts]
    w_refs = refs[n_parts:2 * n_parts]
    o_ref = refs[2 * n_parts]
    acc = _mm(a_refs[0][...], w_refs[0][...])
    if n_parts == 2:
        acc = acc + (_mm(a_refs[0][...], w_refs[1][...]) + _mm(a_refs[1][...], w_refs[0][...]))
    o_ref[...] = acc.astype(o_ref.dtype)


def _matmul(a_parts, w_parts, out_dtype):
    n, d = a_parts[0].shape
    c = w_parts[0].shape[1]
    n_parts = len(a_parts)
    tm = min(1024, n)
    tn = min(1024 // n_parts, c)
    return pl.pallas_call(
        functools.partial(_matmul_body, n_parts=n_parts),
        grid=(n // tm, c // tn),
        in_specs=[pl.BlockSpec((tm, d), lambda i, j: (i, 0))] * n_parts
        + [pl.BlockSpec((d, tn), lambda i, j: (0, j))] * n_parts,
        out_specs=pl.BlockSpec((tm, tn), lambda i, j: (i, j)),
        out_shape=jax.ShapeDtypeStruct((n, c), out_dtype),
        compiler_params=_params(("parallel", "parallel"), 48),
    )(*a_parts, *w_parts)


def _shift_rows(p, prev_row, next_row, mu_prev, mu_next):
    n = p.shape[0]
    rid = lax.broadcasted_iota(jnp.int32, p.shape, 0)
    prev = jnp.where(rid == 0, prev_row, pltpu.roll(p, 1, 0))
    nxt = jnp.where(rid == n - 1, next_row, pltpu.roll(p, n - 1, 0))
    return p + mu_prev * (prev - p) + mu_next * (nxt - p)


def _rw_prep_body(m_ref, mp_ref, mn_ref, l_ref, lp_ref, ln_ref, mum_ref, mul_ref, w0_ref, w2_ref,
                  a0_ref, a2_ref, g2_ref, kk_ref, ka_ref, rk_ref, e_ref, et_ref,
                  r_o, k_o, v_o, kn_o, ab_o, lw0_o, lw1_o, g_o, bo_o):
    i = pl.program_id(1)
    first = i == 0
    last = i == pl.num_programs(1) - 1
    zero_m = jnp.zeros((1, m_ref.shape[2]), F32)
    zero_l = jnp.zeros((1, l_ref.shape[2]), F32)
    pm = _shift_rows(m_ref[0],
                     jnp.where(first, zero_m, mp_ref[0, SUBLANES - 1:SUBLANES, :]),
                     jnp.where(last, zero_m, mn_ref[0, 0:1, :]),
                     mum_ref[0:1, :], mum_ref[1:2, :])
    plo = _shift_rows(l_ref[0],
                      jnp.where(first, zero_l, lp_ref[0, SUBLANES - 1:SUBLANES, :]),
                      jnp.where(last, zero_l, ln_ref[0, 0:1, :]),
                      mul_ref[0:1, :], mul_ref[1:2, :])
    r = pm[:, 0:RW_W]
    k = pm[:, RW_W:2 * RW_W]
    v = pm[:, 2 * RW_W:3 * RW_W]
    wd = plo[:, 0:128]
    ad = plo[:, 128:256]
    gd = plo[:, 256:512]

    a = _sigmoid(a0_ref[...] + _dot1(ad, a2_ref[...]))
    g = _dot1(_sigmoid(gd), g2_ref[...])
    e = e_ref[...]
    et = et_ref[...]
    kk = k * kk_ref[...]
    ss = _dot1(_dot1(kk * kk, e), et)
    kn = kk / jnp.maximum(jnp.sqrt(ss), 1e-12)
    kmod = k * (1.0 + (a - 1.0) * ka_ref[...])
    wl = jnp.tanh(wd)
    c = math.exp(-0.5)
    lw0 = -_sigmoid(w0_ref[0:1, :] + _dot1(wl, w2_ref[0])) * c
    lw1 = -_sigmoid(w0_ref[1:2, :] + _dot1(wl, w2_ref[1])) * c
    rks = _dot1(_dot1(r * kmod * rk_ref[...], e), et)
    r_o[0] = r.astype(r_o.dtype)
    k_o[0] = kmod.astype(k_o.dtype)
    v_o[0] = v.astype(v_o.dtype)
    kn_o[0] = kn.astype(kn_o.dtype)
    ab_o[0] = (kn * a).astype(ab_o.dtype)
    lw0_o[0] = lw0
    lw1_o[0] = lw1
    g_o[0] = g.astype(g_o.dtype)
    bo_o[0] = (rks * v).astype(bo_o.dtype)


def _rw_prep(p_f, mu_main, mu_lora, w0, w2p, a0, a2p, g2p, k_k, k_a, r_k, e, et):
    b, t, _ = p_f.shape
    cm, cl = 3 * RW_W, PF_LORA_W
    col = {cm: 0, cl: PF_LORA_OFF // PF_LORA_W}
    tt = min(128, t)
    nb = tt // SUBLANES
    nt8 = t // SUBLANES

    def main_spec(c):
        return pl.BlockSpec((1, tt, c), lambda bi, i: (bi, i, col[c]))

    def prev_spec(c):
        return pl.BlockSpec((1, SUBLANES, c), lambda bi, i: (bi, jnp.maximum(i * nb - 1, 0), col[c]))

    def next_spec(c):
        return pl.BlockSpec((1, SUBLANES, c), lambda bi, i: (bi, jnp.minimum((i + 1) * nb, nt8 - 1), col[c]))

    def full(x):
        nd = x.ndim
        return pl.BlockSpec(x.shape, lambda bi, i: (0,) * nd)

    consts = (mu_main, mu_lora, w0, w2p, a0, a2p, g2p, k_k, k_a, r_k, e, et)
    out_dtypes = [BF16] * 5 + [F32, F32, BF16, BF16]
    return pl.pallas_call(
        _rw_prep_body,
        grid=(b, t // tt),
        in_specs=[main_spec(cm), prev_spec(cm), next_spec(cm), main_spec(cl), prev_spec(cl), next_spec(cl)]
        + [full(x) for x in consts],
        out_specs=[pl.BlockSpec((1, tt, RW_W), lambda bi, i: (bi, i, 0))] * 9,
        out_shape=[jax.ShapeDtypeStruct((b, t, RW_W), dt) for dt in out_dtypes],
        compiler_params=_params(("parallel", "parallel"), 56),
    )(p_f, p_f, p_f, p_f, p_f, p_f, *consts)


def _rw_scan_body(r_ref, k_ref, v_ref, kn_ref, ab_ref, lw_ref, y_ref, s_ref, *, rev):
    L = RW_CHUNK

    @pl.when(pl.program_id(1) == 0)
    def _():
        s_ref[...] = jnp.zeros_like(s_ref)

    row = lax.broadcasted_iota(jnp.int32, (L, L), 0)
    col = lax.broadcasted_iota(jnp.int32, (L, L), 1)
    if rev:
        incl = col >= row
        strict = col > row
    else:
        incl = col <= row
        strict = col < row
    tri = jnp.where(incl, 1.0, 0.0).astype(BF16)
    eye = jnp.where(row == col, 1.0, 0.0).astype(F32)
    lane = lax.broadcasted_iota(jnp.int32, (L, LANES), 1)
    head_masks = (lane < RW_HEAD, lane >= RW_HEAD)
    sr = lax.broadcasted_iota(jnp.int32, (LANES, LANES), 0)
    sc = lax.broadcasted_iota(jnp.int32, (LANES, LANES), 1)
    bd = (sr < RW_HEAD) == (sc < RW_HEAD)
    r4 = lax.broadcasted_iota(jnp.int32, (4 * L, 2 * L), 0)
    c4 = lax.broadcasted_iota(jnp.int32, (4 * L, 2 * L), 1) & (L - 1)
    t4 = r4 & (L - 1)
    own = jnp.where((r4 & L) != 0, 1, 0)
    causal4 = (c4 > t4 - own) if rev else (c4 < t4 + own)

    heads =[(hp, h) for hp in range(RW_HEADS // 2) for h in range(2)]
    sls = [slice(hp * LANES, (hp + 1) * LANES) for hp in range(RW_HEADS // 2)]

    def intra(offs):
        pre = []
        for off in offs:
            rows = slice(off, off + L)
            lw = lw_ref[0, rows, :]
            cum = _dot_sel(tri, lw)
            tot = cum[0:1, :] if rev else cum[L - 1:L, :]
            p_inv = jnp.exp(-cum)
            p_end = jnp.exp(tot - cum)
            ab = ab_ref[0, rows, :].astype(F32)
            k = k_ref[0, rows, :].astype(F32)
            pre.append(dict(at=-kn_ref[0, rows, :].astype(F32) * jnp.exp(cum - lw),
                            rt=r_ref[0, rows, :].astype(F32) * jnp.exp(cum), bt=ab * p_inv, kt=k * p_inv,
                            v=v_ref[0, rows, :].astype(F32), bt_end=ab * p_end, kt_end=k * p_end,
                            p_all=jnp.exp(tot)))
        zeros_lv = jnp.zeros((L, LANES), F32)
        nck = len(offs)
        at_h = [[] for _ in range(nck)]
        blks = [[] for _ in range(nck)]
        for ci, c in enumerate(pre):
            for hp in range(RW_HEADS // 2):
                sl = sls[hp]
                rows4 = []
                for h in range(2):
                    at_m = jnp.where(head_masks[h], c["at"][:, sl], 0.0)
                    at_h[ci].append(at_m)
                    rows4 += [at_m, jnp.where(head_masks[h], c["rt"][:, sl], 0.0)]
                blk = _dotp(jnp.concatenate(rows4, axis=0), jnp.concatenate([c["bt"][:, sl], c["kt"][:, sl]], axis=0),
                            NT, RW_PASSES_A)
                blks[ci].append(jnp.where(causal4, blk, 0.0))
        a_ab = [[blks[ci][hp][2 * h * L:(2 * h + 1) * L, :L] for hp, h in heads] for ci in range(nck)]
        a_rb = [[blks[ci][hp][(2 * h + 1) * L:(2 * h + 2) * L, :L] for hp, h in heads] for ci in range(nck)]
        av_pair = [[_dotp(blks[ci][hp], jnp.concatenate([zeros_lv, pre[ci]["v"][:, sls[hp]]], axis=0), NN, RW_PASSES_A)
                    for hp in range(RW_HEADS // 2)] for ci in range(nck)]
        av = [[av_pair[ci][hp][2 * h * L:(2 * h + 2) * L] for hp, h in heads] for ci in range(nck)]
        tm = [[eye + a for a in a_ab[ci]] for ci in range(nck)]
        pw = [[_dotp(a, a, NN, RW_PASSES_INV) for a in a_ab[ci]] for ci in range(nck)]
        for _ in range(4):
            st = [[_dotp(jnp.concatenate([t_, p], axis=0), p, NN, RW_PASSES_INV) for t_, p in zip(tm[ci], pw[ci])]
                  for ci in range(nck)]
            tm = [[t_ + s_[:L] for t_, s_ in zip(tm[ci], st[ci])] for ci in range(nck)]
            pw = [[s_[L:] for s_ in st[ci]] for ci in range(nck)]
        tm = [[t_ + _dotp(t_, p, NN, RW_PASSES_INV) for t_, p in zip(tm[ci], pw[ci])] for ci in range(nck)]
        out = []
        for ci, c in enumerate(pre):
            tx = [_dotp(tm[ci][i], jnp.concatenate([at_h[ci][i], jnp.where(head_masks[h], av[ci][i][:L], 0.0)], axis=1),
                        NN, RW_PASSES_A) for i, (hp, h) in enumerate(heads)]
            out.append(dict(tx=tx, av=av[ci], a_rb=a_rb[ci], rt=c["rt"], v=c["v"], bt_end=c["bt_end"],
                            kt_end=c["kt_end"], p_all=c["p_all"]))
        return out

    def sequential(c, off, states):
        tx, av, a_rb = c["tx"], c["av"], c["a_rb"]
        out = []
        for hp in range(RW_HEADS // 2):
            sl = sls[hp]
            i0, i1 = 2 * hp, 2 * hp + 1
            w_pair = tx[i0][:, :LANES] + tx[i1][:, :LANES]
            u0_pair = tx[i0][:, LANES:] + tx[i1][:, LANES:]
            y0_pair = jnp.where(head_masks[0], av[i0][L:], av[i1][L:])
            s = states[hp]
            ws = _dotp(jnp.concatenate([w_pair, c["rt"][:, sl]], axis=0), s, NT, RW_PASSES_S)
            u = ws[:L] + u0_pair
            y = ws[L:] + y0_pair + jnp.where(head_masks[0], _dotp(a_rb[i0], u, NN, RW_PASSES_A),
                                             _dotp(a_rb[i1], u, NN, RW_PASSES_A))
            uv_t = jnp.concatenate([u, c["v"][:, sl]], axis=0).T
            upd = _dotp(uv_t, jnp.concatenate([c["bt_end"][:, sl], c["kt_end"][:, sl]], axis=0), NN, RW_PASSES_S)
            out.append(s * c["p_all"][:, sl] + jnp.where(bd, upd, 0.0))
            y_ref[0, off:off + L, sl] = y
        return out

    offs = [c * L for c in range(RW_SUB)]
    if rev:
        offs = offs[::-1]
    chunks = intra(offs)
    states = [s_ref[hp] for hp in range(RW_HEADS // 2)]
    for c, off in zip(chunks, offs):
        states = sequential(c, off, states)
    for hp in range(RW_HEADS // 2):
        s_ref[hp] = states[hp]


def _rw_scan(r, k, v, kn, ab, lw, rev):
    b, t, w = r.shape
    tb = RW_CHUNK * RW_SUB
    nc = t // tb
    if rev:
        idx = lambda bi, c: (bi, nc - 1 - c, 0)
    else:
        idx = lambda bi, c: (bi, c, 0)
    spec = pl.BlockSpec((1, tb, w), idx)
    return pl.pallas_call(
        functools.partial(_rw_scan_body, rev=rev),
        grid=(b, nc),
        in_specs=[spec] * 6,
        out_specs=spec,
        out_shape=jax.ShapeDtypeStruct((b, t, w), F32),
        scratch_shapes=[pltpu.VMEM((RW_HEADS // 2, LANES, LANES), F32)],
        compiler_params=_params(("parallel", "arbitrary"), 32),
    )(r, k, v, kn, ab, lw)


def _ml_scan_body(q_ref, k_ref, v_ref, g_ref, b_ref, h_ref, c_ref, n_ref, m_ref, *, rev, direction):
    L = ML_CHUNK

    @pl.when(pl.program_id(1) == 0)
    def _():
        c_ref[...] = jnp.zeros_like(c_ref)
        n_ref[...] = jnp.zeros_like(n_ref)
        m_ref[...] = jnp.zeros_like(m_ref)

    row = lax.broadcasted_iota(jnp.int32, (L, L), 0)
    col = lax.broadcasted_iota(jnp.int32, (L, L), 1)
    incl = (col >= row) if rev else (col <= row)
    tri = jnp.where(incl, 1.0, 0.0).astype(BF16)
    lane = lax.broadcasted_iota(jnp.int32, (L, LANES), 1)
    sub = lax.broadcasted_iota(jnp.int32, (LANES, L), 0)

    g = g_ref[0] + b_ref[...]
    ls = jnp.minimum(g, 0.0) - jnp.log(1.0 + jnp.exp(-jnp.abs(g)))
    bc = _dot_sel(tri, ls)
    g_t = g.T
    bc_t = bc.T
    scale = ML_QK ** -0.5
    for h in range(ML_HEADS):
        ci = direction * ML_HEADS + h
        cf = 2 * ML_HEADS + direction * ML_HEADS + h
        b_col = jnp.sum(jnp.where(lane == cf, bc, 0.0), axis=1, keepdims=True)
        i_col = jnp.sum(jnp.where(lane == ci, g, 0.0), axis=1, keepdims=True)
        b_row = jnp.sum(jnp.where(sub == cf, bc_t, 0.0), axis=0, keepdims=True)
        i_row = jnp.sum(jnp.where(sub == ci, g_t, 0.0), axis=0, keepdims=True)
        m_prev = jnp.max(m_ref[h:h + 1, :], axis=1, keepdims=True)
        qs = q_ref[0, :, h * ML_QK:(h + 1) * ML_QK].astype(F32) * scale
        kc = k_ref[0, :, h * ML_QK:(h + 1) * ML_QK].astype(F32)
        vc = v_ref[0, :, h * ML_V:(h + 1) * ML_V].astype(F32)
        dmat = jnp.where(incl, b_col - b_row + i_row, -jnp.inf)
        inter = b_col + m_prev
        m_t = jnp.maximum(inter, jnp.max(dmat, axis=1, keepdims=True))
        s = _dot1(qs, kc, NT) * jnp.exp(dmat - m_t)
        e_inter = jnp.exp(inter - m_t)
        num = _dot1(s, vc) + e_inter * _dot1(qs, c_ref[h])
        den = jnp.sum(s, axis=1, keepdims=True) + e_inter * jnp.sum(qs * n_ref[h:h + 1, :], axis=1, keepdims=True)
        h_ref[0, :, h * ML_V:(h + 1) * ML_V] = num / jnp.maximum(jnp.abs(den), jnp.exp(-m_t))
        b_last = jnp.min(b_row, axis=1, keepdims=True)
        g_row = b_last - b_row + i_row
        g_col = b_last - b_col + i_col
        m_new = jnp.maximum(b_last + m_prev, jnp.max(g_row, axis=1, keepdims=True))
        w_col = jnp.exp(g_col - m_new)
        dec = jnp.exp(b_last + m_prev - m_new)
        c_ref[h] = dec * c_ref[h] + _dot1(kc.T, vc * w_col)
        n_ref[h:h + 1, :] = dec * n_ref[h:h + 1, :] + jnp.sum(kc * w_col, axis=0, keepdims=True)
        m_ref[h:h + 1, :] = jnp.broadcast_to(m_new, (1, LANES))


def _ml_scan(p_b, p_f, bias, rev):
    b, t, _ = p_b.shape
    L = ML_CHUNK
    nc = t // L
    tix = (lambda c: nc - 1 - c) if rev else (lambda c: c)
    return pl.pallas_call(
        functools.partial(_ml_scan_body, rev=rev, direction=1 if rev else 0),
        grid=(b, nc),
        in_specs=[pl.BlockSpec((1, L, ML_QKW), lambda bi, c: (bi, tix(c), 0)),
                  pl.BlockSpec((1, L, ML_QKW), lambda bi, c: (bi, tix(c), 1)),
                  pl.BlockSpec((1, L, ML_VW), lambda bi, c: (bi, tix(c), 1)),
                  pl.BlockSpec((1, L, LANES), lambda bi, c: (bi, tix(c), PF_MLG_OFF // LANES)),
                  pl.BlockSpec((1, LANES), lambda bi, c: (0, 0))],
        out_specs=pl.BlockSpec((1, L, ML_VW), lambda bi, c: (bi, tix(c), 0)),
        out_shape=jax.ShapeDtypeStruct((b, t, ML_VW), F32),
        scratch_shapes=[pltpu.VMEM((ML_HEADS, ML_QK, ML_V), F32),
                        pltpu.VMEM((SUBLANES, LANES), F32),
                        pltpu.VMEM((SUBLANES, LANES), F32)],
        compiler_params=_params(("parallel", "arbitrary"), 32),
    )(p_b, p_b, p_b, p_f, bias)


def _post_body(yf_ref, yb_ref, bo_ref, g_ref, lnw_ref, lnb_ref, e_ref, et_ref,
               hf_ref, hb_ref, o_ref, nw_ref, ya_ref, yb_o_ref):
    e = e_ref[...]
    et = et_ref[...]
    y = yf_ref[...] + yb_ref[...]
    inv = 1.0 / RW_HEAD
    mu = _dot1(_dot1(y, e), et) * inv
    yc = y - mu
    var = _dot1(_dot1(yc * yc, e), et) * inv
    yn = yc * lax.rsqrt(var + RW_GN_EPS) * lnw_ref[...] + lnb_ref[...]
    ya_ref[...] = ((yn + bo_ref[...].astype(F32)) * g_ref[...].astype(F32)).astype(ya_ref.dtype)

    hsum = hf_ref[...] + hb_ref[...]
    parts = []
    for h in range(ML_HEADS):
        hh = hsum[:, h * ML_V:(h + 1) * ML_V]
        parts.append(hh * lax.rsqrt(jnp.mean(hh * hh, axis=-1, keepdims=True) + ML_NORM_EPS))
    hn = jnp.concatenate(parts, axis=1) * nw_ref[...]
    yb_o_ref[...] = (hn * _sigmoid(o_ref[...].astype(F32))).astype(yb_o_ref.dtype)


def _post(yf, yb, bonus, g, ln_w, ln_b, e, et, hf, hb, vo, norm_w):
    n = yf.shape[0]
    tm = min(256, n)
    tok = lambda c: pl.BlockSpec((tm, c), lambda i: (i, 0))
    const = lambda x: pl.BlockSpec(x.shape, lambda i: (0, 0))
    return pl.pallas_call(
        _post_body,
        grid=(n // tm,),
        in_specs=[tok(RW_W), tok(RW_W), tok(RW_W), tok(RW_W), const(ln_w), const(ln_b), const(e), const(et),
                  tok(ML_VW), tok(ML_VW), pl.BlockSpec((tm, ML_VW), lambda i: (i, (2 * ML_QKW + ML_VW) // ML_VW)),
                  const(norm_w)],
        out_specs=[tok(RW_W), tok(ML_VW)],
        out_shape=[jax.ShapeDtypeStruct((n, RW_W), BF16), jax.ShapeDtypeStruct((n, ML_VW), BF16)],
        compiler_params=_params(("parallel",), 40),
    )(yf, yb, bonus, g, ln_w, ln_b, e, et, hf, hb, vo, norm_w)


def _merge_body(ya_ref, yb_ref, pa_ref, pb_ref, ga_ref, gb_ref, ba_ref, bb_ref, o_ref):
    pa = _mm(ya_ref[...], pa_ref[...])
    pb = _mm(yb_ref[...], pb_ref[...])
    ga = ga_ref[...].astype(F32) + ba_ref[...]
    gb = gb_ref[...].astype(F32) + bb_ref[...]
    o_ref[...] = (_sigmoid(ga) * pa + _sigmoid(gb) * pb).astype(o_ref.dtype)


def _merge(ya, yb, p_a, p_b, proj_b, b_gate):
    n = ya.shape[0]
    tm = min(1024, n)
    tn = 1024
    nj = D_MODEL // tn
    g0 = PB_GATE_OFF // tn
    return pl.pallas_call(
        _merge_body,
        grid=(n // tm, nj),
        in_specs=[pl.BlockSpec((tm, RW_W), lambda i, j: (i, 0)), pl.BlockSpec((tm, ML_VW), lambda i, j: (i, 0)),
                  pl.BlockSpec((RW_W, tn), lambda i, j: (0, j)), pl.BlockSpec((ML_VW, tn), lambda i, j: (0, j)),
                  pl.BlockSpec((tm, tn), lambda i, j: (i, g0 + j)), pl.BlockSpec((tm, tn), lambda i, j: (i, g0 + nj + j)),
                  pl.BlockSpec((1, tn), lambda i, j: (0, j)), pl.BlockSpec((1, tn), lambda i, j: (0, j + nj))],
        out_specs=pl.BlockSpec((tm, tn), lambda i, j: (i, j)),
        out_shape=jax.ShapeDtypeStruct((n, D_MODEL), BF16),
        compiler_params=_params(("parallel", "parallel"), 40),
    )(ya, yb, p_a, p_b, proj_b, proj_b, b_gate, b_gate)


def _resid_mm_body(a_ref, w_ref, x_ref, o_ref):
    o_ref[...] = x_ref[...] + _mm(a_ref[...], w_ref[...])


def _resid_mm(a, w, x):
    n, kdim = a.shape
    c = w.shape[1]
    tm = min(1024, n)
    tn = 1024
    return pl.pallas_call(
        _resid_mm_body,
        grid=(n // tm, c // tn),
        in_specs=[pl.BlockSpec((tm, kdim), lambda i, j: (i, 0)), pl.BlockSpec((kdim, tn), lambda i, j: (0, j)),
                  pl.BlockSpec((tm, tn), lambda i, j: (i, j))],
        out_specs=pl.BlockSpec((tm, tn), lambda i, j: (i, j)),
        out_shape=jax.ShapeDtypeStruct((n, c), F32),
        compiler_params=_params(("parallel", "parallel"), 48),
    )(a, w, x)


def _topk_rows(s, k, payload=None):
    nrow, t = s.shape
    rid = lax.broadcasted_iota(jnp.int32, (nrow, t), 0).astype(F32)
    kid = lax.broadcasted_iota(jnp.int32, (k, t), 0)
    vals = jnp.zeros((k, t), F32)
    sel = jnp.zeros((k, t), F32)
    for j in range(k):
        m = jnp.max(s, axis=0, keepdims=True)
        pos = jnp.min(jnp.where(s == m, rid, float(nrow)), axis=0, keepdims=True)
        hit = rid == pos
        if payload is None:
            picked = pos
        else:
            picked = jnp.max(jnp.where(hit, payload, -1.0), axis=0, keepdims=True)
        vals = jnp.where(kid == j, m, vals)
        sel = jnp.where(kid == j, picked, sel)
        s = jnp.where(hit, -jnp.inf, s)
    return vals, sel


def _router_body(q_ref, keys_ref, a_ref, b_ref, gw_ref):
    K = PEER_TOPK
    a_parts, b_parts, w_parts = [], [], []
    for h in range(PEER_HEADS):
        sv, si = [], []
        for p in range(2):
            c0 = (h * 2 + p) * PEER_HALF
            st = _dot3(keys_ref[h * 2 + p], q_ref[:, c0:c0 + PEER_HALF], NT)
            vals, idx = _topk_rows(st, K)
            sv.append(vals)
            si.append(idx)
        jid = lax.broadcasted_iota(jnp.int32, (SUBLANES, sv[0].shape[1]), 0)
        c_parts = [sv[0][0:1, :] + sv[1]]
        i_parts = [si[0][0:1, :] * float(PEER_NKEYS) + si[1]]
        for i in range(1, SUBLANES):
            keep = jid < K // (i + 1)
            c_parts.append(jnp.where(keep, sv[0][i:i + 1, :] + sv[1][0:SUBLANES, :], -jnp.inf))
            i_parts.append(si[0][i:i + 1, :] * float(PEER_NKEYS) + si[1][0:SUBLANES, :])
        c_parts.append(sv[0][SUBLANES:K, :] + sv[1][0:1, :])
        i_parts.append(si[0][SUBLANES:K, :] * float(PEER_NKEYS) + si[1][0:1, :])
        cand = jnp.concatenate(c_parts, axis=0)
        cidx = jnp.concatenate(i_parts, axis=0)
        best, eidx = _topk_rows(cand, K, payload=cidx)
        ex = jnp.exp(best - best[0:1, :])
        w_parts.append(ex / jnp.sum(ex, axis=0, keepdims=True))
        hi = jnp.floor(eidx * (1.0 / PEER_NKEYS))
        a_parts.append(hi)
        b_parts.append(eidx - hi * float(PEER_NKEYS))
    a_ref[...] = jnp.concatenate(a_parts, axis=0).T
    b_ref[...] = jnp.concatenate(b_parts, axis=0).T
    gw_ref[...] = jnp.concatenate(w_parts, axis=0).T


def _router(q, keys):
    n = q.shape[0]
    tq = min(256, n)
    tok = pl.BlockSpec((tq, PEER_HK), lambda i: (i, 0))
    sds = jax.ShapeDtypeStruct((n, PEER_HK), F32)
    return pl.pallas_call(
        _router_body,
        grid=(n // tq,),
        in_specs=[pl.BlockSpec((tq, q.shape[1]), lambda i: (i, 0)),
                  pl.BlockSpec(keys.shape, lambda i: (0, 0, 0))],
        out_specs=[tok, tok, tok],
        out_shape=[sds, sds, sds],
        compiler_params=_params(("parallel",), 32),
    )(q, keys)


PEER_CI = 4
PEER_CE = PEER_CI * PEER_NKEYS


PEER_SPLIT = 1
PEER_KEYS_PER_CALL = PEER_NKEYS // PEER_SPLIT
HI16 = 0xFFFF0000


def _peer_body(x_ref, h_ref, a_ref, b_ref, w_ref, ut_ref, v_ref, gf_ref, o_ref, g2_ref, z_ref, *,
               pitch, key0, final_norm):
    j = pl.program_id(1)
    last = pl.num_programs(1) - 1
    t = x_ref.shape[0]
    half = t // 2
    cur = j % 2

    def produce():
        z_ref[cur] = _mm(h_ref[...], ut_ref[...])

    def consume():
        z = z_ref[1 - cur]
        parts = []
        for ii in range(PEER_CI):
            start = pl.multiple_of(((j - 1) * PEER_CI + ii) * pitch, SUBLANES)
            words = g2_ref[pl.ds(start, half), :]
            g_first = lax.bitcast_convert_type(words << 16, F32)
            g_second = lax.bitcast_convert_type(words & jnp.uint32(HI16), F32)
            zz = z[:, ii * PEER_NKEYS:(ii + 1) * PEER_NKEYS]
            act = 0.5 * zz * (1.0 + lax.erf(zz * (2.0 ** -0.5)))
            parts.append(jnp.concatenate([act[:half] * g_first, act[half:] * g_second], axis=0).astype(BF16))
        o_ref[...] += _mm(jnp.concatenate(parts, axis=1), v_ref[...])

    @pl.when(j == 0)
    def _():
        o_ref[...] = x_ref[...]
        kid = lax.broadcasted_iota(jnp.int32, (PEER_KEYS_PER_CALL, PEER_HK), 0).astype(F32) + float(key0)
        kid_b = lax.broadcasted_iota(jnp.int32, (PEER_NKEYS, PEER_HK), 0).astype(F32)

        def g_bits(ti):
            arow = a_ref[pl.ds(ti, 1), :]
            brow = b_ref[pl.ds(ti, 1), :]
            wrow = w_ref[pl.ds(ti, 1), :]
            lhs = jnp.where(kid == arow, wrow, 0.0).astype(BF16)
            rhs = jnp.where(kid_b == brow, 1.0, 0.0).astype(BF16)
            g_t = _mm(lhs, rhs, NT)
            return lax.bitcast_convert_type(g_t.astype(BF16).astype(F32), jnp.uint32)

        def per_pair(ti, carry):
            words = (g_bits(ti) >> 16) | g_bits(ti + half)
            g2_ref[pl.ds(ti, PEER_KEYS_PER_CALL, stride=pitch), :] = words
            return carry

        lax.fori_loop(0, half, per_pair, 0, unroll=8)
        produce()

    @pl.when(jnp.logical_and(j > 0, j < last))
    def _():
        produce()
        consume()

    @pl.when(j == last)
    def _():
        consume()

    if final_norm:
        @pl.when(j == last)
        def _():
            y = o_ref[...]
            o_ref[...] = y * lax.rsqrt(jnp.mean(y * y, axis=-1, keepdims=True) + RMS_EPS) * gf_ref[...]


def _peer(x, h, a_idx, b_idx, gw, u_t, v_tab, g_final, final_norm):
    n, d = x.shape
    t = min(512, n)
    pitch = t // 2 + SUBLANES
    nch = PEER_N // PEER_CE // PEER_SPLIT
    once = pl.Buffered(1)
    tok = lambda c: pl.BlockSpec((t, c), lambda i, j: (i, 0), pipeline_mode=once)
    for part in range(PEER_SPLIT):
        c0 = part * nch
        is_last = part == PEER_SPLIT - 1
        x = pl.pallas_call(
            functools.partial(_peer_body, pitch=pitch, key0=part * PEER_KEYS_PER_CALL,
                              final_norm=final_norm and is_last),
            grid=(n // t, nch + 1),
            in_specs=[tok(d), tok(d), tok(PEER_HK), tok(PEER_HK), tok(PEER_HK),
                      pl.BlockSpec((d, PEER_CE), lambda i, j, c0=c0: (0, c0 + jnp.minimum(j, nch - 1))),
                      pl.BlockSpec((PEER_CE, d), lambda i, j, c0=c0: (c0 + jnp.maximum(j - 1, 0), 0)),
                      pl.BlockSpec((1, d), lambda i, j: (0, 0))],
            out_specs=pl.BlockSpec((t, d), lambda i, j: (i, 0)),
            out_shape=jax.ShapeDtypeStruct((n, d), F32),
            scratch_shapes=[pltpu.VMEM((PEER_KEYS_PER_CALL * pitch, LANES), jnp.uint32),
                            pltpu.VMEM((2, t, PEER_CE), F32)],
            compiler_params=_params(("parallel", "arbitrary"), 56),
        )(x, h, a_idx, b_idx, gw, u_t, v_tab, g_final)
    return x


def _pad_cols(pieces, width):
    rows = pieces[0][0].shape[0]
    out = jnp.zeros((rows, width), pieces[0][0].dtype)
    for arr, off in pieces:
        out = out.at[:, off:off + arr.shape[1]].set(arr)
    return out


def _pad_rows(x, rows):
    return jnp.zeros((rows,) + x.shape[1:], x.dtype).at[:x.shape[0]].set(x)


def _lora_layout(x):
    o = 3 * RW_W
    return _pad_cols([(x[:, o:o + 64], 0), (x[:, o + 64:o + 128], 128), (x[:, o + 128:o + 288], 256)], 512)


def _prep_layer(l, w):
    f = {}
    w_in = w["w_in"][l]
    rw = w_in[:, :RW_COLS]
    ml = w_in[:, RW_COLS:RW_COLS + ML_COLS]
    gt = w_in[:, RW_COLS + ML_COLS:]
    f["norm_mix"] = w["norm_mix"][l][None, :]
    f["w_f"] = _pad_cols([(rw[:, :3 * RW_W], 0), (_lora_layout(rw), PF_LORA_OFF),
                          (ml[:, 2 * ML_QKW + 2 * ML_VW:], PF_MLG_OFF)], PF_COLS).astype(BF16)
    f["w_b"] = jnp.concatenate([ml[:, :2 * ML_QKW + 2 * ML_VW], gt], axis=1).astype(BF16)
    mu = jnp.stack([w["rw_mu_prev"][l], w["rw_mu_next"][l]])
    f["mu_main"] = mu[:, :3 * RW_W]
    f["mu_lora"] = _lora_layout(mu)
    f["w0"] = w["rw_w0"][l]
    f["w2p"] = jnp.stack([_pad_rows(w["rw_w2"][l, 0], 128), _pad_rows(w["rw_w2"][l, 1], 128)]).astype(BF16)
    f["a0"] = w["rw_a0"][l][None, :]
    f["a2p"] = _pad_rows(w["rw_a2"][l], 128).astype(BF16)
    f["g2p"] = _pad_rows(w["rw_g2"][l], 256).astype(BF16)
    f["k_k"] = w["rw_k_k"][l][None, :]
    f["k_a"] = w["rw_k_a"][l][None, :]
    f["r_k"] = w["rw_r_k"][l].reshape(1, RW_W)
    f["ln_w"] = w["rw_ln_w"][l][None, :]
    f["ln_b"] = w["rw_ln_b"][l][None, :]
    f["ml_bias"] = _pad_cols([(w["ml_b_i"][l].reshape(1, -1), 0), (w["ml_b_f"][l].reshape(1, -1), 2 * ML_HEADS)], LANES)
    f["ml_norm_w"] = w["ml_norm_w"][l][None, :]
    f["p_a"] = w["p_a"][l].astype(BF16)
    f["p_b"] = w["p_b"][l].astype(BF16)
    f["b_gate"] = w["b_gate"][l][None, :]
    f["w_out"] = w["w_out"][l].astype(BF16)
    f["norm_ffn"] = w["norm_ffn"][l][None, :]
    f["wq"] = (w["peer_wq"][l].astype(BF16),)
    f["keys"] = w["peer_keys"][l].reshape(PEER_HEADS * 2, PEER_NKEYS, PEER_HALF)
    f["u_t"] = w["peer_u"][l].astype(BF16).T
    f["v"] = w["peer_v"][l].astype(BF16)
    return f


def _head_selectors():
    ch = jnp.arange(RW_W)[:, None] // RW_HEAD
    e = (ch == jnp.arange(LANES)[None, :]).astype(BF16)
    return e, e.T


def _layer(x, f, e, et, g_final, final_norm):
    b, t, d = x.shape
    n = b * t
    x2 = x.reshape(n, d)
    (hn,) = _norm_cast(x2, f["norm_mix"], 1)
    p_f = _matmul((hn,), (f["w_f"],), F32)
    p_b = _matmul((hn,), (f["w_b"],), BF16)
    p_f3 = p_f.reshape(b, t, -1)
    p_b3 = p_b.reshape(b, t, -1)

    r, k, v, kn, ab, lw0, lw1, gg, bonus = _rw_prep(
        p_f3, f["mu_main"], f["mu_lora"], f["w0"], f["w2p"], f["a0"], f["a2p"], f["g2p"], f["k_k"], f["k_a"],
        f["r_k"], e, et)
    y_f = _rw_scan(r, k, v, kn, ab, lw0, rev=False)
    y_b = _rw_scan(r, k, v, kn, ab, lw1, rev=True)

    h_f = _ml_scan(p_b3, p_f3, f["ml_bias"], rev=False)
    h_b = _ml_scan(p_b3, p_f3, f["ml_bias"], rev=True)

    ya, yb = _post(y_f.reshape(n, -1), y_b.reshape(n, -1), bonus.reshape(n, -1), gg.reshape(n, -1),
                   f["ln_w"], f["ln_b"], e, et, h_f.reshape(n, -1), h_b.reshape(n, -1), p_b, f["ml_norm_w"])
    merged = _merge(ya, yb, f["p_a"], f["p_b"], p_b, f["b_gate"])
    x1 = _resid_mm(merged, f["w_out"], x2)

    h_parts = _norm_cast(x1, f["norm_ffn"], 1)
    q = _matmul(h_parts, f["wq"], F32)
    a_idx, b_idx, gw = _router(q, f["keys"])
    x_out = _peer(x1, h_parts[0], a_idx, b_idx, gw, f["u_t"], f["v"], g_final, final_norm)
    return x_out.reshape(b, t, d)


def _trunk(x, layers, e, et, g_final):
    for l, f in enumerate(layers):
        x = _layer(x, f, e, et, g_final, final_norm=(l == len(layers) - 1))
    return x


def kernel(x_prompt, x_sample, norm_mix, w_in, rw_mu_prev, rw_mu_next, rw_w0, rw_w2, rw_a0, rw_a2, rw_g2, rw_k_k, rw_k_a, rw_r_k, rw_ln_w, rw_ln_b, ml_b_i, ml_b_f, ml_norm_w, p_a, p_b, b_gate, w_out, norm_ffn, peer_wq, peer_keys, peer_u, peer_v, norm_final):
    w = dict(norm_mix=norm_mix, w_in=w_in, rw_mu_prev=rw_mu_prev, rw_mu_next=rw_mu_next, rw_w0=rw_w0, rw_w2=rw_w2,
             rw_a0=rw_a0, rw_a2=rw_a2, rw_g2=rw_g2, rw_k_k=rw_k_k, rw_k_a=rw_k_a, rw_r_k=rw_r_k, rw_ln_w=rw_ln_w,
             rw_ln_b=rw_ln_b, ml_b_i=ml_b_i, ml_b_f=ml_b_f, ml_norm_w=ml_norm_w, p_a=p_a, p_b=p_b, b_gate=b_gate,
             w_out=w_out, norm_ffn=norm_ffn, peer_wq=peer_wq, peer_keys=peer_keys, peer_u=peer_u, peer_v=peer_v)
    depth = w_in.shape[0]
    layers = [_prep_layer(l, w) for l in range(depth)]
    e, et = _head_selectors()
    g_final = norm_final[None, :]
    return (_trunk(x_prompt, layers, e, et, g_final), _trunk(x_sample, layers, e, et, g_final))
```

```python
import functools
import math

import jax
import jax.numpy as jnp
from jax import lax
from jax.experimental import pallas as pl
from jax.experimental.pallas import tpu as pltpu

F32 = jnp.float32
BF16 = jnp.bfloat16

D_MODEL = 2048
RMS_EPS = 1e-6
RW_W = 1024
RW_HEAD = 64
RW_HEADS = 16
RW_DECAY_LORA = 64
RW_A_LORA = 64
RW_G_LORA = 160
RW_GN_EPS = 64e-5
RW_COLS = 3 * RW_W + RW_DECAY_LORA + RW_A_LORA + RW_G_LORA
RW_CHUNK = 64
RW_SUB = 4
RW_PASSES_A = 1
RW_PASSES_INV = 1
RW_PASSES_S = 1
ML_HEADS = 4
ML_QK = 128
ML_V = 256
ML_QKW = ML_HEADS * ML_QK
ML_VW = ML_HEADS * ML_V
ML_CHUNK = 128
ML_NORM_EPS = 1e-6
ML_COLS = 2 * ML_QKW + 2 * ML_VW + 4 * ML_HEADS
PF_LORA_OFF = 3 * RW_W
PF_LORA_W = 512
PF_MLG_OFF = PF_LORA_OFF + PF_LORA_W
PF_COLS = 4096
PB_GATE_OFF = 2 * ML_QKW + 2 * ML_VW
PB_COLS = PB_GATE_OFF + 2 * D_MODEL
PEER_HEADS = 8
PEER_NKEYS = 128
PEER_N = PEER_NKEYS * PEER_NKEYS
PEER_HALF = 128
PEER_TOPK = 16
PEER_HK = PEER_HEADS * PEER_TOPK
LANES = 128
SUBLANES = 8

NN = ((1,), (0,))
NT = ((1,), (1,))


def _mm(a, b, dims=NN):
    return lax.dot_general(a, b, (dims, ((), ())), preferred_element_type=F32)


def _split2(a):
    hi = a.astype(BF16)
    lo = (a - hi.astype(F32)).astype(BF16)
    return hi, lo


def _split3(a):
    hi = a.astype(BF16)
    r1 = a - hi.astype(F32)
    mid = r1.astype(BF16)
    lo = (r1 - mid.astype(F32)).astype(BF16)
    return hi, mid, lo


def _dot1(a, b, dims=NN):
    return _mm(a.astype(BF16), b.astype(BF16), dims)


def _dot3(a, b, dims=NN):
    ah, al = _split2(a)
    bh, bl = _split2(b)
    return _mm(ah, bh, dims) + (_mm(ah, bl, dims) + _mm(al, bh, dims))


def _dotp(a, b, dims, passes):
    return _dot3(a, b, dims) if passes == 3 else _dot1(a, b, dims)


def _dot_sel(sel_bf16, x, dims=NN):
    h, m, l = _split3(x)
    return _mm(sel_bf16, h, dims) + (_mm(sel_bf16, m, dims) + _mm(sel_bf16, l, dims))


def _dot_xsel(x, sel_bf16, dims=NN):
    h, m, l = _split3(x)
    return _mm(h, sel_bf16, dims) + (_mm(m, sel_bf16, dims) + _mm(l, sel_bf16, dims))


def _params(sem, vmem_mb):
    return pltpu.CompilerParams(dimension_semantics=sem, vmem_limit_bytes=vmem_mb << 20)


def _sigmoid(x):
    return 1.0 / (1.0 + jnp.exp(-x))


def _norm_cast_body(x_ref, g_ref, *o_refs):
    xf = x_ref[...]
    h = xf * lax.rsqrt(jnp.mean(xf * xf, axis=-1, keepdims=True) + RMS_EPS) * g_ref[...]
    hi = h.astype(BF16)
    o_refs[0][...] = hi
    if len(o_refs) == 2:
        o_refs[1][...] = (h - hi.astype(F32)).astype(BF16)


def _norm_cast(x, g, n_parts):
    n, d = x.shape
    tm = min(512, n)
    spec = pl.BlockSpec((tm, d), lambda i: (i, 0))
    return pl.pallas_call(
        _norm_cast_body,
        grid=(n // tm,),
        in_specs=[spec, pl.BlockSpec((1, d), lambda i: (0, 0))],
        out_specs=[spec] * n_parts,
        out_shape=[jax.ShapeDtypeStruct((n, d), BF16)] * n_parts,
        compiler_params=_params(("parallel",), 32),
    )(x, g)


def _matmul_body(*refs, n_parts):
    a_refs = refs[:n_parts]
    w_refs = refs[n_parts:2 * n_parts]
    o_ref = refs[2 * n_parts]
    acc = _mm(a_refs[0][...], w_refs[0][...])
    if n_parts == 2:
        acc = acc + (_mm(a_refs[0][...], w_refs[1][...]) + _mm(a_refs[1][...], w_refs[0][...]))
    o_ref[...] = acc.astype(o_ref.dtype)


def _matmul(a_parts, w_parts, out_dtype):
    n, d = a_parts[0].shape
    c = w_parts[0].shape[1]
    n_parts = len(a_parts)
    tm = min(1024, n)
    tn = min(1024 // n_parts, c)
    return pl.pallas_call(
        functools.partial(_matmul_body, n_parts=n_parts),
        grid=(n // tm, c // tn),
        in_specs=[pl.BlockSpec((tm, d), lambda i, j: (i, 0))] * n_parts
        + [pl.BlockSpec((d, tn), lambda i, j: (0, j))] * n_parts,
        out_specs=pl.BlockSpec((tm, tn), lambda i, j: (i, j)),
        out_shape=jax.ShapeDtypeStruct((n, c), out_dtype),
        compiler_params=_params(("parallel", "parallel"), 48),
    )(*a_parts, *w_parts)


def _shift_rows(p, prev_row, next_row, mu_prev, mu_next):
    n = p.shape[0]
    rid = lax.broadcasted_iota(jnp.int32, p.shape, 0)
    prev = jnp.where(rid == 0, prev_row, pltpu.roll(p, 1, 0))
    nxt = jnp.where(rid == n - 1, next_row, pltpu.roll(p, n - 1, 0))
    return p + mu_prev * (prev - p) + mu_next * (nxt - p)


def _rw_prep_body(m_ref, mp_ref, mn_ref, l_ref, lp_ref, ln_ref, mum_ref, mul_ref, w0_ref, w2_ref,
                  a0_ref, a2_ref, g2_ref, kk_ref, ka_ref, rk_ref, e_ref, et_ref,
                  r_o, k_o, v_o, kn_o, ab_o, lw0_o, lw1_o, g_o, bo_o):
    i = pl.program_id(1)
    first = i == 0
    last = i == pl.num_programs(1) - 1
    zero_m = jnp.zeros((1, m_ref.shape[2]), F32)
    zero_l = jnp.zeros((1, l_ref.shape[2]), F32)
    pm = _shift_rows(m_ref[0],
                     jnp.where(first, zero_m, mp_ref[0, SUBLANES - 1:SUBLANES, :]),
                     jnp.where(last, zero_m, mn_ref[0, 0:1, :]),
                     mum_ref[0:1, :], mum_ref[1:2, :])
    plo = _shift_rows(l_ref[0],
                      jnp.where(first, zero_l, lp_ref[0, SUBLANES - 1:SUBLANES, :]),
                      jnp.where(last, zero_l, ln_ref[0, 0:1, :]),
                      mul_ref[0:1, :], mul_ref[1:2, :])
    r = pm[:, 0:RW_W]
    k = pm[:, RW_W:2 * RW_W]
    v = pm[:, 2 * RW_W:3 * RW_W]
    wd = plo[:, 0:128]
    ad = plo[:, 128:256]
    gd = plo[:, 256:512]

    a = _sigmoid(a0_ref[...] + _dot1(ad, a2_ref[...]))
    g = _dot1(_sigmoid(gd), g2_ref[...])
    e = e_ref[...]
    et = et_ref[...]
    kk = k * kk_ref[...]
    ss = _dot1(_dot1(kk * kk, e), et)
    kn = kk / jnp.maximum(jnp.sqrt(ss), 1e-12)
    kmod = k * (1.0 + (a - 1.0) * ka_ref[...])
    wl = jnp.tanh(wd)
    c = math.exp(-0.5)
    lw0 = -_sigmoid(w0_ref[0:1, :] + _dot1(wl, w2_ref[0])) * c
    lw1 = -_sigmoid(w0_ref[1:2, :] + _dot1(wl, w2_ref[1])) * c
    rks = _dot1(_dot1(r * kmod * rk_ref[...], e), et)
    r_o[0] = r.astype(r_o.dtype)
    k_o[0] = kmod.astype(k_o.dtype)
    v_o[0] = v.astype(v_o.dtype)
    kn_o[0] = kn.astype(kn_o.dtype)
    ab_o[0] = (kn * a).astype(ab_o.dtype)
    lw0_o[0] = lw0
    lw1_o[0] = lw1
    g_o[0] = g.astype(g_o.dtype)
    bo_o[0] = (rks * v).astype(bo_o.dtype)


def _rw_prep(p_f, mu_main, mu_lora, w0, w2p, a0, a2p, g2p, k_k, k_a, r_k, e, et):
    b, t, _ = p_f.shape
    cm, cl = 3 * RW_W, PF_LORA_W
    col = {cm: 0, cl: PF_LORA_OFF // PF_LORA_W}
    tt = min(128, t)
    nb = tt // SUBLANES
    nt8 = t // SUBLANES

    def main_spec(c):
        return pl.BlockSpec((1, tt, c), lambda bi, i: (bi, i, col[c]))

    def prev_spec(c):
        return pl.BlockSpec((1, SUBLANES, c), lambda bi, i: (bi, jnp.maximum(i * nb - 1, 0), col[c]))

    def next_spec(c):
        return pl.BlockSpec((1, SUBLANES, c), lambda bi, i: (bi, jnp.minimum((i + 1) * nb, nt8 - 1), col[c]))

    def full(x):
        nd = x.ndim
        return pl.BlockSpec(x.shape, lambda bi, i: (0,) * nd)

    consts = (mu_main, mu_lora, w0, w2p, a0, a2p, g2p, k_k, k_a, r_k, e, et)
    out_dtypes = [BF16] * 5 + [F32, F32, BF16, BF16]
    return pl.pallas_call(
        _rw_prep_body,
        grid=(b, t // tt),
        in_specs=[main_spec(cm), prev_spec(cm), next_spec(cm), main_spec(cl), prev_spec(cl), next_spec(cl)]
        + [full(x) for x in consts],
        out_specs=[pl.BlockSpec((1, tt, RW_W), lambda bi, i: (bi, i, 0))] * 9,
        out_shape=[jax.ShapeDtypeStruct((b, t, RW_W), dt) for dt in out_dtypes],
        compiler_params=_params(("parallel", "parallel"), 56),
    )(p_f, p_f, p_f, p_f, p_f, p_f, *consts)


def _rw_scan_body(r_ref, k_ref, v_ref, kn_ref, ab_ref, lw_ref, y_ref, s_ref, *, rev):
    L = RW_CHUNK

    @pl.when(pl.program_id(1) == 0)
    def _():
        s_ref[...] = jnp.zeros_like(s_ref)

    row = lax.broadcasted_iota(jnp.int32, (L, L), 0)
    col = lax.broadcasted_iota(jnp.int32, (L, L), 1)
    if rev:
        incl = col >= row
        strict = col > row
    else:
        incl = col <= row
        strict = col < row
    tri = jnp.where(incl, 1.0, 0.0).astype(BF16)
    eye = jnp.where(row == col, 1.0, 0.0).astype(F32)
    lane = lax.broadcasted_iota(jnp.int32, (L, LANES), 1)
    head_masks = (lane < RW_HEAD, lane >= RW_HEAD)
    sr = lax.broadcasted_iota(jnp.int32, (LANES, LANES), 0)
    sc = lax.broadcasted_iota(jnp.int32, (LANES, LANES), 1)
    bd = (sr < RW_HEAD) == (sc < RW_HEAD)
    r4 = lax.broadcasted_iota(jnp.int32, (4 * L, 2 * L), 0)
    c4 = lax.broadcasted_iota(jnp.int32, (4 * L, 2 * L), 1) & (L - 1)
    t4 = r4 & (L - 1)
    own = jnp.where((r4 & L) != 0, 1, 0)
    causal4 = (c4 > t4 - own) if rev else (c4 < t4 + own)

    heads =[(hp, h) for hp in range(RW_HEADS // 2) for h in range(2)]
    sls = [slice(hp * LANES, (hp + 1) * LANES) for hp in range(RW_HEADS // 2)]

    def intra(offs):
        pre = []
        for off in offs:
            rows = slice(off, off + L)
            lw = lw_ref[0, rows, :]
            cum = _dot_sel(tri, lw)
            tot = cum[0:1, :] if rev else cum[L - 1:L, :]
            p_inv = jnp.exp(-cum)
            p_end = jnp.exp(tot - cum)
            ab = ab_ref[0, rows, :].astype(F32)
            k = k_ref[0, rows, :].astype(F32)
            pre.append(dict(at=-kn_ref[0, rows, :].astype(F32) * jnp.exp(cum - lw),
                            rt=r_ref[0, rows, :].astype(F32) * jnp.exp(cum), bt=ab * p_inv, kt=k * p_inv,
                            v=v_ref[0, rows, :].astype(F32), bt_end=ab * p_end, kt_end=k * p_end,
                            p_all=jnp.exp(tot)))
        zeros_lv = jnp.zeros((L, LANES), F32)
        nck = len(offs)
        at_h = [[] for _ in range(nck)]
        blks = [[] for _ in range(nck)]
        for ci, c in enumerate(pre):
            for hp in range(RW_HEADS // 2):
                sl = sls[hp]
                rows4 = []
                for h in range(2):
                    at_m = jnp.where(head_masks[h], c["at"][:, sl], 0.0)
                    at_h[ci].append(at_m)
                    rows4 += [at_m, jnp.where(head_masks[h], c["rt"][:, sl], 0.0)]
                blk = _dotp(jnp.concatenate(rows4, axis=0), jnp.concatenate([c["bt"][:, sl], c["kt"][:, sl]], axis=0),
                            NT, RW_PASSES_A)
                blks[ci].append(jnp.where(causal4, blk, 0.0))
        a_ab = [[blks[ci][hp][2 * h * L:(2 * h + 1) * L, :L] for hp, h in heads] for ci in range(nck)]
        a_rb = [[blks[ci][hp][(2 * h + 1) * L:(2 * h + 2) * L, :L] for hp, h in heads] for ci in range(nck)]
        av_pair = [[_dotp(blks[ci][hp], jnp.concatenate([zeros_lv, pre[ci]["v"][:, sls[hp]]], axis=0), NN, RW_PASSES_A)
                    for hp in range(RW_HEADS // 2)] for ci in range(nck)]
        av = [[av_pair[ci][hp][2 * h * L:(2 * h + 2) * L] for hp, h in heads] for ci in range(nck)]
        tm = [[eye + a for a in a_ab[ci]] for ci in range(nck)]
        pw = [[_dotp(a, a, NN, RW_PASSES_INV) for a in a_ab[ci]] for ci in range(nck)]
        for _ in range(4):
            st = [[_dotp(jnp.concatenate([t_, p], axis=0), p, NN, RW_PASSES_INV) for t_, p in zip(tm[ci], pw[ci])]
                  for ci in range(nck)]
            tm = [[t_ + s_[:L] for t_, s_ in zip(tm[ci], st[ci])] for ci in range(nck)]
            pw = [[s_[L:] for s_ in st[ci]] for ci in range(nck)]
        tm = [[t_ + _dotp(t_, p, NN, RW_PASSES_INV) for t_, p in zip(tm[ci], pw[ci])] for ci in range(nck)]
        out = []
        for ci, c in enumerate(pre):
            tx = [_dotp(tm[ci][i], jnp.concatenate([at_h[ci][i], jnp.where(head_masks[h], av[ci][i][:L], 0.0)], axis=1),
                        NN, RW_PASSES_A) for i, (hp, h) in enumerate(heads)]
            out.append(dict(tx=tx, av=av[ci], a_rb=a_rb[ci], rt=c["rt"], v=c["v"], bt_end=c["bt_end"],
                            kt_end=c["kt_end"], p_all=c["p_all"]))
        return out

    def sequential(c, off, states):
        tx, av, a_rb = c["tx"], c["av"], c["a_rb"]
        out = []
        for hp in range(RW_HEADS // 2):
            sl = sls[hp]
            i0, i1 = 2 * hp, 2 * hp + 1
            w_pair = tx[i0][:, :LANES] + tx[i1][:, :LANES]
            u0_pair = tx[i0][:, LANES:] + tx[i1][:, LANES:]
            y0_pair = jnp.where(head_masks[0], av[i0][L:], av[i1][L:])
            s = states[hp]
            ws = _dotp(jnp.concatenate([w_pair, c["rt"][:, sl]], axis=0), s, NT, RW_PASSES_S)
            u = ws[:L] + u0_pair
            y = ws[L:] + y0_pair + jnp.where(head_masks[0], _dotp(a_rb[i0], u, NN, RW_PASSES_A),
                                             _dotp(a_rb[i1], u, NN, RW_PASSES_A))
            uv_t = jnp.concatenate([u, c["v"][:, sl]], axis=0).T
            upd = _dotp(uv_t, jnp.concatenate([c["bt_end"][:, sl], c["kt_end"][:, sl]], axis=0), NN, RW_PASSES_S)
            out.append(s * c["p_all"][:, sl] + jnp.where(bd, upd, 0.0))
            y_ref[0, off:off + L, sl] = y
        return out

    offs = [c * L for c in range(RW_SUB)]
    if rev:
        offs = offs[::-1]
    chunks = intra(offs)
    states = [s_ref[hp] for hp in range(RW_HEADS // 2)]
    for c, off in zip(chunks, offs):
        states = sequential(c, off, states)
    for hp in range(RW_HEADS // 2):
        s_ref[hp] = states[hp]


def _rw_scan(r, k, v, kn, ab, lw, rev):
    b, t, w = r.shape
    tb = RW_CHUNK * RW_SUB
    nc = t // tb
    if rev:
        idx = lambda bi, c: (bi, nc - 1 - c, 0)
    else:
        idx = lambda bi, c: (bi, c, 0)
    spec = pl.BlockSpec((1, tb, w), idx)
    return pl.pallas_call(
        functools.partial(_rw_scan_body, rev=rev),
        grid=(b, nc),
        in_specs=[spec] * 6,
        out_specs=spec,
        out_shape=jax.ShapeDtypeStruct((b, t, w), F32),
        scratch_shapes=[pltpu.VMEM((RW_HEADS // 2, LANES, LANES), F32)],
        compiler_params=_params(("parallel", "arbitrary"), 32),
    )(r, k, v, kn, ab, lw)


def _ml_scan_body(q_ref, k_ref, v_ref, g_ref, b_ref, h_ref, c_ref, n_ref, m_ref, *, rev, direction):
    L = ML_CHUNK

    @pl.when(pl.program_id(1) == 0)
    def _():
        c_ref[...] = jnp.zeros_like(c_ref)
        n_ref[...] = jnp.zeros_like(n_ref)
        m_ref[...] = jnp.zeros_like(m_ref)

    row = lax.broadcasted_iota(jnp.int32, (L, L), 0)
    col = lax.broadcasted_iota(jnp.int32, (L, L), 1)
    incl = (col >= row) if rev else (col <= row)
    tri = jnp.where(incl, 1.0, 0.0).astype(BF16)
    lane = lax.broadcasted_iota(jnp.int32, (L, LANES), 1)
    sub = lax.broadcasted_iota(jnp.int32, (LANES, L), 0)

    g = g_ref[0] + b_ref[...]
    ls = jnp.minimum(g, 0.0) - jnp.log(1.0 + jnp.exp(-jnp.abs(g)))
    bc = _dot_sel(tri, ls)
    g_t = g.T
    bc_t = bc.T
    scale = ML_QK ** -0.5
    hs = range(ML_HEADS)
    ci = [direction * ML_HEADS + h for h in hs]
    cf = [2 * ML_HEADS + direction * ML_HEADS + h for h in hs]
    b_col = [jnp.sum(jnp.where(lane == cf[h], bc, 0.0), axis=1, keepdims=True) for h in hs]
    i_col = [jnp.sum(jnp.where(lane == ci[h], g, 0.0), axis=1, keepdims=True) for h in hs]
    b_row = [jnp.sum(jnp.where(sub == cf[h], bc_t, 0.0), axis=0, keepdims=True) for h in hs]
    i_row = [jnp.sum(jnp.where(sub == ci[h], g_t, 0.0), axis=0, keepdims=True) for h in hs]
    m_prev = [jnp.max(m_ref[h:h + 1, :], axis=1, keepdims=True) for h in hs]
    qs = [q_ref[0, :, h * ML_QK:(h + 1) * ML_QK].astype(F32) * scale for h in hs]
    kc = [k_ref[0, :, h * ML_QK:(h + 1) * ML_QK].astype(F32) for h in hs]
    vc = [v_ref[0, :, h * ML_V:(h + 1) * ML_V].astype(F32) for h in hs]
    qk = [_dot1(qs[h], kc[h], NT) for h in hs]
    qc = [_dot1(qs[h], c_ref[h]) for h in hs]
    dmat = [jnp.where(incl, b_col[h] - b_row[h] + i_row[h], -jnp.inf) for h in hs]
    inter = [b_col[h] + m_prev[h] for h in hs]
    m_t = [jnp.maximum(inter[h], jnp.max(dmat[h], axis=1, keepdims=True)) for h in hs]
    s = [qk[h] * jnp.exp(dmat[h] - m_t[h]) for h in hs]
    e_inter = [jnp.exp(inter[h] - m_t[h]) for h in hs]
    num = [_dot1(s[h], vc[h]) + e_inter[h] * qc[h] for h in hs]
    den = [jnp.sum(s[h], axis=1, keepdims=True)
           + e_inter[h] * jnp.sum(qs[h] * n_ref[h:h + 1, :], axis=1, keepdims=True) for h in hs]
    for h in hs:
        h_ref[0, :, h * ML_V:(h + 1) * ML_V] = num[h] / jnp.maximum(jnp.abs(den[h]), jnp.exp(-m_t[h]))
    b_last = [jnp.min(b_row[h], axis=1, keepdims=True) for h in hs]
    g_row = [b_last[h] - b_row[h] + i_row[h] for h in hs]
    g_col = [b_last[h] - b_col[h] + i_col[h] for h in hs]
    m_new = [jnp.maximum(b_last[h] + m_prev[h], jnp.max(g_row[h], axis=1, keepdims=True)) for h in hs]
    w_col = [jnp.exp(g_col[h] - m_new[h]) for h in hs]
    dec = [jnp.exp(b_last[h] + m_prev[h] - m_new[h]) for h in hs]
    upd = [_dot1(kc[h].T, vc[h] * w_col[h]) for h in hs]
    for h in hs:
        c_ref[h] = dec[h] * c_ref[h] + upd[h]
        n_ref[h:h + 1, :] = dec[h] * n_ref[h:h + 1, :] + jnp.sum(kc[h] * w_col[h], axis=0, keepdims=True)
        m_ref[h:h + 1, :] = jnp.broadcast_to(m_new[h], (1, LANES))


def _ml_scan(p_b, p_f, bias, rev):
    b, t, _ = p_b.shape
    L = ML_CHUNK
    nc = t // L
    tix = (lambda c: nc - 1 - c) if rev else (lambda c: c)
    return pl.pallas_call(
        functools.partial(_ml_scan_body, rev=rev, direction=1 if rev else 0),
        grid=(b, nc),
        in_specs=[pl.BlockSpec((1, L, ML_QKW), lambda bi, c: (bi, tix(c), 0)),
                  pl.BlockSpec((1, L, ML_QKW), lambda bi, c: (bi, tix(c), 1)),
                  pl.BlockSpec((1, L, ML_VW), lambda bi, c: (bi, tix(c), 1)),
                  pl.BlockSpec((1, L, LANES), lambda bi, c: (bi, tix(c), PF_MLG_OFF // LANES)),
                  pl.BlockSpec((1, LANES), lambda bi, c: (0, 0))],
        out_specs=pl.BlockSpec((1, L, ML_VW), lambda bi, c: (bi, tix(c), 0)),
        out_shape=jax.ShapeDtypeStruct((b, t, ML_VW), F32),
        scratch_shapes=[pltpu.VMEM((ML_HEADS, ML_QK, ML_V), F32),
                        pltpu.VMEM((SUBLANES, LANES), F32),
                        pltpu.VMEM((SUBLANES, LANES), F32)],
        compiler_params=_params(("parallel", "arbitrary"), 32),
    )(p_b, p_b, p_b, p_f, bias)


def _post_body(yf_ref, yb_ref, bo_ref, g_ref, lnw_ref, lnb_ref, e_ref, et_ref,
               hf_ref, hb_ref, o_ref, nw_ref, ya_ref, yb_o_ref):
    e = e_ref[...]
    et = et_ref[...]
    y = yf_ref[...] + yb_ref[...]
    inv = 1.0 / RW_HEAD
    mu = _dot1(_dot1(y, e), et) * inv
    yc = y - mu
    var = _dot1(_dot1(yc * yc, e), et) * inv
    yn = yc * lax.rsqrt(var + RW_GN_EPS) * lnw_ref[...] + lnb_ref[...]
    ya_ref[...] = ((yn + bo_ref[...].astype(F32)) * g_ref[...].astype(F32)).astype(ya_ref.dtype)

    hsum = hf_ref[...] + hb_ref[...]
    parts = []
    for h in range(ML_HEADS):
        hh = hsum[:, h * ML_V:(h + 1) * ML_V]
        parts.append(hh * lax.rsqrt(jnp.mean(hh * hh, axis=-1, keepdims=True) + ML_NORM_EPS))
    hn = jnp.concatenate(parts, axis=1) * nw_ref[...]
    yb_o_ref[...] = (hn * _sigmoid(o_ref[...].astype(F32))).astype(yb_o_ref.dtype)


def _post(yf, yb, bonus, g, ln_w, ln_b, e, et, hf, hb, vo, norm_w):
    n = yf.shape[0]
    tm = min(256, n)
    tok = lambda c: pl.BlockSpec((tm, c), lambda i: (i, 0))
    const = lambda x: pl.BlockSpec(x.shape, lambda i: (0, 0))
    return pl.pallas_call(
        _post_body,
        grid=(n // tm,),
        in_specs=[tok(RW_W), tok(RW_W), tok(RW_W), tok(RW_W), const(ln_w), const(ln_b), const(e), const(et),
                  tok(ML_VW), tok(ML_VW), pl.BlockSpec((tm, ML_VW), lambda i: (i, (2 * ML_QKW + ML_VW) // ML_VW)),
                  const(norm_w)],
        out_specs=[tok(RW_W), tok(ML_VW)],
        out_shape=[jax.ShapeDtypeStruct((n, RW_W), BF16), jax.ShapeDtypeStruct((n, ML_VW), BF16)],
        compiler_params=_params(("parallel",), 40),
    )(yf, yb, bonus, g, ln_w, ln_b, e, et, hf, hb, vo, norm_w)


def _merge_body(ya_ref, yb_ref, pa_ref, pb_ref, ga_ref, gb_ref, ba_ref, bb_ref, o_ref):
    pa = _mm(ya_ref[...], pa_ref[...])
    pb = _mm(yb_ref[...], pb_ref[...])
    ga = ga_ref[...].astype(F32) + ba_ref[...]
    gb = gb_ref[...].astype(F32) + bb_ref[...]
    o_ref[...] = (_sigmoid(ga) * pa + _sigmoid(gb) * pb).astype(o_ref.dtype)


def _merge(ya, yb, p_a, p_b, proj_b, b_gate):
    n = ya.shape[0]
    tm = min(1024, n)
    tn = 1024
    nj = D_MODEL // tn
    g0 = PB_GATE_OFF // tn
    return pl.pallas_call(
        _merge_body,
        grid=(n // tm, nj),
        in_specs=[pl.BlockSpec((tm, RW_W), lambda i, j: (i, 0)), pl.BlockSpec((tm, ML_VW), lambda i, j: (i, 0)),
                  pl.BlockSpec((RW_W, tn), lambda i, j: (0, j)), pl.BlockSpec((ML_VW, tn), lambda i, j: (0, j)),
                  pl.BlockSpec((tm, tn), lambda i, j: (i, g0 + j)), pl.BlockSpec((tm, tn), lambda i, j: (i, g0 + nj + j)),
                  pl.BlockSpec((1, tn), lambda i, j: (0, j)), pl.BlockSpec((1, tn), lambda i, j: (0, j + nj))],
        out_specs=pl.BlockSpec((tm, tn), lambda i, j: (i, j)),
        out_shape=jax.ShapeDtypeStruct((n, D_MODEL), BF16),
        compiler_params=_params(("parallel", "parallel"), 40),
    )(ya, yb, p_a, p_b, proj_b, proj_b, b_gate, b_gate)


def _resid_mm_body(a_ref, w_ref, x_ref, o_ref):
    o_ref[...] = x_ref[...] + _mm(a_ref[...], w_ref[...])


def _resid_mm(a, w, x):
    n, kdim = a.shape
    c = w.shape[1]
    tm = min(1024, n)
    tn = 1024
    return pl.pallas_call(
        _resid_mm_body,
        grid=(n // tm, c // tn),
        in_specs=[pl.BlockSpec((tm, kdim), lambda i, j: (i, 0)), pl.BlockSpec((kdim, tn), lambda i, j: (0, j)),
                  pl.BlockSpec((tm, tn), lambda i, j: (i, j))],
        out_specs=pl.BlockSpec((tm, tn), lambda i, j: (i, j)),
        out_shape=jax.ShapeDtypeStruct((n, c), F32),
        compiler_params=_params(("parallel", "parallel"), 48),
    )(a, w, x)


def _topk_rows(s, k, payload=None):
    nrow, t = s.shape
    rid = lax.broadcasted_iota(jnp.int32, (nrow, t), 0).astype(F32)
    kid = lax.broadcasted_iota(jnp.int32, (k, t), 0)
    vals = jnp.zeros((k, t), F32)
    sel = jnp.zeros((k, t), F32)
    for j in range(k):
        m = jnp.max(s, axis=0, keepdims=True)
        pos = jnp.min(jnp.where(s == m, rid, float(nrow)), axis=0, keepdims=True)
        hit = rid == pos
        if payload is None:
            picked = pos
        else:
            picked = jnp.max(jnp.where(hit, payload, -1.0), axis=0, keepdims=True)
        vals = jnp.where(kid == j, m, vals)
        sel = jnp.where(kid == j, picked, sel)
        s = jnp.where(hit, -jnp.inf, s)
    return vals, sel


def _router_body(q_ref, keys_ref, a_ref, b_ref, gw_ref):
    K = PEER_TOPK
    a_parts, b_parts, w_parts = [], [], []
    for h in range(PEER_HEADS):
        sv, si = [], []
        for p in range(2):
            c0 = (h * 2 + p) * PEER_HALF
            st = _dot3(keys_ref[h * 2 + p], q_ref[:, c0:c0 + PEER_HALF], NT)
            vals, idx = _topk_rows(st, K)
            sv.append(vals)
            si.append(idx)
        jid = lax.broadcasted_iota(jnp.int32, (SUBLANES, sv[0].shape[1]), 0)
        c_parts = [sv[0][0:1, :] + sv[1]]
        i_parts = [si[0][0:1, :] * float(PEER_NKEYS) + si[1]]
        for i in range(1, SUBLANES):
            keep = jid < K // (i + 1)
            c_parts.append(jnp.where(keep, sv[0][i:i + 1, :] + sv[1][0:SUBLANES, :], -jnp.inf))
            i_parts.append(si[0][i:i + 1, :] * float(PEER_NKEYS) + si[1][0:SUBLANES, :])
        c_parts.append(sv[0][SUBLANES:K, :] + sv[1][0:1, :])
        i_parts.append(si[0][SUBLANES:K, :] * float(PEER_NKEYS) + si[1][0:1, :])
        cand = jnp.concatenate(c_parts, axis=0)
        cidx = jnp.concatenate(i_parts, axis=0)
        best, eidx = _topk_rows(cand, K, payload=cidx)
        ex = jnp.exp(best - best[0:1, :])
        w_parts.append(ex / jnp.sum(ex, axis=0, keepdims=True))
        hi = jnp.floor(eidx * (1.0 / PEER_NKEYS))
        a_parts.append(hi)
        b_parts.append(eidx - hi * float(PEER_NKEYS))
    a_ref[...] = jnp.concatenate(a_parts, axis=0).T
    b_ref[...] = jnp.concatenate(b_parts, axis=0).T
    gw_ref[...] = jnp.concatenate(w_parts, axis=0).T


def _router(q, keys):
    n = q.shape[0]
    tq = min(256, n)
    tok = pl.BlockSpec((tq, PEER_HK), lambda i: (i, 0))
    sds = jax.ShapeDtypeStruct((n, PEER_HK), F32)
    return pl.pallas_call(
        _router_body,
        grid=(n // tq,),
        in_specs=[pl.BlockSpec((tq, q.shape[1]), lambda i: (i, 0)),
                  pl.BlockSpec(keys.shape, lambda i: (0, 0, 0))],
        out_specs=[tok, tok, tok],
        out_shape=[sds, sds, sds],
        compiler_params=_params(("parallel",), 32),
    )(q, keys)


PEER_CI = 8
PEER_CE = PEER_CI * PEER_NKEYS


PEER_SPLIT = 1
PEER_KEYS_PER_CALL = PEER_NKEYS // PEER_SPLIT
HI16 = 0xFFFF0000


def _peer_body(x_ref, h_ref, a_ref, b_ref, w_ref, ut_ref, v_ref, gf_ref, o_ref, g2_ref, z_ref, *,
               pitch, key0, final_norm):
    j = pl.program_id(1)
    last = pl.num_programs(1) - 1
    t = x_ref.shape[0]
    half = t // 2
    cur = j % 2

    def produce():
        z_ref[cur] = _mm(h_ref[...], ut_ref[...])

    def consume():
        z = z_ref[1 - cur]
        parts = []
        for ii in range(PEER_CI):
            start = pl.multiple_of(((j - 1) * PEER_CI + ii) * pitch, SUBLANES)
            words = g2_ref[pl.ds(start, half), :]
            g_first = lax.bitcast_convert_type(words << 16, F32)
            g_second = lax.bitcast_convert_type(words & jnp.uint32(HI16), F32)
            zz = z[:, ii * PEER_NKEYS:(ii + 1) * PEER_NKEYS]
            act = 0.5 * zz * (1.0 + lax.erf(zz * (2.0 ** -0.5)))
            parts.append(jnp.concatenate([act[:half] * g_first, act[half:] * g_second], axis=0).astype(BF16))
        o_ref[...] += _mm(jnp.concatenate(parts, axis=1), v_ref[...])

    @pl.when(j == 0)
    def _():
        o_ref[...] = x_ref[...]
        kid = lax.broadcasted_iota(jnp.int32, (PEER_KEYS_PER_CALL, PEER_HK), 0).astype(F32) + float(key0)
        kid_b = lax.broadcasted_iota(jnp.int32, (PEER_NKEYS, PEER_HK), 0).astype(F32)

        def g_bits(ti):
            arow = a_ref[pl.ds(ti, 1), :]
            brow = b_ref[pl.ds(ti, 1), :]
            wrow = w_ref[pl.ds(ti, 1), :]
            lhs = jnp.where(kid == arow, wrow, 0.0).astype(BF16)
            rhs = jnp.where(kid_b == brow, 1.0, 0.0).astype(BF16)
            g_t = _mm(lhs, rhs, NT)
            return lax.bitcast_convert_type(g_t.astype(BF16).astype(F32), jnp.uint32)

        def per_pair(ti, carry):
            words = (g_bits(ti) >> 16) | g_bits(ti + half)
            g2_ref[pl.ds(ti, PEER_KEYS_PER_CALL, stride=pitch), :] = words
            return carry

        lax.fori_loop(0, half, per_pair, 0, unroll=8)
        produce()

    @pl.when(jnp.logical_and(j > 0, j < last))
    def _():
        produce()
        consume()

    @pl.when(j == last)
    def _():
        consume()

    if final_norm:
        @pl.when(j == last)
        def _():
            y = o_ref[...]
            o_ref[...] = y * lax.rsqrt(jnp.mean(y * y, axis=-1, keepdims=True) + RMS_EPS) * gf_ref[...]


def _peer(x, h, a_idx, b_idx, gw, u_t, v_tab, g_final, final_norm):
    n, d = x.shape
    t = min(512, n)
    pitch = t // 2 + SUBLANES
    nch = PEER_N // PEER_CE // PEER_SPLIT
    once = pl.Buffered(1)
    tok = lambda c: pl.BlockSpec((t, c), lambda i, j: (i, 0), pipeline_mode=once)
    for part in range(PEER_SPLIT):
        c0 = part * nch
        is_last = part == PEER_SPLIT - 1
        x = pl.pallas_call(
            functools.partial(_peer_body, pitch=pitch, key0=part * PEER_KEYS_PER_CALL,
                              final_norm=final_norm and is_last),
            grid=(n // t, nch + 1),
            in_specs=[tok(d), tok(d), tok(PEER_HK), tok(PEER_HK), tok(PEER_HK),
                      pl.BlockSpec((d, PEER_CE), lambda i, j, c0=c0: (0, c0 + jnp.minimum(j, nch - 1))),
                      pl.BlockSpec((PEER_CE, d), lambda i, j, c0=c0: (c0 + jnp.maximum(j - 1, 0), 0)),
                      pl.BlockSpec((1, d), lambda i, j: (0, 0))],
            out_specs=pl.BlockSpec((t, d), lambda i, j: (i, 0)),
            out_shape=jax.ShapeDtypeStruct((n, d), F32),
            scratch_shapes=[pltpu.VMEM((PEER_KEYS_PER_CALL * pitch, LANES), jnp.uint32),
                            pltpu.VMEM((2, t, PEER_CE), F32)],
            compiler_params=_params(("parallel", "arbitrary"), 56),
        )(x, h, a_idx, b_idx, gw, u_t, v_tab, g_final)
    return x


def _pad_cols(pieces, width):
    rows = pieces[0][0].shape[0]
    out = jnp.zeros((rows, width), pieces[0][0].dtype)
    for arr, off in pieces:
        out = out.at[:, off:off + arr.shape[1]].set(arr)
    return out


def _pad_rows(x, rows):
    return jnp.zeros((rows,) + x.shape[1:], x.dtype).at[:x.shape[0]].set(x)


def _lora_layout(x):
    o = 3 * RW_W
    return _pad_cols([(x[:, o:o + 64], 0), (x[:, o + 64:o + 128], 128), (x[:, o + 128:o + 288], 256)], 512)


def _prep_layer(l, w):
    f = {}
    w_in = w["w_in"][l]
    rw = w_in[:, :RW_COLS]
    ml = w_in[:, RW_COLS:RW_COLS + ML_COLS]
    gt = w_in[:, RW_COLS + ML_COLS:]
    f["norm_mix"] = w["norm_mix"][l][None, :]
    f["w_f"] = _pad_cols([(rw[:, :3 * RW_W], 0), (_lora_layout(rw), PF_LORA_OFF),
                          (ml[:, 2 * ML_QKW + 2 * ML_VW:], PF_MLG_OFF)], PF_COLS).astype(BF16)
    f["w_b"] = jnp.concatenate([ml[:, :2 * ML_QKW + 2 * ML_VW], gt], axis=1).astype(BF16)
    mu = jnp.stack([w["rw_mu_prev"][l], w["rw_mu_next"][l]])
    f["mu_main"] = mu[:, :3 * RW_W]
    f["mu_lora"] = _lora_layout(mu)
    f["w0"] = w["rw_w0"][l]
    f["w2p"] = jnp.stack([_pad_rows(w["rw_w2"][l, 0], 128), _pad_rows(w["rw_w2"][l, 1], 128)]).astype(BF16)
    f["a0"] = w["rw_a0"][l][None, :]
    f["a2p"] = _pad_rows(w["rw_a2"][l], 128).astype(BF16)
    f["g2p"] = _pad_rows(w["rw_g2"][l], 256).astype(BF16)
    f["k_k"] = w["rw_k_k"][l][None, :]
    f["k_a"] = w["rw_k_a"][l][None, :]
    f["r_k"] = w["rw_r_k"][l].reshape(1, RW_W)
    f["ln_w"] = w["rw_ln_w"][l][None, :]
    f["ln_b"] = w["rw_ln_b"][l][None, :]
    f["ml_bias"] = _pad_cols([(w["ml_b_i"][l].reshape(1, -1), 0), (w["ml_b_f"][l].reshape(1, -1), 2 * ML_HEADS)], LANES)
    f["ml_norm_w"] = w["ml_norm_w"][l][None, :]
    f["p_a"] = w["p_a"][l].astype(BF16)
    f["p_b"] = w["p_b"][l].astype(BF16)
    f["b_gate"] = w["b_gate"][l][None, :]
    f["w_out"] = w["w_out"][l].astype(BF16)
    f["norm_ffn"] = w["norm_ffn"][l][None, :]
    f["wq"] = (w["peer_wq"][l].astype(BF16),)
    f["keys"] = w["peer_keys"][l].reshape(PEER_HEADS * 2, PEER_NKEYS, PEER_HALF)
    f["u_t"] = w["peer_u"][l].astype(BF16).T
    f["v"] = w["peer_v"][l].astype(BF16)
    return f


def _head_selectors():
    ch = jnp.arange(RW_W)[:, None] // RW_HEAD
    e = (ch == jnp.arange(LANES)[None, :]).astype(BF16)
    return e, e.T


def _layer(x, f, e, et, g_final, final_norm):
    b, t, d = x.shape
    n = b * t
    x2 = x.reshape(n, d)
    (hn,) = _norm_cast(x2, f["norm_mix"], 1)
    p_f = _matmul((hn,), (f["w_f"],), F32)
    p_b = _matmul((hn,), (f["w_b"],), BF16)
    p_f3 = p_f.reshape(b, t, -1)
    p_b3 = p_b.reshape(b, t, -1)

    r, k, v, kn, ab, lw0, lw1, gg, bonus = _rw_prep(
        p_f3, f["mu_main"], f["mu_lora"], f["w0"], f["w2p"], f["a0"], f["a2p"], f["g2p"], f["k_k"], f["k_a"],
        f["r_k"], e, et)
    y_f = _rw_scan(r, k, v, kn, ab, lw0, rev=False)
    y_b = _rw_scan(r, k, v, kn, ab, lw1, rev=True)

    h_f = _ml_scan(p_b3, p_f3, f["ml_bias"], rev=False)
    h_b = _ml_scan(p_b3, p_f3, f["ml_bias"], rev=True)

    ya, yb = _post(y_f.reshape(n, -1), y_b.reshape(n, -1), bonus.reshape(n, -1), gg.reshape(n, -1),
                   f["ln_w"], f["ln_b"], e, et, h_f.reshape(n, -1), h_b.reshape(n, -1), p_b, f["ml_norm_w"])
    merged = _merge(ya, yb, f["p_a"], f["p_b"], p_b, f["b_gate"])
    x1 = _resid_mm(merged, f["w_out"], x2)

    h_parts = _norm_cast(x1, f["norm_ffn"], 1)
    q = _matmul(h_parts, f["wq"], F32)
    a_idx, b_idx, gw = _router(q, f["keys"])
    x_out = _peer(x1, h_parts[0], a_idx, b_idx, gw, f["u_t"], f["v"], g_final, final_norm)
    return x_out.reshape(b, t, d)


def _trunk(x, layers, e, et, g_final):
    for l, f in enumerate(layers):
        x = _layer(x, f, e, et, g_final, final_norm=(l == len(layers) - 1))
    return x


def kernel(x_prompt, x_sample, norm_mix, w_in, rw_mu_prev, rw_mu_next, rw_w0, rw_w2, rw_a0, rw_a2, rw_g2, rw_k_k, rw_k_a, rw_r_k, rw_ln_w, rw_ln_b, ml_b_i, ml_b_f, ml_norm_w, p_a, p_b, b_gate, w_out, norm_ffn, peer_wq, peer_keys, peer_u, peer_v, norm_final):
    w = dict(norm_mix=norm_mix, w_in=w_in, rw_mu_prev=rw_mu_prev, rw_mu_next=rw_mu_next, rw_w0=rw_w0, rw_w2=rw_w2,
             rw_a0=rw_a0, rw_a2=rw_a2, rw_g2=rw_g2, rw_k_k=rw_k_k, rw_k_a=rw_k_a, rw_r_k=rw_r_k, rw_ln_w=rw_ln_w,
             rw_ln_b=rw_ln_b, ml_b_i=ml_b_i, ml_b_f=ml_b_f, ml_norm_w=ml_norm_w, p_a=p_a, p_b=p_b, b_gate=b_gate,
             w_out=w_out, norm_ffn=norm_ffn, peer_wq=peer_wq, peer_keys=peer_keys, peer_u=peer_u, peer_v=peer_v)
    depth = w_in.shape[0]
    layers = [_prep_layer(l, w) for l in range(depth)]
    e, et = _head_selectors()
    g_final = norm_final[None, :]
    return (_trunk(x_prompt, layers, e, et, g_final), _trunk(x_sample, layers, e, et, g_final))
```

```python
import functools
import math

import jax
import jax.numpy as jnp
from jax import lax
from jax.experimental import pallas as pl
from jax.experimental.pallas import tpu as pltpu

F32 = jnp.float32
BF16 = jnp.bfloat16

D_MODEL = 2048
RMS_EPS = 1e-6
RW_W = 1024
RW_HEAD = 64
RW_HEADS = 16
RW_DECAY_LORA = 64
RW_A_LORA = 64
RW_G_LORA = 160
RW_GN_EPS = 64e-5
RW_COLS = 3 * RW_W + RW_DECAY_LORA + RW_A_LORA + RW_G_LORA
RW_CHUNK = 64
RW_SUB = 4
RW_PASSES_A = 1
RW_PASSES_INV = 1
RW_PASSES_S = 1
ML_HEADS = 4
ML_QK = 128
ML_V = 256
ML_QKW = ML_HEADS * ML_QK
ML_VW = ML_HEADS * ML_V
ML_CHUNK = 128
ML_NORM_EPS = 1e-6
ML_COLS = 2 * ML_QKW + 2 * ML_VW + 4 * ML_HEADS
PF_LORA_OFF = 3 * RW_W
PF_LORA_W = 512
PF_MLG_OFF = PF_LORA_OFF + PF_LORA_W
PF_COLS = 4096
PB_GATE_OFF = 2 * ML_QKW + 2 * ML_VW
PB_COLS = PB_GATE_OFF + 2 * D_MODEL
PEER_HEADS = 8
PEER_NKEYS = 128
PEER_N = PEER_NKEYS * PEER_NKEYS
PEER_HALF = 128
PEER_TOPK = 16
PEER_HK = PEER_HEADS * PEER_TOPK
LANES = 128
SUBLANES = 8

NN = ((1,), (0,))
NT = ((1,), (1,))


def _mm(a, b, dims=NN):
    return lax.dot_general(a, b, (dims, ((), ())), preferred_element_type=F32)


def _split2(a):
    hi = a.astype(BF16)
    lo = (a - hi.astype(F32)).astype(BF16)
    return hi, lo


def _split3(a):
    hi = a.astype(BF16)
    r1 = a - hi.astype(F32)
    mid = r1.astype(BF16)
    lo = (r1 - mid.astype(F32)).astype(BF16)
    return hi, mid, lo


def _dot1(a, b, dims=NN):
    return _mm(a.astype(BF16), b.astype(BF16), dims)


def _dot3(a, b, dims=NN):
    ah, al = _split2(a)
    bh, bl = _split2(b)
    return _mm(ah, bh, dims) + (_mm(ah, bl, dims) + _mm(al, bh, dims))


def _dotp(a, b, dims, passes):
    return _dot3(a, b, dims) if passes == 3 else _dot1(a, b, dims)


def _dot_sel(sel_bf16, x, dims=NN):
    h, m, l = _split3(x)
    return _mm(sel_bf16, h, dims) + (_mm(sel_bf16, m, dims) + _mm(sel_bf16, l, dims))


def _dot_xsel(x, sel_bf16, dims=NN):
    h, m, l = _split3(x)
    return _mm(h, sel_bf16, dims) + (_mm(m, sel_bf16, dims) + _mm(l, sel_bf16, dims))


def _params(sem, vmem_mb):
    return pltpu.CompilerParams(dimension_semantics=sem, vmem_limit_bytes=vmem_mb << 20)


def _sigmoid(x):
    return 1.0 / (1.0 + jnp.exp(-x))


def _norm_cast_body(x_ref, g_ref, *o_refs):
    xf = x_ref[...]
    h = xf * lax.rsqrt(jnp.mean(xf * xf, axis=-1, keepdims=True) + RMS_EPS) * g_ref[...]
    hi = h.astype(BF16)
    o_refs[0][...] = hi
    if len(o_refs) == 2:
        o_refs[1][...] = (h - hi.astype(F32)).astype(BF16)


def _norm_cast(x, g, n_parts):
    n, d = x.shape
    tm = min(512, n)
    spec = pl.BlockSpec((tm, d), lambda i: (i, 0))
    return pl.pallas_call(
        _norm_cast_body,
        grid=(n // tm,),
        in_specs=[spec, pl.BlockSpec((1, d), lambda i: (0, 0))],
        out_specs=[spec] * n_parts,
        out_shape=[jax.ShapeDtypeStruct((n, d), BF16)] * n_parts,
        compiler_params=_params(("parallel",), 32),
    )(x, g)


def _matmul_body(*refs, n_parts):
    a_refs = refs[:n_parts]
    w_refs = refs[n_parts:2 * n_parts]
    o_ref = refs[2 * n_parts]
    acc = _mm(a_refs[0][...], w_refs[0][...])
    if n_parts == 2:
        acc = acc + (_mm(a_refs[0][...], w_refs[1][...]) + _mm(a_refs[1][...], w_refs[0][...]))
    o_ref[...] = acc.astype(o_ref.dtype)


def _matmul(a_parts, w_parts, out_dtype):
    n, d = a_parts[0].shape
    c = w_parts[0].shape[1]
    n_parts = len(a_parts)
    tm = min(1024, n)
    tn = min(1024 // n_parts, c)
    return pl.pallas_call(
        functools.partial(_matmul_body, n_parts=n_parts),
        grid=(n // tm, c // tn),
        in_specs=[pl.BlockSpec((tm, d), lambda i, j: (i, 0))] * n_parts
        + [pl.BlockSpec((d, tn), lambda i, j: (0, j))] * n_parts,
        out_specs=pl.BlockSpec((tm, tn), lambda i, j: (i, j)),
        out_shape=jax.ShapeDtypeStruct((n, c), out_dtype),
        compiler_params=_params(("parallel", "parallel"), 48),
    )(*a_parts, *w_parts)


def _shift_rows(p, prev_row, next_row, mu_prev, mu_next):
    n = p.shape[0]
    rid = lax.broadcasted_iota(jnp.int32, p.shape, 0)
    prev = jnp.where(rid == 0, prev_row, pltpu.roll(p, 1, 0))
    nxt = jnp.where(rid == n - 1, next_row, pltpu.roll(p, n - 1, 0))
    return p + mu_prev * (prev - p) + mu_next * (nxt - p)


def _rw_prep_body(m_ref, mp_ref, mn_ref, l_ref, lp_ref, ln_ref, mum_ref, mul_ref, w0_ref, w2_ref,
                  a0_ref, a2_ref, g2_ref, kk_ref, ka_ref, rk_ref, e_ref, et_ref,
                  r_o, k_o, v_o, kn_o, ab_o, lw0_o, lw1_o, g_o, bo_o):
    i = pl.program_id(1)
    first = i == 0
    last = i == pl.num_programs(1) - 1
    zero_m = jnp.zeros((1, m_ref.shape[2]), F32)
    zero_l = jnp.zeros((1, l_ref.shape[2]), F32)
    pm = _shift_rows(m_ref[0],
                     jnp.where(first, zero_m, mp_ref[0, SUBLANES - 1:SUBLANES, :]),
                     jnp.where(last, zero_m, mn_ref[0, 0:1, :]),
                     mum_ref[0:1, :], mum_ref[1:2, :])
    plo = _shift_rows(l_ref[0],
                      jnp.where(first, zero_l, lp_ref[0, SUBLANES - 1:SUBLANES, :]),
                      jnp.where(last, zero_l, ln_ref[0, 0:1, :]),
                      mul_ref[0:1, :], mul_ref[1:2, :])
    r = pm[:, 0:RW_W]
    k = pm[:, RW_W:2 * RW_W]
    v = pm[:, 2 * RW_W:3 * RW_W]
    wd = plo[:, 0:128]
    ad = plo[:, 128:256]
    gd = plo[:, 256:512]

    a = _sigmoid(a0_ref[...] + _dot1(ad, a2_ref[...]))
    g = _dot1(_sigmoid(gd), g2_ref[...])
    e = e_ref[...]
    et = et_ref[...]
    kk = k * kk_ref[...]
    ss = _dot1(_dot1(kk * kk, e), et)
    kn = kk / jnp.maximum(jnp.sqrt(ss), 1e-12)
    kmod = k * (1.0 + (a - 1.0) * ka_ref[...])
    wl = jnp.tanh(wd)
    c = math.exp(-0.5)
    lw0 = -_sigmoid(w0_ref[0:1, :] + _dot1(wl, w2_ref[0])) * c
    lw1 = -_sigmoid(w0_ref[1:2, :] + _dot1(wl, w2_ref[1])) * c
    rks = _dot1(_dot1(r * kmod * rk_ref[...], e), et)
    r_o[0] = r.astype(r_o.dtype)
    k_o[0] = kmod.astype(k_o.dtype)
    v_o[0] = v.astype(v_o.dtype)
    kn_o[0] = kn.astype(kn_o.dtype)
    ab_o[0] = (kn * a).astype(ab_o.dtype)
    lw0_o[0] = lw0
    lw1_o[0] = lw1
    g_o[0] = g.astype(g_o.dtype)
    bo_o[0] = (rks * v).astype(bo_o.dtype)


def _rw_prep(p_f, mu_main, mu_lora, w0, w2p, a0, a2p, g2p, k_k, k_a, r_k, e, et):
    b, t, _ = p_f.shape
    cm, cl = 3 * RW_W, PF_LORA_W
    col = {cm: 0, cl: PF_LORA_OFF // PF_LORA_W}
    tt = min(128, t)
    nb = tt // SUBLANES
    nt8 = t // SUBLANES

    def main_spec(c):
        return pl.BlockSpec((1, tt, c), lambda bi, i: (bi, i, col[c]))

    def prev_spec(c):
        return pl.BlockSpec((1, SUBLANES, c), lambda bi, i: (bi, jnp.maximum(i * nb - 1, 0), col[c]))

    def next_spec(c):
        return pl.BlockSpec((1, SUBLANES, c), lambda bi, i: (bi, jnp.minimum((i + 1) * nb, nt8 - 1), col[c]))

    def full(x):
        nd = x.ndim
        return pl.BlockSpec(x.shape, lambda bi, i: (0,) * nd)

    consts = (mu_main, mu_lora, w0, w2p, a0, a2p, g2p, k_k, k_a, r_k, e, et)
    out_dtypes = [BF16] * 5 + [F32, F32, BF16, BF16]
    return pl.pallas_call(
        _rw_prep_body,
        grid=(b, t // tt),
        in_specs=[main_spec(cm), prev_spec(cm), next_spec(cm), main_spec(cl), prev_spec(cl), next_spec(cl)]
        + [full(x) for x in consts],
        out_specs=[pl.BlockSpec((1, tt, RW_W), lambda bi, i: (bi, i, 0))] * 9,
        out_shape=[jax.ShapeDtypeStruct((b, t, RW_W), dt) for dt in out_dtypes],
        compiler_params=_params(("parallel", "parallel"), 56),
    )(p_f, p_f, p_f, p_f, p_f, p_f, *consts)


def _rw_scan_body(r_ref, k_ref, v_ref, kn_ref, ab_ref, lw_ref, y_ref, s_ref, *, rev):
    L = RW_CHUNK

    @pl.when(pl.program_id(1) == 0)
    def _():
        s_ref[...] = jnp.zeros_like(s_ref)

    row = lax.broadcasted_iota(jnp.int32, (L, L), 0)
    col = lax.broadcasted_iota(jnp.int32, (L, L), 1)
    if rev:
        incl = col >= row
        strict = col > row
    else:
        incl = col <= row
        strict = col < row
    tri = jnp.where(incl, 1.0, 0.0).astype(BF16)
    eye = jnp.where(row == col, 1.0, 0.0).astype(F32)
    lane = lax.broadcasted_iota(jnp.int32, (L, LANES), 1)
    head_masks = (lane < RW_HEAD, lane >= RW_HEAD)
    sr = lax.broadcasted_iota(jnp.int32, (LANES, LANES), 0)
    sc = lax.broadcasted_iota(jnp.int32, (LANES, LANES), 1)
    bd = (sr < RW_HEAD) == (sc < RW_HEAD)
    r4 = lax.broadcasted_iota(jnp.int32, (4 * L, 2 * L), 0)
    c4 = lax.broadcasted_iota(jnp.int32, (4 * L, 2 * L), 1) & (L - 1)
    t4 = r4 & (L - 1)
    own = jnp.where((r4 & L) != 0, 1, 0)
    causal4 = (c4 > t4 - own) if rev else (c4 < t4 + own)

    heads =[(hp, h) for hp in range(RW_HEADS // 2) for h in range(2)]
    sls = [slice(hp * LANES, (hp + 1) * LANES) for hp in range(RW_HEADS // 2)]

    def intra(offs):
        pre = []
        for off in offs:
            rows = slice(off, off + L)
            lw = lw_ref[0, rows, :]
            cum = _dot_sel(tri, lw)
            tot = cum[0:1, :] if rev else cum[L - 1:L, :]
            p_inv = jnp.exp(-cum)
            p_end = jnp.exp(tot - cum)
            ab = ab_ref[0, rows, :].astype(F32)
            k = k_ref[0, rows, :].astype(F32)
            pre.append(dict(at=-kn_ref[0, rows, :].astype(F32) * jnp.exp(cum - lw),
                            rt=r_ref[0, rows, :].astype(F32) * jnp.exp(cum), bt=ab * p_inv, kt=k * p_inv,
                            v=v_ref[0, rows, :].astype(F32), bt_end=ab * p_end, kt_end=k * p_end,
                            p_all=jnp.exp(tot)))
        zeros_lv = jnp.zeros((L, LANES), F32)
        nck = len(offs)
        at_h = [[] for _ in range(nck)]
        blks = [[] for _ in range(nck)]
        for ci, c in enumerate(pre):
            for hp in range(RW_HEADS // 2):
                sl = sls[hp]
                rows4 = []
                for h in range(2):
                    at_m = jnp.where(head_masks[h], c["at"][:, sl], 0.0)
                    at_h[ci].append(at_m)
                    rows4 += [at_m, jnp.where(head_masks[h], c["rt"][:, sl], 0.0)]
                blk = _dotp(jnp.concatenate(rows4, axis=0), jnp.concatenate([c["bt"][:, sl], c["kt"][:, sl]], axis=0),
                            NT, RW_PASSES_A)
                blks[ci].append(jnp.where(causal4, blk, 0.0))
        a_ab = [[blks[ci][hp][2 * h * L:(2 * h + 1) * L, :L] for hp, h in heads] for ci in range(nck)]
        a_rb = [[blks[ci][hp][(2 * h + 1) * L:(2 * h + 2) * L, :L] for hp, h in heads] for ci in range(nck)]
        av_pair = [[_dotp(blks[ci][hp], jnp.concatenate([zeros_lv, pre[ci]["v"][:, sls[hp]]], axis=0), NN, RW_PASSES_A)
                    for hp in range(RW_HEADS // 2)] for ci in range(nck)]
        av = [[av_pair[ci][hp][2 * h * L:(2 * h + 2) * L] for hp, h in heads] for ci in range(nck)]
        tm = [[eye + a for a in a_ab[ci]] for ci in range(nck)]
        pw = [[_dotp(a, a, NN, RW_PASSES_INV) for a in a_ab[ci]] for ci in range(nck)]
        for _ in range(4):
            st = [[_dotp(jnp.concatenate([t_, p], axis=0), p, NN, RW_PASSES_INV) for t_, p in zip(tm[ci], pw[ci])]
                  for ci in range(nck)]
            tm = [[t_ + s_[:L] for t_, s_ in zip(tm[ci], st[ci])] for ci in range(nck)]
            pw = [[s_[L:] for s_ in st[ci]] for ci in range(nck)]
        tm = [[t_ + _dotp(t_, p, NN, RW_PASSES_INV) for t_, p in zip(tm[ci], pw[ci])] for ci in range(nck)]
        out = []
        for ci, c in enumerate(pre):
            tx = [_dotp(tm[ci][i], jnp.concatenate([at_h[ci][i], jnp.where(head_masks[h], av[ci][i][:L], 0.0)], axis=1),
                        NN, RW_PASSES_A) for i, (hp, h) in enumerate(heads)]
            bk_t = [jnp.concatenate([c["bt_end"][:, sl], c["kt_end"][:, sl]], axis=0).T for sl in sls]
            p_col = [jnp.sum(jnp.where(sr == sc, jnp.broadcast_to(c["p_all"][:, sl], (LANES, LANES)), 0.0), axis=1,
                             keepdims=True) for sl in sls]
            out.append(dict(tx=tx, av=av[ci], a_rb=a_rb[ci], rt=c["rt"], v=c["v"], bk_t=bk_t, p_col=p_col))
        return out

    def sequential(c, off, states):
        tx, av, a_rb = c["tx"], c["av"], c["a_rb"]
        out = []
        for hp in range(RW_HEADS // 2):
            sl = sls[hp]
            i0, i1 = 2 * hp, 2 * hp + 1
            w_pair = tx[i0][:, :LANES] + tx[i1][:, :LANES]
            u0_pair = tx[i0][:, LANES:] + tx[i1][:, LANES:]
            y0_pair = jnp.where(head_masks[0], av[i0][L:], av[i1][L:])
            s = states[hp]
            ws = _dotp(jnp.concatenate([w_pair, c["rt"][:, sl]], axis=0), s, NN, RW_PASSES_S)
            u = ws[:L] + u0_pair
            y = ws[L:] + y0_pair + jnp.where(head_masks[0], _dotp(a_rb[i0], u, NN, RW_PASSES_A),
                                             _dotp(a_rb[i1], u, NN, RW_PASSES_A))
            upd = _dotp(c["bk_t"][hp], jnp.concatenate([u, c["v"][:, sl]], axis=0), NN, RW_PASSES_S)
            out.append(s * c["p_col"][hp] + jnp.where(bd, upd, 0.0))
            y_ref[0, off:off + L, sl] = y
        return out

    offs = [c * L for c in range(RW_SUB)]
    if rev:
        offs = offs[::-1]
    chunks = intra(offs)
    states = [s_ref[hp] for hp in range(RW_HEADS // 2)]
    for c, off in zip(chunks, offs):
        states = sequential(c, off, states)
    for hp in range(RW_HEADS // 2):
        s_ref[hp] = states[hp]


def _rw_scan(r, k, v, kn, ab, lw, rev):
    b, t, w = r.shape
    tb = RW_CHUNK * RW_SUB
    nc = t // tb
    if rev:
        idx = lambda bi, c: (bi, nc - 1 - c, 0)
    else:
        idx = lambda bi, c: (bi, c, 0)
    spec = pl.BlockSpec((1, tb, w), idx)
    return pl.pallas_call(
        functools.partial(_rw_scan_body, rev=rev),
        grid=(b, nc),
        in_specs=[spec] * 6,
        out_specs=spec,
        out_shape=jax.ShapeDtypeStruct((b, t, w), F32),
        scratch_shapes=[pltpu.VMEM((RW_HEADS // 2, LANES, LANES), F32)],
        compiler_params=_params(("parallel", "arbitrary"), 32),
    )(r, k, v, kn, ab, lw)


def _ml_scan_body(q_ref, k_ref, v_ref, g_ref, b_ref, h_ref, c_ref, n_ref, m_ref, *, rev, direction):
    L = ML_CHUNK

    @pl.when(pl.program_id(1) == 0)
    def _():
        c_ref[...] = jnp.zeros_like(c_ref)
        n_ref[...] = jnp.zeros_like(n_ref)
        m_ref[...] = jnp.zeros_like(m_ref)

    row = lax.broadcasted_iota(jnp.int32, (L, L), 0)
    col = lax.broadcasted_iota(jnp.int32, (L, L), 1)
    incl = (col >= row) if rev else (col <= row)
    tri = jnp.where(incl, 1.0, 0.0).astype(BF16)
    lane = lax.broadcasted_iota(jnp.int32, (L, LANES), 1)
    sub = lax.broadcasted_iota(jnp.int32, (LANES, L), 0)

    g = g_ref[0] + b_ref[...]
    ls = jnp.minimum(g, 0.0) - jnp.log(1.0 + jnp.exp(-jnp.abs(g)))
    bc = _dot_sel(tri, ls)
    g_t = g.T
    bc_t = bc.T
    scale = ML_QK ** -0.5
    hs = range(ML_HEADS)
    ci = [direction * ML_HEADS + h for h in hs]
    cf = [2 * ML_HEADS + direction * ML_HEADS + h for h in hs]
    b_col = [jnp.sum(jnp.where(lane == cf[h], bc, 0.0), axis=1, keepdims=True) for h in hs]
    i_col = [jnp.sum(jnp.where(lane == ci[h], g, 0.0), axis=1, keepdims=True) for h in hs]
    b_row = [jnp.sum(jnp.where(sub == cf[h], bc_t, 0.0), axis=0, keepdims=True) for h in hs]
    i_row = [jnp.sum(jnp.where(sub == ci[h], g_t, 0.0), axis=0, keepdims=True) for h in hs]
    m_prev = [jnp.max(m_ref[h:h + 1, :], axis=1, keepdims=True) for h in hs]
    qs = [q_ref[0, :, h * ML_QK:(h + 1) * ML_QK].astype(F32) * scale for h in hs]
    kc = [k_ref[0, :, h * ML_QK:(h + 1) * ML_QK].astype(F32) for h in hs]
    vc = [v_ref[0, :, h * ML_V:(h + 1) * ML_V].astype(F32) for h in hs]
    qk = [_dot1(qs[h], kc[h], NT) for h in hs]
    qc = [_dot1(qs[h], c_ref[h]) for h in hs]
    dmat = [jnp.where(incl, b_col[h] - b_row[h] + i_row[h], -jnp.inf) for h in hs]
    inter = [b_col[h] + m_prev[h] for h in hs]
    m_t = [jnp.maximum(inter[h], jnp.max(dmat[h], axis=1, keepdims=True)) for h in hs]
    s = [qk[h] * jnp.exp(dmat[h] - m_t[h]) for h in hs]
    e_inter = [jnp.exp(inter[h] - m_t[h]) for h in hs]
    num = [_dot1(s[h], vc[h]) + e_inter[h] * qc[h] for h in hs]
    den = [jnp.sum(s[h], axis=1, keepdims=True)
           + e_inter[h] * jnp.sum(qs[h] * n_ref[h:h + 1, :], axis=1, keepdims=True) for h in hs]
    for h in hs:
        h_ref[0, :, h * ML_V:(h + 1) * ML_V] = num[h] / jnp.maximum(jnp.abs(den[h]), jnp.exp(-m_t[h]))
    b_last = [jnp.min(b_row[h], axis=1, keepdims=True) for h in hs]
    g_row = [b_last[h] - b_row[h] + i_row[h] for h in hs]
    g_col = [b_last[h] - b_col[h] + i_col[h] for h in hs]
    m_new = [jnp.maximum(b_last[h] + m_prev[h], jnp.max(g_row[h], axis=1, keepdims=True)) for h in hs]
    w_col = [jnp.exp(g_col[h] - m_new[h]) for h in hs]
    dec = [jnp.exp(b_last[h] + m_prev[h] - m_new[h]) for h in hs]
    upd = [_dot1(kc[h].T, vc[h] * w_col[h]) for h in hs]
    for h in hs:
        c_ref[h] = dec[h] * c_ref[h] + upd[h]
        n_ref[h:h + 1, :] = dec[h] * n_ref[h:h + 1, :] + jnp.sum(kc[h] * w_col[h], axis=0, keepdims=True)
        m_ref[h:h + 1, :] = jnp.broadcast_to(m_new[h], (1, LANES))


def _ml_scan(p_b, p_f, bias, rev):
    b, t, _ = p_b.shape
    L = ML_CHUNK
    nc = t // L
    tix = (lambda c: nc - 1 - c) if rev else (lambda c: c)
    return pl.pallas_call(
        functools.partial(_ml_scan_body, rev=rev, direction=1 if rev else 0),
        grid=(b, nc),
        in_specs=[pl.BlockSpec((1, L, ML_QKW), lambda bi, c: (bi, tix(c), 0)),
                  pl.BlockSpec((1, L, ML_QKW), lambda bi, c: (bi, tix(c), 1)),
                  pl.BlockSpec((1, L, ML_VW), lambda bi, c: (bi, tix(c), 1)),
                  pl.BlockSpec((1, L, LANES), lambda bi, c: (bi, tix(c), PF_MLG_OFF // LANES)),
                  pl.BlockSpec((1, LANES), lambda bi, c: (0, 0))],
        out_specs=pl.BlockSpec((1, L, ML_VW), lambda bi, c: (bi, tix(c), 0)),
        out_shape=jax.ShapeDtypeStruct((b, t, ML_VW), F32),
        scratch_shapes=[pltpu.VMEM((ML_HEADS, ML_QK, ML_V), F32),
                        pltpu.VMEM((SUBLANES, LANES), F32),
                        pltpu.VMEM((SUBLANES, LANES), F32)],
        compiler_params=_params(("parallel", "arbitrary"), 32),
    )(p_b, p_b, p_b, p_f, bias)


def _post_body(yf_ref, yb_ref, bo_ref, g_ref, lnw_ref, lnb_ref, e_ref, et_ref,
               hf_ref, hb_ref, o_ref, nw_ref, ya_ref, yb_o_ref):
    e = e_ref[...]
    et = et_ref[...]
    y = yf_ref[...] + yb_ref[...]
    inv = 1.0 / RW_HEAD
    mu = _dot1(_dot1(y, e), et) * inv
    yc = y - mu
    var = _dot1(_dot1(yc * yc, e), et) * inv
    yn = yc * lax.rsqrt(var + RW_GN_EPS) * lnw_ref[...] + lnb_ref[...]
    ya_ref[...] = ((yn + bo_ref[...].astype(F32)) * g_ref[...].astype(F32)).astype(ya_ref.dtype)

    hsum = hf_ref[...] + hb_ref[...]
    parts = []
    for h in range(ML_HEADS):
        hh = hsum[:, h * ML_V:(h + 1) * ML_V]
        parts.append(hh * lax.rsqrt(jnp.mean(hh * hh, axis=-1, keepdims=True) + ML_NORM_EPS))
    hn = jnp.concatenate(parts, axis=1) * nw_ref[...]
    yb_o_ref[...] = (hn * _sigmoid(o_ref[...].astype(F32))).astype(yb_o_ref.dtype)


def _post(yf, yb, bonus, g, ln_w, ln_b, e, et, hf, hb, vo, norm_w):
    n = yf.shape[0]
    tm = min(256, n)
    tok = lambda c: pl.BlockSpec((tm, c), lambda i: (i, 0))
    const = lambda x: pl.BlockSpec(x.shape, lambda i: (0, 0))
    return pl.pallas_call(
        _post_body,
        grid=(n // tm,),
        in_specs=[tok(RW_W), tok(RW_W), tok(RW_W), tok(RW_W), const(ln_w), const(ln_b), const(e), const(et),
                  tok(ML_VW), tok(ML_VW), pl.BlockSpec((tm, ML_VW), lambda i: (i, (2 * ML_QKW + ML_VW) // ML_VW)),
                  const(norm_w)],
        out_specs=[tok(RW_W), tok(ML_VW)],
        out_shape=[jax.ShapeDtypeStruct((n, RW_W), BF16), jax.ShapeDtypeStruct((n, ML_VW), BF16)],
        compiler_params=_params(("parallel",), 40),
    )(yf, yb, bonus, g, ln_w, ln_b, e, et, hf, hb, vo, norm_w)


def _merge_body(ya_ref, yb_ref, pa_ref, pb_ref, ga_ref, gb_ref, ba_ref, bb_ref, o_ref):
    pa = _mm(ya_ref[...], pa_ref[...])
    pb = _mm(yb_ref[...], pb_ref[...])
    ga = ga_ref[...].astype(F32) + ba_ref[...]
    gb = gb_ref[...].astype(F32) + bb_ref[...]
    o_ref[...] = (_sigmoid(ga) * pa + _sigmoid(gb) * pb).astype(o_ref.dtype)


def _merge(ya, yb, p_a, p_b, proj_b, b_gate):
    n = ya.shape[0]
    tm = min(1024, n)
    tn = 1024
    nj = D_MODEL // tn
    g0 = PB_GATE_OFF // tn
    return pl.pallas_call(
        _merge_body,
        grid=(n // tm, nj),
        in_specs=[pl.BlockSpec((tm, RW_W), lambda i, j: (i, 0)), pl.BlockSpec((tm, ML_VW), lambda i, j: (i, 0)),
                  pl.BlockSpec((RW_W, tn), lambda i, j: (0, j)), pl.BlockSpec((ML_VW, tn), lambda i, j: (0, j)),
                  pl.BlockSpec((tm, tn), lambda i, j: (i, g0 + j)), pl.BlockSpec((tm, tn), lambda i, j: (i, g0 + nj + j)),
                  pl.BlockSpec((1, tn), lambda i, j: (0, j)), pl.BlockSpec((1, tn), lambda i, j: (0, j + nj))],
        out_specs=pl.BlockSpec((tm, tn), lambda i, j: (i, j)),
        out_shape=jax.ShapeDtypeStruct((n, D_MODEL), BF16),
        compiler_params=_params(("parallel", "parallel"), 40),
    )(ya, yb, p_a, p_b, proj_b, proj_b, b_gate, b_gate)


def _resid_mm_body(a_ref, w_ref, x_ref, o_ref):
    o_ref[...] = x_ref[...] + _mm(a_ref[...], w_ref[...])


def _resid_mm(a, w, x):
    n, kdim = a.shape
    c = w.shape[1]
    tm = min(1024, n)
    tn = 1024
    return pl.pallas_call(
        _resid_mm_body,
        grid=(n // tm, c // tn),
        in_specs=[pl.BlockSpec((tm, kdim), lambda i, j: (i, 0)), pl.BlockSpec((kdim, tn), lambda i, j: (0, j)),
                  pl.BlockSpec((tm, tn), lambda i, j: (i, j))],
        out_specs=pl.BlockSpec((tm, tn), lambda i, j: (i, j)),
        out_shape=jax.ShapeDtypeStruct((n, c), F32),
        compiler_params=_params(("parallel", "parallel"), 48),
    )(a, w, x)


def _topk_rows(s, k, payload=None):
    nrow, t = s.shape
    rid = lax.broadcasted_iota(jnp.int32, (nrow, t), 0).astype(F32)
    kid = lax.broadcasted_iota(jnp.int32, (k, t), 0)
    vals = jnp.zeros((k, t), F32)
    sel = jnp.zeros((k, t), F32)
    for j in range(k):
        m = jnp.max(s, axis=0, keepdims=True)
        pos = jnp.min(jnp.where(s == m, rid, float(nrow)), axis=0, keepdims=True)
        hit = rid == pos
        if payload is None:
            picked = pos
        else:
            picked = jnp.max(jnp.where(hit, payload, -1.0), axis=0, keepdims=True)
        vals = jnp.where(kid == j, m, vals)
        sel = jnp.where(kid == j, picked, sel)
        s = jnp.where(hit, -jnp.inf, s)
    return vals, sel


def _router_body(q_ref, keys_ref, a_ref, b_ref, gw_ref):
    K = PEER_TOPK
    a_parts, b_parts, w_parts = [], [], []
    for h in range(PEER_HEADS):
        sv, si = [], []
        for p in range(2):
            c0 = (h * 2 + p) * PEER_HALF
            st = _dot3(keys_ref[h * 2 + p], q_ref[:, c0:c0 + PEER_HALF], NT)
            vals, idx = _topk_rows(st, K)
            sv.append(vals)
            si.append(idx)
        jid = lax.broadcasted_iota(jnp.int32, (SUBLANES, sv[0].shape[1]), 0)
        c_parts = [sv[0][0:1, :] + sv[1]]
        i_parts = [si[0][0:1, :] * float(PEER_NKEYS) + si[1]]
        for i in range(1, SUBLANES):
            keep = jid < K // (i + 1)
            c_parts.append(jnp.where(keep, sv[0][i:i + 1, :] + sv[1][0:SUBLANES, :], -jnp.inf))
            i_parts.append(si[0][i:i + 1, :] * float(PEER_NKEYS) + si[1][0:SUBLANES, :])
        c_parts.append(sv[0][SUBLANES:K, :] + sv[1][0:1, :])
        i_parts.append(si[0][SUBLANES:K, :] * float(PEER_NKEYS) + si[1][0:1, :])
        cand = jnp.concatenate(c_parts, axis=0)
        cidx = jnp.concatenate(i_parts, axis=0)
        best, eidx = _topk_rows(cand, K, payload=cidx)
        ex = jnp.exp(best - best[0:1, :])
        w_parts.append(ex / jnp.sum(ex, axis=0, keepdims=True))
        hi = jnp.floor(eidx * (1.0 / PEER_NKEYS))
        a_parts.append(hi)
        b_parts.append(eidx - hi * float(PEER_NKEYS))
    a_ref[...] = jnp.concatenate(a_parts, axis=0).T
    b_ref[...] = jnp.concatenate(b_parts, axis=0).T
    gw_ref[...] = jnp.concatenate(w_parts, axis=0).T


def _router(q, keys):
    n = q.shape[0]
    tq = min(256, n)
    tok = pl.BlockSpec((tq, PEER_HK), lambda i: (i, 0))
    sds = jax.ShapeDtypeStruct((n, PEER_HK), F32)
    return pl.pallas_call(
        _router_body,
        grid=(n // tq,),
        in_specs=[pl.BlockSpec((tq, q.shape[1]), lambda i: (i, 0)),
                  pl.BlockSpec(keys.shape, lambda i: (0, 0, 0))],
        out_specs=[tok, tok, tok],
        out_shape=[sds, sds, sds],
        compiler_params=_params(("parallel",), 32),
    )(q, keys)


PEER_CI = 8
PEER_CE = PEER_CI * PEER_NKEYS


PEER_SPLIT = 1
PEER_KEYS_PER_CALL = PEER_NKEYS // PEER_SPLIT
HI16 = 0xFFFF0000


def _peer_body(x_ref, h_ref, a_ref, b_ref, w_ref, ut_ref, v_ref, gf_ref, o_ref, g2_ref, z_ref, *,
               pitch, key0, final_norm):
    j = pl.program_id(1)
    last = pl.num_programs(1) - 1
    t = x_ref.shape[0]
    half = t // 2
    cur = j % 2

    def produce():
        z_ref[cur] = _mm(h_ref[...], ut_ref[...])

    def consume():
        z = z_ref[1 - cur]
        parts = []
        for ii in range(PEER_CI):
            start = pl.multiple_of(((j - 1) * PEER_CI + ii) * pitch, SUBLANES)
            words = g2_ref[pl.ds(start, half), :]
            g_first = lax.bitcast_convert_type(words << 16, F32)
            g_second = lax.bitcast_convert_type(words & jnp.uint32(HI16), F32)
            zz = z[:, ii * PEER_NKEYS:(ii + 1) * PEER_NKEYS]
            act = 0.5 * zz * (1.0 + lax.erf(zz * (2.0 ** -0.5)))
            parts.append(jnp.concatenate([act[:half] * g_first, act[half:] * g_second], axis=0).astype(BF16))
        o_ref[...] += _mm(jnp.concatenate(parts, axis=1), v_ref[...])

    @pl.when(j == 0)
    def _():
        o_ref[...] = x_ref[...]
        kid = lax.broadcasted_iota(jnp.int32, (PEER_KEYS_PER_CALL, PEER_HK), 0).astype(F32) + float(key0)
        kid_b = lax.broadcasted_iota(jnp.int32, (PEER_NKEYS, PEER_HK), 0).astype(F32)

        def g_bits(ti):
            arow = a_ref[pl.ds(ti, 1), :]
            brow = b_ref[pl.ds(ti, 1), :]
            wrow = w_ref[pl.ds(ti, 1), :]
            lhs = jnp.where(kid == arow, wrow, 0.0).astype(BF16)
            rhs = jnp.where(kid_b == brow, 1.0, 0.0).astype(BF16)
            g_t = _mm(lhs, rhs, NT)
            return lax.bitcast_convert_type(g_t.astype(BF16).astype(F32), jnp.uint32)

        def per_pair(ti, carry):
            words = (g_bits(ti) >> 16) | g_bits(ti + half)
            g2_ref[pl.ds(ti, PEER_KEYS_PER_CALL, stride=pitch), :] = words
            return carry

        lax.fori_loop(0, half, per_pair, 0, unroll=16)
        produce()

    @pl.when(jnp.logical_and(j > 0, j < last))
    def _():
        produce()
        consume()

    @pl.when(j == last)
    def _():
        consume()

    if final_norm:
        @pl.when(j == last)
        def _():
            y = o_ref[...]
            o_ref[...] = y * lax.rsqrt(jnp.mean(y * y, axis=-1, keepdims=True) + RMS_EPS) * gf_ref[...]


def _peer(x, h, a_idx, b_idx, gw, u_t, v_tab, g_final, final_norm):
    n, d = x.shape
    t = min(512, n)
    pitch = t // 2 + SUBLANES
    nch = PEER_N // PEER_CE // PEER_SPLIT
    once = pl.Buffered(1)
    tok = lambda c: pl.BlockSpec((t, c), lambda i, j: (i, 0), pipeline_mode=once)
    for part in range(PEER_SPLIT):
        c0 = part * nch
        is_last = part == PEER_SPLIT - 1
        x = pl.pallas_call(
            functools.partial(_peer_body, pitch=pitch, key0=part * PEER_KEYS_PER_CALL,
                              final_norm=final_norm and is_last),
            grid=(n // t, nch + 1),
            in_specs=[tok(d), tok(d), tok(PEER_HK), tok(PEER_HK), tok(PEER_HK),
                      pl.BlockSpec((d, PEER_CE), lambda i, j, c0=c0: (0, c0 + jnp.minimum(j, nch - 1))),
                      pl.BlockSpec((PEER_CE, d), lambda i, j, c0=c0: (c0 + jnp.maximum(j - 1, 0), 0)),
                      pl.BlockSpec((1, d), lambda i, j: (0, 0))],
            out_specs=pl.BlockSpec((t, d), lambda i, j: (i, 0)),
            out_shape=jax.ShapeDtypeStruct((n, d), F32),
            scratch_shapes=[pltpu.VMEM((PEER_KEYS_PER_CALL * pitch, LANES), jnp.uint32),
                            pltpu.VMEM((2, t, PEER_CE), F32)],
            compiler_params=_params(("parallel", "arbitrary"), 56),
        )(x, h, a_idx, b_idx, gw, u_t, v_tab, g_final)
    return x


def _pad_cols(pieces, width):
    rows = pieces[0][0].shape[0]
    out = jnp.zeros((rows, width), pieces[0][0].dtype)
    for arr, off in pieces:
        out = out.at[:, off:off + arr.shape[1]].set(arr)
    return out


def _pad_rows(x, rows):
    return jnp.zeros((rows,) + x.shape[1:], x.dtype).at[:x.shape[0]].set(x)


def _lora_layout(x):
    o = 3 * RW_W
    return _pad_cols([(x[:, o:o + 64], 0), (x[:, o + 64:o + 128], 128), (x[:, o + 128:o + 288], 256)], 512)


def _prep_layer(l, w):
    f = {}
    w_in = w["w_in"][l]
    rw = w_in[:, :RW_COLS]
    ml = w_in[:, RW_COLS:RW_COLS + ML_COLS]
    gt = w_in[:, RW_COLS + ML_COLS:]
    f["norm_mix"] = w["norm_mix"][l][None, :]
    f["w_f"] = _pad_cols([(rw[:, :3 * RW_W], 0), (_lora_layout(rw), PF_LORA_OFF),
                          (ml[:, 2 * ML_QKW + 2 * ML_VW:], PF_MLG_OFF)], PF_COLS).astype(BF16)
    f["w_b"] = jnp.concatenate([ml[:, :2 * ML_QKW + 2 * ML_VW], gt], axis=1).astype(BF16)
    mu = jnp.stack([w["rw_mu_prev"][l], w["rw_mu_next"][l]])
    f["mu_main"] = mu[:, :3 * RW_W]
    f["mu_lora"] = _lora_layout(mu)
    f["w0"] = w["rw_w0"][l]
    f["w2p"] = jnp.stack([_pad_rows(w["rw_w2"][l, 0], 128), _pad_rows(w["rw_w2"][l, 1], 128)]).astype(BF16)
    f["a0"] = w["rw_a0"][l][None, :]
    f["a2p"] = _pad_rows(w["rw_a2"][l], 128).astype(BF16)
    f["g2p"] = _pad_rows(w["rw_g2"][l], 256).astype(BF16)
    f["k_k"] = w["rw_k_k"][l][None, :]
    f["k_a"] = w["rw_k_a"][l][None, :]
    f["r_k"] = w["rw_r_k"][l].reshape(1, RW_W)
    f["ln_w"] = w["rw_ln_w"][l][None, :]
    f["ln_b"] = w["rw_ln_b"][l][None, :]
    f["ml_bias"] = _pad_cols([(w["ml_b_i"][l].reshape(1, -1), 0), (w["ml_b_f"][l].reshape(1, -1), 2 * ML_HEADS)], LANES)
    f["ml_norm_w"] = w["ml_norm_w"][l][None, :]
    f["p_a"] = w["p_a"][l].astype(BF16)
    f["p_b"] = w["p_b"][l].astype(BF16)
    f["b_gate"] = w["b_gate"][l][None, :]
    f["w_out"] = w["w_out"][l].astype(BF16)
    f["norm_ffn"] = w["norm_ffn"][l][None, :]
    f["wq"] = (w["peer_wq"][l].astype(BF16),)
    f["keys"] = w["peer_keys"][l].reshape(PEER_HEADS * 2, PEER_NKEYS, PEER_HALF)
    f["u_t"] = w["peer_u"][l].astype(BF16).T
    f["v"] = w["peer_v"][l].astype(BF16)
    return f


def _head_selectors():
    ch = jnp.arange(RW_W)[:, None] // RW_HEAD
    e = (ch == jnp.arange(LANES)[None, :]).astype(BF16)
    return e, e.T


def _layer(x, f, e, et, g_final, final_norm):
    b, t, d = x.shape
    n = b * t
    x2 = x.reshape(n, d)
    (hn,) = _norm_cast(x2, f["norm_mix"], 1)
    p_f = _matmul((hn,), (f["w_f"],), F32)
    p_b = _matmul((hn,), (f["w_b"],), BF16)
    p_f3 = p_f.reshape(b, t, -1)
    p_b3 = p_b.reshape(b, t, -1)

    r, k, v, kn, ab, lw0, lw1, gg, bonus = _rw_prep(
        p_f3, f["mu_main"], f["mu_lora"], f["w0"], f["w2p"], f["a0"], f["a2p"], f["g2p"], f["k_k"], f["k_a"],
        f["r_k"], e, et)
    y_f = _rw_scan(r, k, v, kn, ab, lw0, rev=False)
    y_b = _rw_scan(r, k, v, kn, ab, lw1, rev=True)

    h_f = _ml_scan(p_b3, p_f3, f["ml_bias"], rev=False)
    h_b = _ml_scan(p_b3, p_f3, f["ml_bias"], rev=True)

    ya, yb = _post(y_f.reshape(n, -1), y_b.reshape(n, -1), bonus.reshape(n, -1), gg.reshape(n, -1),
                   f["ln_w"], f["ln_b"], e, et, h_f.reshape(n, -1), h_b.reshape(n, -1), p_b, f["ml_norm_w"])
    merged = _merge(ya, yb, f["p_a"], f["p_b"], p_b, f["b_gate"])
    x1 = _resid_mm(merged, f["w_out"], x2)

    h_parts = _norm_cast(x1, f["norm_ffn"], 1)
    q = _matmul(h_parts, f["wq"], F32)
    a_idx, b_idx, gw = _router(q, f["keys"])
    x_out = _peer(x1, h_parts[0], a_idx, b_idx, gw, f["u_t"], f["v"], g_final, final_norm)
    return x_out.reshape(b, t, d)


def _trunk(x, layers, e, et, g_final):
    for l, f in enumerate(layers):
        x = _layer(x, f, e, et, g_final, final_norm=(l == len(layers) - 1))
    return x


def kernel(x_prompt, x_sample, norm_mix, w_in, rw_mu_prev, rw_mu_next, rw_w0, rw_w2, rw_a0, rw_a2, rw_g2, rw_k_k, rw_k_a, rw_r_k, rw_ln_w, rw_ln_b, ml_b_i, ml_b_f, ml_norm_w, p_a, p_b, b_gate, w_out, norm_ffn, peer_wq, peer_keys, peer_u, peer_v, norm_final):
    w = dict(norm_mix=norm_mix, w_in=w_in, rw_mu_prev=rw_mu_prev, rw_mu_next=rw_mu_next, rw_w0=rw_w0, rw_w2=rw_w2,
             rw_a0=rw_a0, rw_a2=rw_a2, rw_g2=rw_g2, rw_k_k=rw_k_k, rw_k_a=rw_k_a, rw_r_k=rw_r_k, rw_ln_w=rw_ln_w,
             rw_ln_b=rw_ln_b, ml_b_i=ml_b_i, ml_b_f=ml_b_f, ml_norm_w=ml_norm_w, p_a=p_a, p_b=p_b, b_gate=b_gate,
             w_out=w_out, norm_ffn=norm_ffn, peer_wq=peer_wq, peer_keys=peer_keys, peer_u=peer_u, peer_v=peer_v)
    depth = w_in.shape[0]
    layers = [_prep_layer(l, w) for l in range(depth)]
    e, et = _head_selectors()
    g_final = norm_final[None, :]
    return (_trunk(x_prompt, layers, e, et, g_final), _trunk(x_sample, layers, e, et, g_final))
```

```python
import functools
import math

import jax
import jax.numpy as jnp
from jax import lax
from jax.experimental import pallas as pl
from jax.experimental.pallas import tpu as pltpu

F32 = jnp.float32
BF16 = jnp.bfloat16

D_MODEL = 2048
RMS_EPS = 1e-6
RW_W = 1024
RW_HEAD = 64
RW_HEADS = 16
RW_DECAY_LORA = 64
RW_A_LORA = 64
RW_G_LORA = 160
RW_GN_EPS = 64e-5
RW_COLS = 3 * RW_W + RW_DECAY_LORA + RW_A_LORA + RW_G_LORA
RW_CHUNK = 64
RW_SUB = 4
RW_PASSES_A = 1
RW_PASSES_INV = 1
RW_PASSES_S = 1
ML_HEADS = 4
ML_QK = 128
ML_V = 256
ML_QKW = ML_HEADS * ML_QK
ML_VW = ML_HEADS * ML_V
ML_CHUNK = 128
ML_NORM_EPS = 1e-6
ML_COLS = 2 * ML_QKW + 2 * ML_VW + 4 * ML_HEADS
PF_LORA_OFF = 3 * RW_W
PF_LORA_W = 512
PF_MLG_OFF = PF_LORA_OFF + PF_LORA_W
PF_COLS = 4096
PB_GATE_OFF = 2 * ML_QKW + 2 * ML_VW
PB_COLS = PB_GATE_OFF + 2 * D_MODEL
PEER_HEADS = 8
PEER_NKEYS = 128
PEER_N = PEER_NKEYS * PEER_NKEYS
PEER_HALF = 128
PEER_TOPK = 16
PEER_HK = PEER_HEADS * PEER_TOPK
LANES = 128
SUBLANES = 8

NN = ((1,), (0,))
NT = ((1,), (1,))


def _mm(a, b, dims=NN):
    return lax.dot_general(a, b, (dims, ((), ())), preferred_element_type=F32)


def _split2(a):
    hi = a.astype(BF16)
    lo = (a - hi.astype(F32)).astype(BF16)
    return hi, lo


def _split3(a):
    hi = a.astype(BF16)
    r1 = a - hi.astype(F32)
    mid = r1.astype(BF16)
    lo = (r1 - mid.astype(F32)).astype(BF16)
    return hi, mid, lo


def _dot1(a, b, dims=NN):
    return _mm(a.astype(BF16), b.astype(BF16), dims)


def _dot3(a, b, dims=NN):
    ah, al = _split2(a)
    bh, bl = _split2(b)
    return _mm(ah, bh, dims) + (_mm(ah, bl, dims) + _mm(al, bh, dims))


def _dotp(a, b, dims, passes):
    return _dot3(a, b, dims) if passes == 3 else _dot1(a, b, dims)


def _dot_sel(sel_bf16, x, dims=NN):
    h, m, l = _split3(x)
    return _mm(sel_bf16, h, dims) + (_mm(sel_bf16, m, dims) + _mm(sel_bf16, l, dims))


def _dot_xsel(x, sel_bf16, dims=NN):
    h, m, l = _split3(x)
    return _mm(h, sel_bf16, dims) + (_mm(m, sel_bf16, dims) + _mm(l, sel_bf16, dims))


def _params(sem, vmem_mb):
    return pltpu.CompilerParams(dimension_semantics=sem, vmem_limit_bytes=vmem_mb << 20)


def _sigmoid(x):
    return 1.0 / (1.0 + jnp.exp(-x))


def _norm_cast_body(x_ref, g_ref, *o_refs):
    xf = x_ref[...]
    h = xf * lax.rsqrt(jnp.mean(xf * xf, axis=-1, keepdims=True) + RMS_EPS) * g_ref[...]
    hi = h.astype(BF16)
    o_refs[0][...] = hi
    if len(o_refs) == 2:
        o_refs[1][...] = (h - hi.astype(F32)).astype(BF16)


def _norm_cast(x, g, n_parts):
    n, d = x.shape
    tm = min(512, n)
    spec = pl.BlockSpec((tm, d), lambda i: (i, 0))
    return pl.pallas_call(
        _norm_cast_body,
        grid=(n // tm,),
        in_specs=[spec, pl.BlockSpec((1, d), lambda i: (0, 0))],
        out_specs=[spec] * n_parts,
        out_shape=[jax.ShapeDtypeStruct((n, d), BF16)] * n_parts,
        compiler_params=_params(("parallel",), 32),
    )(x, g)


def _matmul_body(*refs, n_parts):
    a_refs = refs[:n_parts]
    w_refs = refs[n_parts:2 * n_parts]
    o_ref = refs[2 * n_parts]
    acc = _mm(a_refs[0][...], w_refs[0][...])
    if n_parts == 2:
        acc = acc + (_mm(a_refs[0][...], w_refs[1][...]) + _mm(a_refs[1][...], w_refs[0][...]))
    o_ref[...] = acc.astype(o_ref.dtype)


def _matmul(a_parts, w_parts, out_dtype):
    n, d = a_parts[0].shape
    c = w_parts[0].shape[1]
    n_parts = len(a_parts)
    tm = min(1024, n)
    tn = min(1024 // n_parts, c)
    return pl.pallas_call(
        functools.partial(_matmul_body, n_parts=n_parts),
        grid=(n // tm, c // tn),
        in_specs=[pl.BlockSpec((tm, d), lambda i, j: (i, 0))] * n_parts
        + [pl.BlockSpec((d, tn), lambda i, j: (0, j))] * n_parts,
        out_specs=pl.BlockSpec((tm, tn), lambda i, j: (i, j)),
        out_shape=jax.ShapeDtypeStruct((n, c), out_dtype),
        compiler_params=_params(("parallel", "parallel"), 48),
    )(*a_parts, *w_parts)


def _shift_rows(p, prev_row, next_row, mu_prev, mu_next):
    n = p.shape[0]
    rid = lax.broadcasted_iota(jnp.int32, p.shape, 0)
    prev = jnp.where(rid == 0, prev_row, pltpu.roll(p, 1, 0))
    nxt = jnp.where(rid == n - 1, next_row, pltpu.roll(p, n - 1, 0))
    return p + mu_prev * (prev - p) + mu_next * (nxt - p)


def _rw_prep_body(m_ref, mp_ref, mn_ref, l_ref, lp_ref, ln_ref, mum_ref, mul_ref, w0_ref, w2_ref,
                  a0_ref, a2_ref, g2_ref, kk_ref, ka_ref, rk_ref, e_ref, et_ref,
                  r_o, k_o, v_o, kn_o, ab_o, lw0_o, lw1_o, g_o, bo_o):
    i = pl.program_id(1)
    first = i == 0
    last = i == pl.num_programs(1) - 1
    zero_m = jnp.zeros((1, m_ref.shape[2]), F32)
    zero_l = jnp.zeros((1, l_ref.shape[2]), F32)
    pm = _shift_rows(m_ref[0],
                     jnp.where(first, zero_m, mp_ref[0, SUBLANES - 1:SUBLANES, :]),
                     jnp.where(last, zero_m, mn_ref[0, 0:1, :]),
                     mum_ref[0:1, :], mum_ref[1:2, :])
    plo = _shift_rows(l_ref[0],
                      jnp.where(first, zero_l, lp_ref[0, SUBLANES - 1:SUBLANES, :]),
                      jnp.where(last, zero_l, ln_ref[0, 0:1, :]),
                      mul_ref[0:1, :], mul_ref[1:2, :])
    r = pm[:, 0:RW_W]
    k = pm[:, RW_W:2 * RW_W]
    v = pm[:, 2 * RW_W:3 * RW_W]
    wd = plo[:, 0:128]
    ad = plo[:, 128:256]
    gd = plo[:, 256:512]

    a = _sigmoid(a0_ref[...] + _dot1(ad, a2_ref[...]))
    g = _dot1(_sigmoid(gd), g2_ref[...])
    e = e_ref[...]
    et = et_ref[...]
    kk = k * kk_ref[...]
    ss = _dot1(_dot1(kk * kk, e), et)
    kn = kk / jnp.maximum(jnp.sqrt(ss), 1e-12)
    kmod = k * (1.0 + (a - 1.0) * ka_ref[...])
    wl = jnp.tanh(wd)
    c = math.exp(-0.5)
    lw0 = -_sigmoid(w0_ref[0:1, :] + _dot1(wl, w2_ref[0])) * c
    lw1 = -_sigmoid(w0_ref[1:2, :] + _dot1(wl, w2_ref[1])) * c
    rks = _dot1(_dot1(r * kmod * rk_ref[...], e), et)
    r_o[0] = r.astype(r_o.dtype)
    k_o[0] = kmod.astype(k_o.dtype)
    v_o[0] = v.astype(v_o.dtype)
    kn_o[0] = kn.astype(kn_o.dtype)
    ab_o[0] = (kn * a).astype(ab_o.dtype)
    lw0_o[0] = lw0
    lw1_o[0] = lw1
    g_o[0] = g.astype(g_o.dtype)
    bo_o[0] = (rks * v).astype(bo_o.dtype)


def _rw_prep(p_f, mu_main, mu_lora, w0, w2p, a0, a2p, g2p, k_k, k_a, r_k, e, et):
    b, t, _ = p_f.shape
    cm, cl = 3 * RW_W, PF_LORA_W
    col = {cm: 0, cl: PF_LORA_OFF // PF_LORA_W}
    tt = min(128, t)
    nb = tt // SUBLANES
    nt8 = t // SUBLANES

    def main_spec(c):
        return pl.BlockSpec((1, tt, c), lambda bi, i: (bi, i, col[c]))

    def prev_spec(c):
        return pl.BlockSpec((1, SUBLANES, c), lambda bi, i: (bi, jnp.maximum(i * nb - 1, 0), col[c]))

    def next_spec(c):
        return pl.BlockSpec((1, SUBLANES, c), lambda bi, i: (bi, jnp.minimum((i + 1) * nb, nt8 - 1), col[c]))

    def full(x):
        nd = x.ndim
        return pl.BlockSpec(x.shape, lambda bi, i: (0,) * nd)

    consts = (mu_main, mu_lora, w0, w2p, a0, a2p, g2p, k_k, k_a, r_k, e, et)
    out_dtypes = [BF16] * 5 + [F32, F32, BF16, BF16]
    return pl.pallas_call(
        _rw_prep_body,
        grid=(b, t // tt),
        in_specs=[main_spec(cm), prev_spec(cm), next_spec(cm), main_spec(cl), prev_spec(cl), next_spec(cl)]
        + [full(x) for x in consts],
        out_specs=[pl.BlockSpec((1, tt, RW_W), lambda bi, i: (bi, i, 0))] * 9,
        out_shape=[jax.ShapeDtypeStruct((b, t, RW_W), dt) for dt in out_dtypes],
        compiler_params=_params(("parallel", "parallel"), 56),
    )(p_f, p_f, p_f, p_f, p_f, p_f, *consts)


def _rw_scan_body(r_ref, k_ref, v_ref, kn_ref, ab_ref, lw_ref, y_ref, s_ref, *, rev):
    L = RW_CHUNK

    @pl.when(pl.program_id(1) == 0)
    def _():
        s_ref[...] = jnp.zeros_like(s_ref)

    row = lax.broadcasted_iota(jnp.int32, (L, L), 0)
    col = lax.broadcasted_iota(jnp.int32, (L, L), 1)
    if rev:
        incl = col >= row
        strict = col > row
    else:
        incl = col <= row
        strict = col < row
    tri = jnp.where(incl, 1.0, 0.0).astype(BF16)
    eye = jnp.where(row == col, 1.0, 0.0).astype(F32)
    lane = lax.broadcasted_iota(jnp.int32, (L, LANES), 1)
    head_masks = (lane < RW_HEAD, lane >= RW_HEAD)
    sr = lax.broadcasted_iota(jnp.int32, (LANES, LANES), 0)
    sc = lax.broadcasted_iota(jnp.int32, (LANES, LANES), 1)
    bd = (sr < RW_HEAD) == (sc < RW_HEAD)
    r4 = lax.broadcasted_iota(jnp.int32, (4 * L, 2 * L), 0)
    c4 = lax.broadcasted_iota(jnp.int32, (4 * L, 2 * L), 1) & (L - 1)
    t4 = r4 & (L - 1)
    own = jnp.where((r4 & L) != 0, 1, 0)
    causal4 = (c4 > t4 - own) if rev else (c4 < t4 + own)

    heads =[(hp, h) for hp in range(RW_HEADS // 2) for h in range(2)]
    sls = [slice(hp * LANES, (hp + 1) * LANES) for hp in range(RW_HEADS // 2)]

    def intra(offs):
        pre = []
        for off in offs:
            rows = slice(off, off + L)
            lw = lw_ref[0, rows, :]
            cum = _dot_sel(tri, lw)
            tot = cum[0:1, :] if rev else cum[L - 1:L, :]
            p_inv = jnp.exp(-cum)
            p_end = jnp.exp(tot - cum)
            ab = ab_ref[0, rows, :].astype(F32)
            k = k_ref[0, rows, :].astype(F32)
            pre.append(dict(at=-kn_ref[0, rows, :].astype(F32) * jnp.exp(cum - lw),
                            rt=r_ref[0, rows, :].astype(F32) * jnp.exp(cum), bt=ab * p_inv, kt=k * p_inv,
                            v=v_ref[0, rows, :].astype(F32), bt_end=ab * p_end, kt_end=k * p_end,
                            p_all=jnp.exp(tot)))
        zeros_lv = jnp.zeros((L, LANES), F32)
        nck = len(offs)
        at_h = [[] for _ in range(nck)]
        blks = [[] for _ in range(nck)]
        for ci, c in enumerate(pre):
            for hp in range(RW_HEADS // 2):
                sl = sls[hp]
                rows4 = []
                for h in range(2):
                    at_m = jnp.where(head_masks[h], c["at"][:, sl], 0.0)
                    at_h[ci].append(at_m)
                    rows4 += [at_m, jnp.where(head_masks[h], c["rt"][:, sl], 0.0)]
                blk = _dotp(jnp.concatenate(rows4, axis=0), jnp.concatenate([c["bt"][:, sl], c["kt"][:, sl]], axis=0),
                            NT, RW_PASSES_A)
                blks[ci].append(jnp.where(causal4, blk, 0.0))
        a_ab = [[blks[ci][hp][2 * h * L:(2 * h + 1) * L, :L] for hp, h in heads] for ci in range(nck)]
        a_rb = [[blks[ci][hp][(2 * h + 1) * L:(2 * h + 2) * L, :L] for hp, h in heads] for ci in range(nck)]
        av_pair = [[_dotp(blks[ci][hp], jnp.concatenate([zeros_lv, pre[ci]["v"][:, sls[hp]]], axis=0), NN, RW_PASSES_A)
                    for hp in range(RW_HEADS // 2)] for ci in range(nck)]
        av = [[av_pair[ci][hp][2 * h * L:(2 * h + 2) * L] for hp, h in heads] for ci in range(nck)]
        tm = [[eye + a for a in a_ab[ci]] for ci in range(nck)]
        pw = [[_dotp(a, a, NN, RW_PASSES_INV) for a in a_ab[ci]] for ci in range(nck)]
        for _ in range(4):
            st = [[_dotp(jnp.concatenate([t_, p], axis=0), p, NN, RW_PASSES_INV) for t_, p in zip(tm[ci], pw[ci])]
                  for ci in range(nck)]
            tm = [[t_ + s_[:L] for t_, s_ in zip(tm[ci], st[ci])] for ci in range(nck)]
            pw = [[s_[L:] for s_ in st[ci]] for ci in range(nck)]
        tm = [[t_ + _dotp(t_, p, NN, RW_PASSES_INV) for t_, p in zip(tm[ci], pw[ci])] for ci in range(nck)]
        out = []
        for ci, c in enumerate(pre):
            tx = [_dotp(tm[ci][i], jnp.concatenate([at_h[ci][i], jnp.where(head_masks[h], av[ci][i][:L], 0.0)], axis=1),
                        NN, RW_PASSES_A) for i, (hp, h) in enumerate(heads)]
            bk_t = [jnp.concatenate([c["bt_end"][:, sl], c["kt_end"][:, sl]], axis=0).T for sl in sls]
            p_col = [jnp.sum(jnp.where(sr == sc, jnp.broadcast_to(c["p_all"][:, sl], (LANES, LANES)), 0.0), axis=1,
                             keepdims=True) for sl in sls]
            out.append(dict(tx=tx, av=av[ci], a_rb=a_rb[ci], rt=c["rt"], v=c["v"], bk_t=bk_t, p_col=p_col))
        return out

    def sequential(c, off, states):
        tx, av, a_rb = c["tx"], c["av"], c["a_rb"]
        out = []
        for hp in range(RW_HEADS // 2):
            sl = sls[hp]
            i0, i1 = 2 * hp, 2 * hp + 1
            w_pair = tx[i0][:, :LANES] + tx[i1][:, :LANES]
            u0_pair = tx[i0][:, LANES:] + tx[i1][:, LANES:]
            y0_pair = jnp.where(head_masks[0], av[i0][L:], av[i1][L:])
            s = states[hp]
            ws = _dotp(jnp.concatenate([w_pair, c["rt"][:, sl]], axis=0), s, NN, RW_PASSES_S)
            u = ws[:L] + u0_pair
            y = ws[L:] + y0_pair + jnp.where(head_masks[0], _dotp(a_rb[i0], u, NN, RW_PASSES_A),
                                             _dotp(a_rb[i1], u, NN, RW_PASSES_A))
            upd = _dotp(c["bk_t"][hp], jnp.concatenate([u, c["v"][:, sl]], axis=0), NN, RW_PASSES_S)
            out.append(s * c["p_col"][hp] + jnp.where(bd, upd, 0.0))
            y_ref[0, off:off + L, sl] = y
        return out

    offs = [c * L for c in range(RW_SUB)]
    if rev:
        offs = offs[::-1]
    chunks = intra(offs)
    states = [s_ref[hp] for hp in range(RW_HEADS // 2)]
    for c, off in zip(chunks, offs):
        states = sequential(c, off, states)
    for hp in range(RW_HEADS // 2):
        s_ref[hp] = states[hp]


def _rw_scan(r, k, v, kn, ab, lw, rev):
    b, t, w = r.shape
    tb = RW_CHUNK * RW_SUB
    nc = t // tb
    if rev:
        idx = lambda bi, c: (bi, nc - 1 - c, 0)
    else:
        idx = lambda bi, c: (bi, c, 0)
    spec = pl.BlockSpec((1, tb, w), idx)
    return pl.pallas_call(
        functools.partial(_rw_scan_body, rev=rev),
        grid=(b, nc),
        in_specs=[spec] * 6,
        out_specs=spec,
        out_shape=jax.ShapeDtypeStruct((b, t, w), F32),
        scratch_shapes=[pltpu.VMEM((RW_HEADS // 2, LANES, LANES), F32)],
        compiler_params=_params(("parallel", "arbitrary"), 32),
    )(r, k, v, kn, ab, lw)


def _ml_scan_body(q_ref, k_ref, v_ref, g_ref, b_ref, h_ref, c_ref, n_ref, m_ref, *, rev, direction):
    L = ML_CHUNK

    @pl.when(pl.program_id(1) == 0)
    def _():
        c_ref[...] = jnp.zeros_like(c_ref)
        n_ref[...] = jnp.zeros_like(n_ref)
        m_ref[...] = jnp.zeros_like(m_ref)

    row = lax.broadcasted_iota(jnp.int32, (L, L), 0)
    col = lax.broadcasted_iota(jnp.int32, (L, L), 1)
    incl = (col >= row) if rev else (col <= row)
    tri = jnp.where(incl, 1.0, 0.0).astype(BF16)
    lane = lax.broadcasted_iota(jnp.int32, (L, LANES), 1)
    sub = lax.broadcasted_iota(jnp.int32, (LANES, L), 0)

    g = g_ref[0] + b_ref[...]
    ls = jnp.minimum(g, 0.0) - jnp.log(1.0 + jnp.exp(-jnp.abs(g)))
    bc = _dot_sel(tri, ls)
    g_t = g.T
    bc_t = bc.T
    scale = ML_QK ** -0.5
    hs = range(ML_HEADS)
    ci = [direction * ML_HEADS + h for h in hs]
    cf = [2 * ML_HEADS + direction * ML_HEADS + h for h in hs]
    b_col = [jnp.sum(jnp.where(lane == cf[h], bc, 0.0), axis=1, keepdims=True) for h in hs]
    i_col = [jnp.sum(jnp.where(lane == ci[h], g, 0.0), axis=1, keepdims=True) for h in hs]
    b_row = [jnp.sum(jnp.where(sub == cf[h], bc_t, 0.0), axis=0, keepdims=True) for h in hs]
    i_row = [jnp.sum(jnp.where(sub == ci[h], g_t, 0.0), axis=0, keepdims=True) for h in hs]
    m_prev = [jnp.max(m_ref[h:h + 1, :], axis=1, keepdims=True) for h in hs]
    qs = [q_ref[0, :, h * ML_QK:(h + 1) * ML_QK].astype(F32) * scale for h in hs]
    kc = [k_ref[0, :, h * ML_QK:(h + 1) * ML_QK].astype(F32) for h in hs]
    vc = [v_ref[0, :, h * ML_V:(h + 1) * ML_V].astype(F32) for h in hs]
    qk = [_dot1(qs[h], kc[h], NT) for h in hs]
    qc = [_dot1(qs[h], c_ref[h]) for h in hs]
    dmat = [jnp.where(incl, b_col[h] - b_row[h] + i_row[h], -jnp.inf) for h in hs]
    inter = [b_col[h] + m_prev[h] for h in hs]
    m_t = [jnp.maximum(inter[h], jnp.max(dmat[h], axis=1, keepdims=True)) for h in hs]
    s = [qk[h] * jnp.exp(dmat[h] - m_t[h]) for h in hs]
    e_inter = [jnp.exp(inter[h] - m_t[h]) for h in hs]
    num = [_dot1(s[h], vc[h]) + e_inter[h] * qc[h] for h in hs]
    den = [jnp.sum(s[h], axis=1, keepdims=True)
           + e_inter[h] * jnp.sum(qs[h] * n_ref[h:h + 1, :], axis=1, keepdims=True) for h in hs]
    for h in hs:
        h_ref[0, :, h * ML_V:(h + 1) * ML_V] = num[h] / jnp.maximum(jnp.abs(den[h]), jnp.exp(-m_t[h]))
    b_last = [jnp.min(b_row[h], axis=1, keepdims=True) for h in hs]
    g_row = [b_last[h] - b_row[h] + i_row[h] for h in hs]
    g_col = [b_last[h] - b_col[h] + i_col[h] for h in hs]
    m_new = [jnp.maximum(b_last[h] + m_prev[h], jnp.max(g_row[h], axis=1, keepdims=True)) for h in hs]
    w_col = [jnp.exp(g_col[h] - m_new[h]) for h in hs]
    dec = [jnp.exp(b_last[h] + m_prev[h] - m_new[h]) for h in hs]
    upd = [_dot1(kc[h].T, vc[h] * w_col[h]) for h in hs]
    for h in hs:
        c_ref[h] = dec[h] * c_ref[h] + upd[h]
        n_ref[h:h + 1, :] = dec[h] * n_ref[h:h + 1, :] + jnp.sum(kc[h] * w_col[h], axis=0, keepdims=True)
        m_ref[h:h + 1, :] = jnp.broadcast_to(m_new[h], (1, LANES))


def _ml_scan(p_b, p_f, bias, rev):
    b, t, _ = p_b.shape
    L = ML_CHUNK
    nc = t // L
    tix = (lambda c: nc - 1 - c) if rev else (lambda c: c)
    return pl.pallas_call(
        functools.partial(_ml_scan_body, rev=rev, direction=1 if rev else 0),
        grid=(b, nc),
        in_specs=[pl.BlockSpec((1, L, ML_QKW), lambda bi, c: (bi, tix(c), 0)),
                  pl.BlockSpec((1, L, ML_QKW), lambda bi, c: (bi, tix(c), 1)),
                  pl.BlockSpec((1, L, ML_VW), lambda bi, c: (bi, tix(c), 1)),
                  pl.BlockSpec((1, L, LANES), lambda bi, c: (bi, tix(c), PF_MLG_OFF // LANES)),
                  pl.BlockSpec((1, LANES), lambda bi, c: (0, 0))],
        out_specs=pl.BlockSpec((1, L, ML_VW), lambda bi, c: (bi, tix(c), 0)),
        out_shape=jax.ShapeDtypeStruct((b, t, ML_VW), F32),
        scratch_shapes=[pltpu.VMEM((ML_HEADS, ML_QK, ML_V), F32),
                        pltpu.VMEM((SUBLANES, LANES), F32),
                        pltpu.VMEM((SUBLANES, LANES), F32)],
        compiler_params=_params(("parallel", "arbitrary"), 32),
    )(p_b, p_b, p_b, p_f, bias)


def _post_body(yf_ref, yb_ref, bo_ref, g_ref, lnw_ref, lnb_ref, e_ref, et_ref,
               hf_ref, hb_ref, o_ref, nw_ref, ya_ref, yb_o_ref):
    e = e_ref[...]
    et = et_ref[...]
    y = yf_ref[...] + yb_ref[...]
    inv = 1.0 / RW_HEAD
    mu = _dot1(_dot1(y, e), et) * inv
    yc = y - mu
    var = _dot1(_dot1(yc * yc, e), et) * inv
    yn = yc * lax.rsqrt(var + RW_GN_EPS) * lnw_ref[...] + lnb_ref[...]
    ya_ref[...] = ((yn + bo_ref[...].astype(F32)) * g_ref[...].astype(F32)).astype(ya_ref.dtype)

    hsum = hf_ref[...] + hb_ref[...]
    parts = []
    for h in range(ML_HEADS):
        hh = hsum[:, h * ML_V:(h + 1) * ML_V]
        parts.append(hh * lax.rsqrt(jnp.mean(hh * hh, axis=-1, keepdims=True) + ML_NORM_EPS))
    hn = jnp.concatenate(parts, axis=1) * nw_ref[...]
    yb_o_ref[...] = (hn * _sigmoid(o_ref[...].astype(F32))).astype(yb_o_ref.dtype)


def _post(yf, yb, bonus, g, ln_w, ln_b, e, et, hf, hb, vo, norm_w):
    n = yf.shape[0]
    tm = min(256, n)
    tok = lambda c: pl.BlockSpec((tm, c), lambda i: (i, 0))
    const = lambda x: pl.BlockSpec(x.shape, lambda i: (0, 0))
    return pl.pallas_call(
        _post_body,
        grid=(n // tm,),
        in_specs=[tok(RW_W), tok(RW_W), tok(RW_W), tok(RW_W), const(ln_w), const(ln_b), const(e), const(et),
                  tok(ML_VW), tok(ML_VW), pl.BlockSpec((tm, ML_VW), lambda i: (i, (2 * ML_QKW + ML_VW) // ML_VW)),
                  const(norm_w)],
        out_specs=[tok(RW_W), tok(ML_VW)],
        out_shape=[jax.ShapeDtypeStruct((n, RW_W), BF16), jax.ShapeDtypeStruct((n, ML_VW), BF16)],
        compiler_params=_params(("parallel",), 40),
    )(yf, yb, bonus, g, ln_w, ln_b, e, et, hf, hb, vo, norm_w)


def _merge_body(ya_ref, yb_ref, pa_ref, pb_ref, ga_ref, gb_ref, ba_ref, bb_ref, o_ref):
    pa = _mm(ya_ref[...], pa_ref[...])
    pb = _mm(yb_ref[...], pb_ref[...])
    ga = ga_ref[...].astype(F32) + ba_ref[...]
    gb = gb_ref[...].astype(F32) + bb_ref[...]
    o_ref[...] = (_sigmoid(ga) * pa + _sigmoid(gb) * pb).astype(o_ref.dtype)


def _merge(ya, yb, p_a, p_b, proj_b, b_gate):
    n = ya.shape[0]
    tm = min(1024, n)
    tn = 1024
    nj = D_MODEL // tn
    g0 = PB_GATE_OFF // tn
    return pl.pallas_call(
        _merge_body,
        grid=(n // tm, nj),
        in_specs=[pl.BlockSpec((tm, RW_W), lambda i, j: (i, 0)), pl.BlockSpec((tm, ML_VW), lambda i, j: (i, 0)),
                  pl.BlockSpec((RW_W, tn), lambda i, j: (0, j)), pl.BlockSpec((ML_VW, tn), lambda i, j: (0, j)),
                  pl.BlockSpec((tm, tn), lambda i, j: (i, g0 + j)), pl.BlockSpec((tm, tn), lambda i, j: (i, g0 + nj + j)),
                  pl.BlockSpec((1, tn), lambda i, j: (0, j)), pl.BlockSpec((1, tn), lambda i, j: (0, j + nj))],
        out_specs=pl.BlockSpec((tm, tn), lambda i, j: (i, j)),
        out_shape=jax.ShapeDtypeStruct((n, D_MODEL), BF16),
        compiler_params=_params(("parallel", "parallel"), 40),
    )(ya, yb, p_a, p_b, proj_b, proj_b, b_gate, b_gate)


def _resid_mm_body(a_ref, w_ref, x_ref, o_ref):
    o_ref[...] = x_ref[...] + _mm(a_ref[...], w_ref[...])


def _resid_mm(a, w, x):
    n, kdim = a.shape
    c = w.shape[1]
    tm = min(1024, n)
    tn = 1024
    return pl.pallas_call(
        _resid_mm_body,
        grid=(n // tm, c // tn),
        in_specs=[pl.BlockSpec((tm, kdim), lambda i, j: (i, 0)), pl.BlockSpec((kdim, tn), lambda i, j: (0, j)),
                  pl.BlockSpec((tm, tn), lambda i, j: (i, j))],
        out_specs=pl.BlockSpec((tm, tn), lambda i, j: (i, j)),
        out_shape=jax.ShapeDtypeStruct((n, c), F32),
        compiler_params=_params(("parallel", "parallel"), 48),
    )(a, w, x)


def _topk_rows(s, k, payload=None):
    nrow, t = s.shape
    rid = lax.broadcasted_iota(jnp.int32, (nrow, t), 0).astype(F32)
    kid = lax.broadcasted_iota(jnp.int32, (k, t), 0)
    vals = jnp.zeros((k, t), F32)
    sel = jnp.zeros((k, t), F32)
    for j in range(k):
        m = jnp.max(s, axis=0, keepdims=True)
        pos = jnp.min(jnp.where(s == m, rid, float(nrow)), axis=0, keepdims=True)
        hit = rid == pos
        if payload is None:
            picked = pos
        else:
            picked = jnp.max(jnp.where(hit, payload, -1.0), axis=0, keepdims=True)
        vals = jnp.where(kid == j, m, vals)
        sel = jnp.where(kid == j, picked, sel)
        s = jnp.where(hit, -jnp.inf, s)
    return vals, sel


def _router_body(h_ref, wq_ref, keys_ref, a_ref, b_ref, gw_ref):
    K = PEER_TOPK
    hb = h_ref[...]
    qs = [_mm(hb, wq_ref[:, h * 2 * PEER_HALF:(h + 1) * 2 * PEER_HALF]) for h in range(PEER_HEADS)]
    a_parts, b_parts, w_parts = [], [], []
    for h in range(PEER_HEADS):
        sv, si = [], []
        for p in range(2):
            st = _dot3(keys_ref[h * 2 + p], qs[h][:, p * PEER_HALF:(p + 1) * PEER_HALF], NT)
            vals, idx = _topk_rows(st, K)
            sv.append(vals)
            si.append(idx)
        jid = lax.broadcasted_iota(jnp.int32, (SUBLANES, sv[0].shape[1]), 0)
        c_parts = [sv[0][0:1, :] + sv[1]]
        i_parts = [si[0][0:1, :] * float(PEER_NKEYS) + si[1]]
        for i in range(1, SUBLANES):
            keep = jid < K // (i + 1)
            c_parts.append(jnp.where(keep, sv[0][i:i + 1, :] + sv[1][0:SUBLANES, :], -jnp.inf))
            i_parts.append(si[0][i:i + 1, :] * float(PEER_NKEYS) + si[1][0:SUBLANES, :])
        c_parts.append(sv[0][SUBLANES:K, :] + sv[1][0:1, :])
        i_parts.append(si[0][SUBLANES:K, :] * float(PEER_NKEYS) + si[1][0:1, :])
        cand = jnp.concatenate(c_parts, axis=0)
        cidx = jnp.concatenate(i_parts, axis=0)
        best, eidx = _topk_rows(cand, K, payload=cidx)
        ex = jnp.exp(best - best[0:1, :])
        w_parts.append(ex / jnp.sum(ex, axis=0, keepdims=True))
        hi = jnp.floor(eidx * (1.0 / PEER_NKEYS))
        a_parts.append(hi)
        b_parts.append(eidx - hi * float(PEER_NKEYS))
    a_ref[...] = jnp.concatenate(a_parts, axis=0).T
    b_ref[...] = jnp.concatenate(b_parts, axis=0).T
    gw_ref[...] = jnp.concatenate(w_parts, axis=0).T


def _router(h, wq, keys):
    n, d = h.shape
    tq = min(256, n)
    tok = pl.BlockSpec((tq, PEER_HK), lambda i: (i, 0))
    sds = jax.ShapeDtypeStruct((n, PEER_HK), F32)
    return pl.pallas_call(
        _router_body,
        grid=(n // tq,),
        in_specs=[pl.BlockSpec((tq, d), lambda i: (i, 0)),
                  pl.BlockSpec(wq.shape, lambda i: (0, 0), pipeline_mode=pl.Buffered(1)),
                  pl.BlockSpec(keys.shape, lambda i: (0, 0, 0))],
        out_specs=[tok, tok, tok],
        out_shape=[sds, sds, sds],
        compiler_params=_params(("parallel",), 32),
    )(h, wq, keys)


PEER_CI = 8
PEER_CE = PEER_CI * PEER_NKEYS


PEER_SPLIT = 1
PEER_KEYS_PER_CALL = PEER_NKEYS // PEER_SPLIT
HI16 = 0xFFFF0000


def _peer_body(x_ref, h_ref, a_ref, b_ref, w_ref, ut_ref, v_ref, gf_ref, o_ref, g2_ref, z_ref, *,
               pitch, key0, final_norm):
    j = pl.program_id(1)
    last = pl.num_programs(1) - 1
    t = x_ref.shape[0]
    half = t // 2
    cur = j % 2

    def produce():
        z_ref[cur] = _mm(h_ref[...], ut_ref[...])

    def consume():
        z = z_ref[1 - cur]
        parts = []
        for ii in range(PEER_CI):
            start = pl.multiple_of(((j - 1) * PEER_CI + ii) * pitch, SUBLANES)
            words = g2_ref[pl.ds(start, half), :]
            g_first = lax.bitcast_convert_type(words << 16, F32)
            g_second = lax.bitcast_convert_type(words & jnp.uint32(HI16), F32)
            zz = z[:, ii * PEER_NKEYS:(ii + 1) * PEER_NKEYS]
            act = 0.5 * zz * (1.0 + lax.erf(zz * (2.0 ** -0.5)))
            parts.append(jnp.concatenate([act[:half] * g_first, act[half:] * g_second], axis=0).astype(BF16))
        o_ref[...] += _mm(jnp.concatenate(parts, axis=1), v_ref[...])

    @pl.when(j == 0)
    def _():
        o_ref[...] = x_ref[...]
        kid = lax.broadcasted_iota(jnp.int32, (PEER_KEYS_PER_CALL, PEER_HK), 0).astype(F32) + float(key0)
        kid_b = lax.broadcasted_iota(jnp.int32, (PEER_NKEYS, PEER_HK), 0).astype(F32)

        def g_bits(ti):
            arow = a_ref[pl.ds(ti, 1), :]
            brow = b_ref[pl.ds(ti, 1), :]
            wrow = w_ref[pl.ds(ti, 1), :]
            lhs = jnp.where(kid == arow, wrow, 0.0).astype(BF16)
            rhs = jnp.where(kid_b == brow, 1.0, 0.0).astype(BF16)
            g_t = _mm(lhs, rhs, NT)
            return lax.bitcast_convert_type(g_t.astype(BF16).astype(F32), jnp.uint32)

        def per_pair(ti, carry):
            words = (g_bits(ti) >> 16) | g_bits(ti + half)
            g2_ref[pl.ds(ti, PEER_KEYS_PER_CALL, stride=pitch), :] = words
            return carry

        lax.fori_loop(0, half, per_pair, 0, unroll=32)
        produce()

    @pl.when(jnp.logical_and(j > 0, j < last))
    def _():
        produce()
        consume()

    @pl.when(j == last)
    def _():
        consume()

    if final_norm:
        @pl.when(j == last)
        def _():
            y = o_ref[...]
            o_ref[...] = y * lax.rsqrt(jnp.mean(y * y, axis=-1, keepdims=True) + RMS_EPS) * gf_ref[...]


def _peer(x, h, a_idx, b_idx, gw, u_t, v_tab, g_final, final_norm):
    n, d = x.shape
    t = min(512, n)
    pitch = t // 2 + SUBLANES
    nch = PEER_N // PEER_CE // PEER_SPLIT
    once = pl.Buffered(1)
    tok = lambda c: pl.BlockSpec((t, c), lambda i, j: (i, 0), pipeline_mode=once)
    for part in range(PEER_SPLIT):
        c0 = part * nch
        is_last = part == PEER_SPLIT - 1
        x = pl.pallas_call(
            functools.partial(_peer_body, pitch=pitch, key0=part * PEER_KEYS_PER_CALL,
                              final_norm=final_norm and is_last),
            grid=(n // t, nch + 1),
            in_specs=[tok(d), tok(d), tok(PEER_HK), tok(PEER_HK), tok(PEER_HK),
                      pl.BlockSpec((d, PEER_CE), lambda i, j, c0=c0: (0, c0 + jnp.minimum(j, nch - 1))),
                      pl.BlockSpec((PEER_CE, d), lambda i, j, c0=c0: (c0 + jnp.maximum(j - 1, 0), 0)),
                      pl.BlockSpec((1, d), lambda i, j: (0, 0))],
            out_specs=pl.BlockSpec((t, d), lambda i, j: (i, 0)),
            out_shape=jax.ShapeDtypeStruct((n, d), F32),
            scratch_shapes=[pltpu.VMEM((PEER_KEYS_PER_CALL * pitch, LANES), jnp.uint32),
                            pltpu.VMEM((2, t, PEER_CE), F32)],
            compiler_params=_params(("parallel", "arbitrary"), 56),
        )(x, h, a_idx, b_idx, gw, u_t, v_tab, g_final)
    return x


def _pad_cols(pieces, width):
    rows = pieces[0][0].shape[0]
    out = jnp.zeros((rows, width), pieces[0][0].dtype)
    for arr, off in pieces:
        out = out.at[:, off:off + arr.shape[1]].set(arr)
    return out


def _pad_rows(x, rows):
    return jnp.zeros((rows,) + x.shape[1:], x.dtype).at[:x.shape[0]].set(x)


def _lora_layout(x):
    o = 3 * RW_W
    return _pad_cols([(x[:, o:o + 64], 0), (x[:, o + 64:o + 128], 128), (x[:, o + 128:o + 288], 256)], 512)


def _prep_layer(l, w):
    f = {}
    w_in = w["w_in"][l]
    rw = w_in[:, :RW_COLS]
    ml = w_in[:, RW_COLS:RW_COLS + ML_COLS]
    gt = w_in[:, RW_COLS + ML_COLS:]
    f["norm_mix"] = w["norm_mix"][l][None, :]
    f["w_f"] = _pad_cols([(rw[:, :3 * RW_W], 0), (_lora_layout(rw), PF_LORA_OFF),
                          (ml[:, 2 * ML_QKW + 2 * ML_VW:], PF_MLG_OFF)], PF_COLS).astype(BF16)
    f["w_b"] = jnp.concatenate([ml[:, :2 * ML_QKW + 2 * ML_VW], gt], axis=1).astype(BF16)
    mu = jnp.stack([w["rw_mu_prev"][l], w["rw_mu_next"][l]])
    f["mu_main"] = mu[:, :3 * RW_W]
    f["mu_lora"] = _lora_layout(mu)
    f["w0"] = w["rw_w0"][l]
    f["w2p"] = jnp.stack([_pad_rows(w["rw_w2"][l, 0], 128), _pad_rows(w["rw_w2"][l, 1], 128)]).astype(BF16)
    f["a0"] = w["rw_a0"][l][None, :]
    f["a2p"] = _pad_rows(w["rw_a2"][l], 128).astype(BF16)
    f["g2p"] = _pad_rows(w["rw_g2"][l], 256).astype(BF16)
    f["k_k"] = w["rw_k_k"][l][None, :]
    f["k_a"] = w["rw_k_a"][l][None, :]
    f["r_k"] = w["rw_r_k"][l].reshape(1, RW_W)
    f["ln_w"] = w["rw_ln_w"][l][None, :]
    f["ln_b"] = w["rw_ln_b"][l][None, :]
    f["ml_bias"] = _pad_cols([(w["ml_b_i"][l].reshape(1, -1), 0), (w["ml_b_f"][l].reshape(1, -1), 2 * ML_HEADS)], LANES)
    f["ml_norm_w"] = w["ml_norm_w"][l][None, :]
    f["p_a"] = w["p_a"][l].astype(BF16)
    f["p_b"] = w["p_b"][l].astype(BF16)
    f["b_gate"] = w["b_gate"][l][None, :]
    f["w_out"] = w["w_out"][l].astype(BF16)
    f["norm_ffn"] = w["norm_ffn"][l][None, :]
    f["wq"] = w["peer_wq"][l].astype(BF16)
    f["keys"] = w["peer_keys"][l].reshape(PEER_HEADS * 2, PEER_NKEYS, PEER_HALF)
    f["u_t"] = w["peer_u"][l].astype(BF16).T
    f["v"] = w["peer_v"][l].astype(BF16)
    return f


def _head_selectors():
    ch = jnp.arange(RW_W)[:, None] // RW_HEAD
    e = (ch == jnp.arange(LANES)[None, :]).astype(BF16)
    return e, e.T


def _layer(x, f, e, et, g_final, final_norm):
    b, t, d = x.shape
    n = b * t
    x2 = x.reshape(n, d)
    (hn,) = _norm_cast(x2, f["norm_mix"], 1)
    p_f = _matmul((hn,), (f["w_f"],), F32)
    p_b = _matmul((hn,), (f["w_b"],), BF16)
    p_f3 = p_f.reshape(b, t, -1)
    p_b3 = p_b.reshape(b, t, -1)

    r, k, v, kn, ab, lw0, lw1, gg, bonus = _rw_prep(
        p_f3, f["mu_main"], f["mu_lora"], f["w0"], f["w2p"], f["a0"], f["a2p"], f["g2p"], f["k_k"], f["k_a"],
        f["r_k"], e, et)
    y_f = _rw_scan(r, k, v, kn, ab, lw0, rev=False)
    y_b = _rw_scan(r, k, v, kn, ab, lw1, rev=True)

    h_f = _ml_scan(p_b3, p_f3, f["ml_bias"], rev=False)
    h_b = _ml_scan(p_b3, p_f3, f["ml_bias"], rev=True)

    ya, yb = _post(y_f.reshape(n, -1), y_b.reshape(n, -1), bonus.reshape(n, -1), gg.reshape(n, -1),
                   f["ln_w"], f["ln_b"], e, et, h_f.reshape(n, -1), h_b.reshape(n, -1), p_b, f["ml_norm_w"])
    merged = _merge(ya, yb, f["p_a"], f["p_b"], p_b, f["b_gate"])
    x1 = _resid_mm(merged, f["w_out"], x2)

    h_parts = _norm_cast(x1, f["norm_ffn"], 1)
    a_idx, b_idx, gw = _router(h_parts[0], f["wq"], f["keys"])
    x_out = _peer(x1, h_parts[0], a_idx, b_idx, gw, f["u_t"], f["v"], g_final, final_norm)
    return x_out.reshape(b, t, d)


def _trunk(x, layers, e, et, g_final):
    for l, f in enumerate(layers):
        x = _layer(x, f, e, et, g_final, final_norm=(l == len(layers) - 1))
    return x


def kernel(x_prompt, x_sample, norm_mix, w_in, rw_mu_prev, rw_mu_next, rw_w0, rw_w2, rw_a0, rw_a2, rw_g2, rw_k_k, rw_k_a, rw_r_k, rw_ln_w, rw_ln_b, ml_b_i, ml_b_f, ml_norm_w, p_a, p_b, b_gate, w_out, norm_ffn, peer_wq, peer_keys, peer_u, peer_v, norm_final):
    w = dict(norm_mix=norm_mix, w_in=w_in, rw_mu_prev=rw_mu_prev, rw_mu_next=rw_mu_next, rw_w0=rw_w0, rw_w2=rw_w2,
             rw_a0=rw_a0, rw_a2=rw_a2, rw_g2=rw_g2, rw_k_k=rw_k_k, rw_k_a=rw_k_a, rw_r_k=rw_r_k, rw_ln_w=rw_ln_w,
             rw_ln_b=rw_ln_b, ml_b_i=ml_b_i, ml_b_f=ml_b_f, ml_norm_w=ml_norm_w, p_a=p_a, p_b=p_b, b_gate=b_gate,
             w_out=w_out, norm_ffn=norm_ffn, peer_wq=peer_wq, peer_keys=peer_keys, peer_u=peer_u, peer_v=peer_v)
    depth = w_in.shape[0]
    layers = [_prep_layer(l, w) for l in range(depth)]
    e, et = _head_selectors()
    g_final = norm_final[None, :]
    return (_trunk(x_prompt, layers, e, et, g_final), _trunk(x_sample, layers, e, et, g_final))
```

```python
import functools
import math

import jax
import jax.numpy as jnp
from jax import lax
from jax.experimental import pallas as pl
from jax.experimental.pallas import tpu as pltpu

F32 = jnp.float32
BF16 = jnp.bfloat16

D_MODEL = 2048
RMS_EPS = 1e-6
RW_W = 1024
RW_HEAD = 64
RW_HEADS = 16
RW_DECAY_LORA = 64
RW_A_LORA = 64
RW_G_LORA = 160
RW_GN_EPS = 64e-5
RW_COLS = 3 * RW_W + RW_DECAY_LORA + RW_A_LORA + RW_G_LORA
RW_CHUNK = 64
RW_SUB = 4
RW_PASSES_A = 1
RW_PASSES_INV = 1
RW_PASSES_S = 1
ML_HEADS = 4
ML_QK = 128
ML_V = 256
ML_QKW = ML_HEADS * ML_QK
ML_VW = ML_HEADS * ML_V
ML_CHUNK = 128
ML_NORM_EPS = 1e-6
ML_COLS = 2 * ML_QKW + 2 * ML_VW + 4 * ML_HEADS
PF_LORA_OFF = 3 * RW_W
PF_LORA_W = 512
PF_MLG_OFF = PF_LORA_OFF + PF_LORA_W
PF_COLS = 4096
PB_GATE_OFF = 2 * ML_QKW + 2 * ML_VW
PB_COLS = PB_GATE_OFF + 2 * D_MODEL
PEER_HEADS = 8
PEER_NKEYS = 128
PEER_N = PEER_NKEYS * PEER_NKEYS
PEER_HALF = 128
PEER_TOPK = 16
PEER_HK = PEER_HEADS * PEER_TOPK
LANES = 128
SUBLANES = 8

NN = ((1,), (0,))
NT = ((1,), (1,))


def _mm(a, b, dims=NN):
    return lax.dot_general(a, b, (dims, ((), ())), preferred_element_type=F32)


def _split2(a):
    hi = a.astype(BF16)
    lo = (a - hi.astype(F32)).astype(BF16)
    return hi, lo


def _split3(a):
    hi = a.astype(BF16)
    r1 = a - hi.astype(F32)
    mid = r1.astype(BF16)
    lo = (r1 - mid.astype(F32)).astype(BF16)
    return hi, mid, lo


def _dot1(a, b, dims=NN):
    return _mm(a.astype(BF16), b.astype(BF16), dims)


def _dot3(a, b, dims=NN):
    ah, al = _split2(a)
    bh, bl = _split2(b)
    return _mm(ah, bh, dims) + (_mm(ah, bl, dims) + _mm(al, bh, dims))


def _dotp(a, b, dims, passes):
    return _dot3(a, b, dims) if passes == 3 else _dot1(a, b, dims)


def _dot_sel(sel_bf16, x, dims=NN):
    h, m, l = _split3(x)
    return _mm(sel_bf16, h, dims) + (_mm(sel_bf16, m, dims) + _mm(sel_bf16, l, dims))


def _dot_xsel(x, sel_bf16, dims=NN):
    h, m, l = _split3(x)
    return _mm(h, sel_bf16, dims) + (_mm(m, sel_bf16, dims) + _mm(l, sel_bf16, dims))


def _params(sem, vmem_mb):
    return pltpu.CompilerParams(dimension_semantics=sem, vmem_limit_bytes=vmem_mb << 20)


def _sigmoid(x):
    return 1.0 / (1.0 + jnp.exp(-x))


def _norm_cast_body(x_ref, g_ref, *o_refs):
    xf = x_ref[...]
    h = xf * lax.rsqrt(jnp.mean(xf * xf, axis=-1, keepdims=True) + RMS_EPS) * g_ref[...]
    hi = h.astype(BF16)
    o_refs[0][...] = hi
    if len(o_refs) == 2:
        o_refs[1][...] = (h - hi.astype(F32)).astype(BF16)


def _norm_cast(x, g, n_parts):
    n, d = x.shape
    tm = min(512, n)
    spec = pl.BlockSpec((tm, d), lambda i: (i, 0))
    return pl.pallas_call(
        _norm_cast_body,
        grid=(n // tm,),
        in_specs=[spec, pl.BlockSpec((1, d), lambda i: (0, 0))],
        out_specs=[spec] * n_parts,
        out_shape=[jax.ShapeDtypeStruct((n, d), BF16)] * n_parts,
        compiler_params=_params(("parallel",), 32),
    )(x, g)


def _matmul_body(*refs, n_parts):
    a_refs = refs[:n_parts]
    w_refs = refs[n_parts:2 * n_parts]
    o_ref = refs[2 * n_parts]
    acc = _mm(a_refs[0][...], w_refs[0][...])
    if n_parts == 2:
        acc = acc + (_mm(a_refs[0][...], w_refs[1][...]) + _mm(a_refs[1][...], w_refs[0][...]))
    o_ref[...] = acc.astype(o_ref.dtype)


def _matmul(a_parts, w_parts, out_dtype):
    n, d = a_parts[0].shape
    c = w_parts[0].shape[1]
    n_parts = len(a_parts)
    tm = min(1024, n)
    tn = min(1024 // n_parts, c)
    return pl.pallas_call(
        functools.partial(_matmul_body, n_parts=n_parts),
        grid=(n // tm, c // tn),
        in_specs=[pl.BlockSpec((tm, d), lambda i, j: (i, 0))] * n_parts
        + [pl.BlockSpec((d, tn), lambda i, j: (0, j))] * n_parts,
        out_specs=pl.BlockSpec((tm, tn), lambda i, j: (i, j)),
        out_shape=jax.ShapeDtypeStruct((n, c), out_dtype),
        compiler_params=_params(("parallel", "parallel"), 48),
    )(*a_parts, *w_parts)


def _shift_rows(p, prev_row, next_row, mu_prev, mu_next):
    n = p.shape[0]
    rid = lax.broadcasted_iota(jnp.int32, p.shape, 0)
    prev = jnp.where(rid == 0, prev_row, pltpu.roll(p, 1, 0))
    nxt = jnp.where(rid == n - 1, next_row, pltpu.roll(p, n - 1, 0))
    return p + mu_prev * (prev - p) + mu_next * (nxt - p)


def _rw_prep_body(m_ref, mp_ref, mn_ref, l_ref, lp_ref, ln_ref, mum_ref, mul_ref, w0_ref, w2_ref,
                  a0_ref, a2_ref, g2_ref, kk_ref, ka_ref, rk_ref, e_ref, et_ref,
                  r_o, k_o, v_o, kn_o, ab_o, lw0_o, lw1_o, g_o, bo_o):
    i = pl.program_id(1)
    first = i == 0
    last = i == pl.num_programs(1) - 1
    zero_m = jnp.zeros((1, m_ref.shape[2]), F32)
    zero_l = jnp.zeros((1, l_ref.shape[2]), F32)
    pm = _shift_rows(m_ref[0],
                     jnp.where(first, zero_m, mp_ref[0, SUBLANES - 1:SUBLANES, :]),
                     jnp.where(last, zero_m, mn_ref[0, 0:1, :]),
                     mum_ref[0:1, :], mum_ref[1:2, :])
    plo = _shift_rows(l_ref[0],
                      jnp.where(first, zero_l, lp_ref[0, SUBLANES - 1:SUBLANES, :]),
                      jnp.where(last, zero_l, ln_ref[0, 0:1, :]),
                      mul_ref[0:1, :], mul_ref[1:2, :])
    r = pm[:, 0:RW_W]
    k = pm[:, RW_W:2 * RW_W]
    v = pm[:, 2 * RW_W:3 * RW_W]
    wd = plo[:, 0:128]
    ad = plo[:, 128:256]
    gd = plo[:, 256:512]

    a = _sigmoid(a0_ref[...] + _dot1(ad, a2_ref[...]))
    g = _dot1(_sigmoid(gd), g2_ref[...])
    e = e_ref[...]
    et = et_ref[...]
    kk = k * kk_ref[...]
    ss = _dot1(_dot1(kk * kk, e), et)
    kn = kk / jnp.maximum(jnp.sqrt(ss), 1e-12)
    kmod = k * (1.0 + (a - 1.0) * ka_ref[...])
    wl = jnp.tanh(wd)
    c = math.exp(-0.5)
    lw0 = -_sigmoid(w0_ref[0:1, :] + _dot1(wl, w2_ref[0])) * c
    lw1 = -_sigmoid(w0_ref[1:2, :] + _dot1(wl, w2_ref[1])) * c
    rks = _dot1(_dot1(r * kmod * rk_ref[...], e), et)
    r_o[0] = r.astype(r_o.dtype)
    k_o[0] = kmod.astype(k_o.dtype)
    v_o[0] = v.astype(v_o.dtype)
    kn_o[0] = kn.astype(kn_o.dtype)
    ab_o[0] = (kn * a).astype(ab_o.dtype)
    lw0_o[0] = lw0
    lw1_o[0] = lw1
    g_o[0] = g.astype(g_o.dtype)
    bo_o[0] = (rks * v).astype(bo_o.dtype)


def _rw_prep(p_f, mu_main, mu_lora, w0, w2p, a0, a2p, g2p, k_k, k_a, r_k, e, et):
    b, t, _ = p_f.shape
    cm, cl = 3 * RW_W, PF_LORA_W
    col = {cm: 0, cl: PF_LORA_OFF // PF_LORA_W}
    tt = min(256, t)
    nb = tt // SUBLANES
    nt8 = t // SUBLANES

    def main_spec(c):
        return pl.BlockSpec((1, tt, c), lambda bi, i: (bi, i, col[c]))

    def prev_spec(c):
        return pl.BlockSpec((1, SUBLANES, c), lambda bi, i: (bi, jnp.maximum(i * nb - 1, 0), col[c]))

    def next_spec(c):
        return pl.BlockSpec((1, SUBLANES, c), lambda bi, i: (bi, jnp.minimum((i + 1) * nb, nt8 - 1), col[c]))

    def full(x):
        nd = x.ndim
        return pl.BlockSpec(x.shape, lambda bi, i: (0,) * nd)

    consts = (mu_main, mu_lora, w0, w2p, a0, a2p, g2p, k_k, k_a, r_k, e, et)
    out_dtypes = [BF16] * 5 + [F32, F32, BF16, BF16]
    return pl.pallas_call(
        _rw_prep_body,
        grid=(b, t // tt),
        in_specs=[main_spec(cm), prev_spec(cm), next_spec(cm), main_spec(cl), prev_spec(cl), next_spec(cl)]
        + [full(x) for x in consts],
        out_specs=[pl.BlockSpec((1, tt, RW_W), lambda bi, i: (bi, i, 0))] * 9,
        out_shape=[jax.ShapeDtypeStruct((b, t, RW_W), dt) for dt in out_dtypes],
        compiler_params=_params(("parallel", "parallel"), 56),
    )(p_f, p_f, p_f, p_f, p_f, p_f, *consts)


def _rw_scan_body(r_ref, k_ref, v_ref, kn_ref, ab_ref, lw_ref, y_ref, s_ref, *, rev):
    L = RW_CHUNK

    @pl.when(pl.program_id(1) == 0)
    def _():
        s_ref[...] = jnp.zeros_like(s_ref)

    row = lax.broadcasted_iota(jnp.int32, (L, L), 0)
    col = lax.broadcasted_iota(jnp.int32, (L, L), 1)
    if rev:
        incl = col >= row
        strict = col > row
    else:
        incl = col <= row
        strict = col < row
    tri = jnp.where(incl, 1.0, 0.0).astype(BF16)
    eye = jnp.where(row == col, 1.0, 0.0).astype(F32)
    lane = lax.broadcasted_iota(jnp.int32, (L, LANES), 1)
    head_masks = (lane < RW_HEAD, lane >= RW_HEAD)
    sr = lax.broadcasted_iota(jnp.int32, (LANES, LANES), 0)
    sc = lax.broadcasted_iota(jnp.int32, (LANES, LANES), 1)
    bd = (sr < RW_HEAD) == (sc < RW_HEAD)
    r4 = lax.broadcasted_iota(jnp.int32, (4 * L, 2 * L), 0)
    c4 = lax.broadcasted_iota(jnp.int32, (4 * L, 2 * L), 1) & (L - 1)
    t4 = r4 & (L - 1)
    own = jnp.where((r4 & L) != 0, 1, 0)
    causal4 = (c4 > t4 - own) if rev else (c4 < t4 + own)

    heads =[(hp, h) for hp in range(RW_HEADS // 2) for h in range(2)]
    sls = [slice(hp * LANES, (hp + 1) * LANES) for hp in range(RW_HEADS // 2)]

    def intra(offs):
        pre = []
        for off in offs:
            rows = slice(off, off + L)
            lw = lw_ref[0, rows, :]
            cum = _dot_sel(tri, lw)
            tot = cum[0:1, :] if rev else cum[L - 1:L, :]
            p_inv = jnp.exp(-cum)
            p_end = jnp.exp(tot - cum)
            ab = ab_ref[0, rows, :].astype(F32)
            k = k_ref[0, rows, :].astype(F32)
            pre.append(dict(at=-kn_ref[0, rows, :].astype(F32) * jnp.exp(cum - lw),
                            rt=r_ref[0, rows, :].astype(F32) * jnp.exp(cum), bt=ab * p_inv, kt=k * p_inv,
                            v=v_ref[0, rows, :].astype(F32), bt_end=ab * p_end, kt_end=k * p_end,
                            p_all=jnp.exp(tot)))
        zeros_lv = jnp.zeros((L, LANES), F32)
        nck = len(offs)
        at_h = [[] for _ in range(nck)]
        blks = [[] for _ in range(nck)]
        for ci, c in enumerate(pre):
            for hp in range(RW_HEADS // 2):
                sl = sls[hp]
                rows4 = []
                for h in range(2):
                    at_m = jnp.where(head_masks[h], c["at"][:, sl], 0.0)
                    at_h[ci].append(at_m)
                    rows4 += [at_m, jnp.where(head_masks[h], c["rt"][:, sl], 0.0)]
                blk = _dotp(jnp.concatenate(rows4, axis=0), jnp.concatenate([c["bt"][:, sl], c["kt"][:, sl]], axis=0),
                            NT, RW_PASSES_A)
                blks[ci].append(jnp.where(causal4, blk, 0.0))
        a_ab = [[blks[ci][hp][2 * h * L:(2 * h + 1) * L, :L] for hp, h in heads] for ci in range(nck)]
        a_rb = [[blks[ci][hp][(2 * h + 1) * L:(2 * h + 2) * L, :L] for hp, h in heads] for ci in range(nck)]
        av_pair = [[_dotp(blks[ci][hp], jnp.concatenate([zeros_lv, pre[ci]["v"][:, sls[hp]]], axis=0), NN, RW_PASSES_A)
                    for hp in range(RW_HEADS // 2)] for ci in range(nck)]
        av = [[av_pair[ci][hp][2 * h * L:(2 * h + 2) * L] for hp, h in heads] for ci in range(nck)]
        tm = [[eye + a for a in a_ab[ci]] for ci in range(nck)]
        pw = [[_dotp(a, a, NN, RW_PASSES_INV) for a in a_ab[ci]] for ci in range(nck)]
        for _ in range(4):
            st = [[_dotp(jnp.concatenate([t_, p], axis=0), p, NN, RW_PASSES_INV) for t_, p in zip(tm[ci], pw[ci])]
                  for ci in range(nck)]
            tm = [[t_ + s_[:L] for t_, s_ in zip(tm[ci], st[ci])] for ci in range(nck)]
            pw = [[s_[L:] for s_ in st[ci]] for ci in range(nck)]
        tm = [[t_ + _dotp(t_, p, NN, RW_PASSES_INV) for t_, p in zip(tm[ci], pw[ci])] for ci in range(nck)]
        out = []
        for ci, c in enumerate(pre):
            tx = [_dotp(tm[ci][i], jnp.concatenate([at_h[ci][i], jnp.where(head_masks[h], av[ci][i][:L], 0.0)], axis=1),
                        NN, RW_PASSES_A) for i, (hp, h) in enumerate(heads)]
            bk_t = [jnp.concatenate([c["bt_end"][:, sl], c["kt_end"][:, sl]], axis=0).T for sl in sls]
            p_col = [jnp.sum(jnp.where(sr == sc, jnp.broadcast_to(c["p_all"][:, sl], (LANES, LANES)), 0.0), axis=1,
                             keepdims=True) for sl in sls]
            out.append(dict(tx=tx, av=av[ci], a_rb=a_rb[ci], rt=c["rt"], v=c["v"], bk_t=bk_t, p_col=p_col))
        return out

    def sequential(c, off, states):
        tx, av, a_rb = c["tx"], c["av"], c["a_rb"]
        out = []
        for hp in range(RW_HEADS // 2):
            sl = sls[hp]
            i0, i1 = 2 * hp, 2 * hp + 1
            w_pair = tx[i0][:, :LANES] + tx[i1][:, :LANES]
            u0_pair = tx[i0][:, LANES:] + tx[i1][:, LANES:]
            y0_pair = jnp.where(head_masks[0], av[i0][L:], av[i1][L:])
            s = states[hp]
            ws = _dotp(jnp.concatenate([w_pair, c["rt"][:, sl]], axis=0), s, NN, RW_PASSES_S)
            u = ws[:L] + u0_pair
            y = ws[L:] + y0_pair + jnp.where(head_masks[0], _dotp(a_rb[i0], u, NN, RW_PASSES_A),
                                             _dotp(a_rb[i1], u, NN, RW_PASSES_A))
            upd = _dotp(c["bk_t"][hp], jnp.concatenate([u, c["v"][:, sl]], axis=0), NN, RW_PASSES_S)
            out.append(s * c["p_col"][hp] + jnp.where(bd, upd, 0.0))
            y_ref[0, off:off + L, sl] = y
        return out

    offs = [c * L for c in range(RW_SUB)]
    if rev:
        offs = offs[::-1]
    chunks = intra(offs)
    states = [s_ref[hp] for hp in range(RW_HEADS // 2)]
    for c, off in zip(chunks, offs):
        states = sequential(c, off, states)
    for hp in range(RW_HEADS // 2):
        s_ref[hp] = states[hp]


def _rw_scan(r, k, v, kn, ab, lw, rev):
    b, t, w = r.shape
    tb = RW_CHUNK * RW_SUB
    nc = t // tb
    if rev:
        idx = lambda bi, c: (bi, nc - 1 - c, 0)
    else:
        idx = lambda bi, c: (bi, c, 0)
    spec = pl.BlockSpec((1, tb, w), idx)
    return pl.pallas_call(
        functools.partial(_rw_scan_body, rev=rev),
        grid=(b, nc),
        in_specs=[spec] * 6,
        out_specs=spec,
        out_shape=jax.ShapeDtypeStruct((b, t, w), F32),
        scratch_shapes=[pltpu.VMEM((RW_HEADS // 2, LANES, LANES), F32)],
        compiler_params=_params(("parallel", "arbitrary"), 32),
    )(r, k, v, kn, ab, lw)


def _ml_scan_body(q_ref, k_ref, v_ref, g_ref, b_ref, h_ref, c_ref, n_ref, m_ref, *, rev, direction):
    L = ML_CHUNK

    @pl.when(pl.program_id(1) == 0)
    def _():
        c_ref[...] = jnp.zeros_like(c_ref)
        n_ref[...] = jnp.zeros_like(n_ref)
        m_ref[...] = jnp.zeros_like(m_ref)

    row = lax.broadcasted_iota(jnp.int32, (L, L), 0)
    col = lax.broadcasted_iota(jnp.int32, (L, L), 1)
    incl = (col >= row) if rev else (col <= row)
    tri = jnp.where(incl, 1.0, 0.0).astype(BF16)
    lane = lax.broadcasted_iota(jnp.int32, (L, LANES), 1)
    sub = lax.broadcasted_iota(jnp.int32, (LANES, L), 0)

    g = g_ref[0] + b_ref[...]
    ls = jnp.minimum(g, 0.0) - jnp.log(1.0 + jnp.exp(-jnp.abs(g)))
    bc = _dot_sel(tri, ls)
    g_t = g.T
    bc_t = bc.T
    scale = ML_QK ** -0.5
    hs = range(ML_HEADS)
    ci = [direction * ML_HEADS + h for h in hs]
    cf = [2 * ML_HEADS + direction * ML_HEADS + h for h in hs]
    b_col = [jnp.sum(jnp.where(lane == cf[h], bc, 0.0), axis=1, keepdims=True) for h in hs]
    i_col = [jnp.sum(jnp.where(lane == ci[h], g, 0.0), axis=1, keepdims=True) for h in hs]
    b_row = [jnp.sum(jnp.where(sub == cf[h], bc_t, 0.0), axis=0, keepdims=True) for h in hs]
    i_row = [jnp.sum(jnp.where(sub == ci[h], g_t, 0.0), axis=0, keepdims=True) for h in hs]
    m_prev = [jnp.max(m_ref[h:h + 1, :], axis=1, keepdims=True) for h in hs]
    qs = [q_ref[0, :, h * ML_QK:(h + 1) * ML_QK].astype(F32) * scale for h in hs]
    kc = [k_ref[0, :, h * ML_QK:(h + 1) * ML_QK].astype(F32) for h in hs]
    vc = [v_ref[0, :, h * ML_V:(h + 1) * ML_V].astype(F32) for h in hs]
    qk = [_dot1(qs[h], kc[h], NT) for h in hs]
    qc = [_dot1(qs[h], c_ref[h]) for h in hs]
    dmat = [jnp.where(incl, b_col[h] - b_row[h] + i_row[h], -jnp.inf) for h in hs]
    inter = [b_col[h] + m_prev[h] for h in hs]
    m_t = [jnp.maximum(inter[h], jnp.max(dmat[h], axis=1, keepdims=True)) for h in hs]
    s = [qk[h] * jnp.exp(dmat[h] - m_t[h]) for h in hs]
    e_inter = [jnp.exp(inter[h] - m_t[h]) for h in hs]
    num = [_dot1(s[h], vc[h]) + e_inter[h] * qc[h] for h in hs]
    den = [jnp.sum(s[h], axis=1, keepdims=True)
           + e_inter[h] * jnp.sum(qs[h] * n_ref[h:h + 1, :], axis=1, keepdims=True) for h in hs]
    for h in hs:
        h_ref[0, :, h * ML_V:(h + 1) * ML_V] = num[h] / jnp.maximum(jnp.abs(den[h]), jnp.exp(-m_t[h]))
    b_last = [jnp.min(b_row[h], axis=1, keepdims=True) for h in hs]
    g_row = [b_last[h] - b_row[h] + i_row[h] for h in hs]
    g_col = [b_last[h] - b_col[h] + i_col[h] for h in hs]
    m_new = [jnp.maximum(b_last[h] + m_prev[h], jnp.max(g_row[h], axis=1, keepdims=True)) for h in hs]
    w_col = [jnp.exp(g_col[h] - m_new[h]) for h in hs]
    dec = [jnp.exp(b_last[h] + m_prev[h] - m_new[h]) for h in hs]
    upd = [_dot1(kc[h].T, vc[h] * w_col[h]) for h in hs]
    for h in hs:
        c_ref[h] = dec[h] * c_ref[h] + upd[h]
        n_ref[h:h + 1, :] = dec[h] * n_ref[h:h + 1, :] + jnp.sum(kc[h] * w_col[h], axis=0, keepdims=True)
        m_ref[h:h + 1, :] = jnp.broadcast_to(m_new[h], (1, LANES))


def _ml_scan(p_b, p_f, bias, rev):
    b, t, _ = p_b.shape
    L = ML_CHUNK
    nc = t // L
    tix = (lambda c: nc - 1 - c) if rev else (lambda c: c)
    return pl.pallas_call(
        functools.partial(_ml_scan_body, rev=rev, direction=1 if rev else 0),
        grid=(b, nc),
        in_specs=[pl.BlockSpec((1, L, ML_QKW), lambda bi, c: (bi, tix(c), 0)),
                  pl.BlockSpec((1, L, ML_QKW), lambda bi, c: (bi, tix(c), 1)),
                  pl.BlockSpec((1, L, ML_VW), lambda bi, c: (bi, tix(c), 1)),
                  pl.BlockSpec((1, L, LANES), lambda bi, c: (bi, tix(c), PF_MLG_OFF // LANES)),
                  pl.BlockSpec((1, LANES), lambda bi, c: (0, 0))],
        out_specs=pl.BlockSpec((1, L, ML_VW), lambda bi, c: (bi, tix(c), 0)),
        out_shape=jax.ShapeDtypeStruct((b, t, ML_VW), F32),
        scratch_shapes=[pltpu.VMEM((ML_HEADS, ML_QK, ML_V), F32),
                        pltpu.VMEM((SUBLANES, LANES), F32),
                        pltpu.VMEM((SUBLANES, LANES), F32)],
        compiler_params=_params(("parallel", "arbitrary"), 32),
    )(p_b, p_b, p_b, p_f, bias)


def _post_body(yf_ref, yb_ref, bo_ref, g_ref, lnw_ref, lnb_ref, e_ref, et_ref,
               hf_ref, hb_ref, o_ref, nw_ref, ya_ref, yb_o_ref):
    e = e_ref[...]
    et = et_ref[...]
    y = yf_ref[...] + yb_ref[...]
    inv = 1.0 / RW_HEAD
    mu = _dot1(_dot1(y, e), et) * inv
    yc = y - mu
    var = _dot1(_dot1(yc * yc, e), et) * inv
    yn = yc * lax.rsqrt(var + RW_GN_EPS) * lnw_ref[...] + lnb_ref[...]
    ya_ref[...] = ((yn + bo_ref[...].astype(F32)) * g_ref[...].astype(F32)).astype(ya_ref.dtype)

    hsum = hf_ref[...] + hb_ref[...]
    parts = []
    for h in range(ML_HEADS):
        hh = hsum[:, h * ML_V:(h + 1) * ML_V]
        parts.append(hh * lax.rsqrt(jnp.mean(hh * hh, axis=-1, keepdims=True) + ML_NORM_EPS))
    hn = jnp.concatenate(parts, axis=1) * nw_ref[...]
    yb_o_ref[...] = (hn * _sigmoid(o_ref[...].astype(F32))).astype(yb_o_ref.dtype)


def _post(yf, yb, bonus, g, ln_w, ln_b, e, et, hf, hb, vo, norm_w):
    n = yf.shape[0]
    tm = min(256, n)
    tok = lambda c: pl.BlockSpec((tm, c), lambda i: (i, 0))
    const = lambda x: pl.BlockSpec(x.shape, lambda i: (0, 0))
    return pl.pallas_call(
        _post_body,
        grid=(n // tm,),
        in_specs=[tok(RW_W), tok(RW_W), tok(RW_W), tok(RW_W), const(ln_w), const(ln_b), const(e), const(et),
                  tok(ML_VW), tok(ML_VW), pl.BlockSpec((tm, ML_VW), lambda i: (i, (2 * ML_QKW + ML_VW) // ML_VW)),
                  const(norm_w)],
        out_specs=[tok(RW_W), tok(ML_VW)],
        out_shape=[jax.ShapeDtypeStruct((n, RW_W), BF16), jax.ShapeDtypeStruct((n, ML_VW), BF16)],
        compiler_params=_params(("parallel",), 40),
    )(yf, yb, bonus, g, ln_w, ln_b, e, et, hf, hb, vo, norm_w)


def _merge_body(ya_ref, yb_ref, pa_ref, pb_ref, ga_ref, gb_ref, ba_ref, bb_ref, o_ref):
    pa = _mm(ya_ref[...], pa_ref[...])
    pb = _mm(yb_ref[...], pb_ref[...])
    ga = ga_ref[...].astype(F32) + ba_ref[...]
    gb = gb_ref[...].astype(F32) + bb_ref[...]
    o_ref[...] = (_sigmoid(ga) * pa + _sigmoid(gb) * pb).astype(o_ref.dtype)


def _merge(ya, yb, p_a, p_b, proj_b, b_gate):
    n = ya.shape[0]
    tm = min(1024, n)
    tn = 1024
    nj = D_MODEL // tn
    g0 = PB_GATE_OFF // tn
    return pl.pallas_call(
        _merge_body,
        grid=(n // tm, nj),
        in_specs=[pl.BlockSpec((tm, RW_W), lambda i, j: (i, 0)), pl.BlockSpec((tm, ML_VW), lambda i, j: (i, 0)),
                  pl.BlockSpec((RW_W, tn), lambda i, j: (0, j)), pl.BlockSpec((ML_VW, tn), lambda i, j: (0, j)),
                  pl.BlockSpec((tm, tn), lambda i, j: (i, g0 + j)), pl.BlockSpec((tm, tn), lambda i, j: (i, g0 + nj + j)),
                  pl.BlockSpec((1, tn), lambda i, j: (0, j)), pl.BlockSpec((1, tn), lambda i, j: (0, j + nj))],
        out_specs=pl.BlockSpec((tm, tn), lambda i, j: (i, j)),
        out_shape=jax.ShapeDtypeStruct((n, D_MODEL), BF16),
        compiler_params=_params(("parallel", "parallel"), 40),
    )(ya, yb, p_a, p_b, proj_b, proj_b, b_gate, b_gate)


def _resid_mm_body(a_ref, w_ref, x_ref, o_ref):
    o_ref[...] = x_ref[...] + _mm(a_ref[...], w_ref[...])


def _resid_mm(a, w, x):
    n, kdim = a.shape
    c = w.shape[1]
    tm = min(1024, n)
    tn = 1024
    return pl.pallas_call(
        _resid_mm_body,
        grid=(n // tm, c // tn),
        in_specs=[pl.BlockSpec((tm, kdim), lambda i, j: (i, 0)), pl.BlockSpec((kdim, tn), lambda i, j: (0, j)),
                  pl.BlockSpec((tm, tn), lambda i, j: (i, j))],
        out_specs=pl.BlockSpec((tm, tn), lambda i, j: (i, j)),
        out_shape=jax.ShapeDtypeStruct((n, c), F32),
        compiler_params=_params(("parallel", "parallel"), 48),
    )(a, w, x)


def _topk_rows(s, k, payload=None):
    nrow, t = s.shape
    rid = lax.broadcasted_iota(jnp.int32, (nrow, t), 0).astype(F32)
    kid = lax.broadcasted_iota(jnp.int32, (k, t), 0)
    vals = jnp.zeros((k, t), F32)
    sel = jnp.zeros((k, t), F32)
    for j in range(k):
        m = jnp.max(s, axis=0, keepdims=True)
        pos = jnp.min(jnp.where(s == m, rid, float(nrow)), axis=0, keepdims=True)
        hit = rid == pos
        if payload is None:
            picked = pos
        else:
            picked = jnp.max(jnp.where(hit, payload, -1.0), axis=0, keepdims=True)
        vals = jnp.where(kid == j, m, vals)
        sel = jnp.where(kid == j, picked, sel)
        s = jnp.where(hit, -jnp.inf, s)
    return vals, sel


def _router_body(h_ref, wq_ref, keys_ref, a_ref, b_ref, gw_ref):
    K = PEER_TOPK
    hb = h_ref[...]
    qs = [_mm(hb, wq_ref[:, h * 2 * PEER_HALF:(h + 1) * 2 * PEER_HALF]) for h in range(PEER_HEADS)]
    a_parts, b_parts, w_parts = [], [], []
    for h in range(PEER_HEADS):
        sv, si = [], []
        for p in range(2):
            st = _dot3(keys_ref[h * 2 + p], qs[h][:, p * PEER_HALF:(p + 1) * PEER_HALF], NT)
            vals, idx = _topk_rows(st, K)
            sv.append(vals)
            si.append(idx)
        jid = lax.broadcasted_iota(jnp.int32, (SUBLANES, sv[0].shape[1]), 0)
        c_parts = [sv[0][0:1, :] + sv[1]]
        i_parts = [si[0][0:1, :] * float(PEER_NKEYS) + si[1]]
        for i in range(1, SUBLANES):
            keep = jid < K // (i + 1)
            c_parts.append(jnp.where(keep, sv[0][i:i + 1, :] + sv[1][0:SUBLANES, :], -jnp.inf))
            i_parts.append(si[0][i:i + 1, :] * float(PEER_NKEYS) + si[1][0:SUBLANES, :])
        c_parts.append(sv[0][SUBLANES:K, :] + sv[1][0:1, :])
        i_parts.append(si[0][SUBLANES:K, :] * float(PEER_NKEYS) + si[1][0:1, :])
        cand = jnp.concatenate(c_parts, axis=0)
        cidx = jnp.concatenate(i_parts, axis=0)
        best, eidx = _topk_rows(cand, K, payload=cidx)
        ex = jnp.exp(best - best[0:1, :])
        w_parts.append(ex / jnp.sum(ex, axis=0, keepdims=True))
        hi = jnp.floor(eidx * (1.0 / PEER_NKEYS))
        a_parts.append(hi)
        b_parts.append(eidx - hi * float(PEER_NKEYS))
    a_ref[...] = jnp.concatenate(a_parts, axis=0).T
    b_ref[...] = jnp.concatenate(b_parts, axis=0).T
    gw_ref[...] = jnp.concatenate(w_parts, axis=0).T


def _router(h, wq, keys):
    n, d = h.shape
    tq = min(256, n)
    tok = pl.BlockSpec((tq, PEER_HK), lambda i: (i, 0))
    sds = jax.ShapeDtypeStruct((n, PEER_HK), F32)
    return pl.pallas_call(
        _router_body,
        grid=(n // tq,),
        in_specs=[pl.BlockSpec((tq, d), lambda i: (i, 0)),
                  pl.BlockSpec(wq.shape, lambda i: (0, 0), pipeline_mode=pl.Buffered(1)),
                  pl.BlockSpec(keys.shape, lambda i: (0, 0, 0))],
        out_specs=[tok, tok, tok],
        out_shape=[sds, sds, sds],
        compiler_params=_params(("parallel",), 32),
    )(h, wq, keys)


PEER_CI = 8
PEER_CE = PEER_CI * PEER_NKEYS


PEER_SPLIT = 1
PEER_KEYS_PER_CALL = PEER_NKEYS // PEER_SPLIT
HI16 = 0xFFFF0000


def _peer_body(x_ref, h_ref, a_ref, b_ref, w_ref, ut_ref, v_ref, gf_ref, o_ref, g2_ref, z_ref, *,
               pitch, key0, final_norm):
    j = pl.program_id(1)
    last = pl.num_programs(1) - 1
    t = x_ref.shape[0]
    half = t // 2
    cur = j % 2

    def produce():
        z_ref[cur] = _mm(h_ref[...], ut_ref[...])

    def consume():
        z = z_ref[1 - cur]
        parts = []
        for ii in range(PEER_CI):
            start = pl.multiple_of(((j - 1) * PEER_CI + ii) * pitch, SUBLANES)
            words = g2_ref[pl.ds(start, half), :]
            g_first = lax.bitcast_convert_type(words << 16, F32)
            g_second = lax.bitcast_convert_type(words & jnp.uint32(HI16), F32)
            zz = z[:, ii * PEER_NKEYS:(ii + 1) * PEER_NKEYS]
            act = 0.5 * zz * (1.0 + lax.erf(zz * (2.0 ** -0.5)))
            parts.append(jnp.concatenate([act[:half] * g_first, act[half:] * g_second], axis=0).astype(BF16))
        o_ref[...] += _mm(jnp.concatenate(parts, axis=1), v_ref[...])

    @pl.when(j == 0)
    def _():
        o_ref[...] = x_ref[...]
        kid = lax.broadcasted_iota(jnp.int32, (PEER_KEYS_PER_CALL, PEER_HK), 0).astype(F32) + float(key0)
        kid_b = lax.broadcasted_iota(jnp.int32, (PEER_NKEYS, PEER_HK), 0).astype(F32)

        def g_bits(ti):
            arow = a_ref[pl.ds(ti, 1), :]
            brow = b_ref[pl.ds(ti, 1), :]
            wrow = w_ref[pl.ds(ti, 1), :]
            lhs = jnp.where(kid == arow, wrow, 0.0).astype(BF16)
            rhs = jnp.where(kid_b == brow, 1.0, 0.0).astype(BF16)
            g_t = _mm(lhs, rhs, NT)
            return lax.bitcast_convert_type(g_t.astype(BF16).astype(F32), jnp.uint32)

        def per_pair(ti, carry):
            words = (g_bits(ti) >> 16) | g_bits(ti + half)
            g2_ref[pl.ds(ti, PEER_KEYS_PER_CALL, stride=pitch), :] = words
            return carry

        lax.fori_loop(0, half, per_pair, 0, unroll=32)
        produce()

    @pl.when(jnp.logical_and(j > 0, j < last))
    def _():
        produce()
        consume()

    @pl.when(j == last)
    def _():
        consume()

    if final_norm:
        @pl.when(j == last)
        def _():
            y = o_ref[...]
            o_ref[...] = y * lax.rsqrt(jnp.mean(y * y, axis=-1, keepdims=True) + RMS_EPS) * gf_ref[...]


def _peer(x, h, a_idx, b_idx, gw, u_t, v_tab, g_final, final_norm):
    n, d = x.shape
    t = min(512, n)
    pitch = t // 2 + SUBLANES
    nch = PEER_N // PEER_CE // PEER_SPLIT
    once = pl.Buffered(1)
    tok = lambda c: pl.BlockSpec((t, c), lambda i, j: (i, 0), pipeline_mode=once)
    for part in range(PEER_SPLIT):
        c0 = part * nch
        is_last = part == PEER_SPLIT - 1
        x = pl.pallas_call(
            functools.partial(_peer_body, pitch=pitch, key0=part * PEER_KEYS_PER_CALL,
                              final_norm=final_norm and is_last),
            grid=(n // t, nch + 1),
            in_specs=[tok(d), tok(d), tok(PEER_HK), tok(PEER_HK), tok(PEER_HK),
                      pl.BlockSpec((d, PEER_CE), lambda i, j, c0=c0: (0, c0 + jnp.minimum(j, nch - 1))),
                      pl.BlockSpec((PEER_CE, d), lambda i, j, c0=c0: (c0 + jnp.maximum(j - 1, 0), 0)),
                      pl.BlockSpec((1, d), lambda i, j: (0, 0))],
            out_specs=pl.BlockSpec((t, d), lambda i, j: (i, 0)),
            out_shape=jax.ShapeDtypeStruct((n, d), F32),
            scratch_shapes=[pltpu.VMEM((PEER_KEYS_PER_CALL * pitch, LANES), jnp.uint32),
                            pltpu.VMEM((2, t, PEER_CE), F32)],
            compiler_params=_params(("parallel", "arbitrary"), 56),
        )(x, h, a_idx, b_idx, gw, u_t, v_tab, g_final)
    return x


def _pad_cols(pieces, width):
    rows = pieces[0][0].shape[0]
    out = jnp.zeros((rows, width), pieces[0][0].dtype)
    for arr, off in pieces:
        out = out.at[:, off:off + arr.shape[1]].set(arr)
    return out


def _pad_rows(x, rows):
    return jnp.zeros((rows,) + x.shape[1:], x.dtype).at[:x.shape[0]].set(x)


def _lora_layout(x):
    o = 3 * RW_W
    return _pad_cols([(x[:, o:o + 64], 0), (x[:, o + 64:o + 128], 128), (x[:, o + 128:o + 288], 256)], 512)


def _prep_layer(l, w):
    f = {}
    w_in = w["w_in"][l]
    rw = w_in[:, :RW_COLS]
    ml = w_in[:, RW_COLS:RW_COLS + ML_COLS]
    gt = w_in[:, RW_COLS + ML_COLS:]
    f["norm_mix"] = w["norm_mix"][l][None, :]
    f["w_f"] = _pad_cols([(rw[:, :3 * RW_W], 0), (_lora_layout(rw), PF_LORA_OFF),
                          (ml[:, 2 * ML_QKW + 2 * ML_VW:], PF_MLG_OFF)], PF_COLS).astype(BF16)
    f["w_b"] = jnp.concatenate([ml[:, :2 * ML_QKW + 2 * ML_VW], gt], axis=1).astype(BF16)
    mu = jnp.stack([w["rw_mu_prev"][l], w["rw_mu_next"][l]])
    f["mu_main"] = mu[:, :3 * RW_W]
    f["mu_lora"] = _lora_layout(mu)
    f["w0"] = w["rw_w0"][l]
    f["w2p"] = jnp.stack([_pad_rows(w["rw_w2"][l, 0], 128), _pad_rows(w["rw_w2"][l, 1], 128)]).astype(BF16)
    f["a0"] = w["rw_a0"][l][None, :]
    f["a2p"] = _pad_rows(w["rw_a2"][l], 128).astype(BF16)
    f["g2p"] = _pad_rows(w["rw_g2"][l], 256).astype(BF16)
    f["k_k"] = w["rw_k_k"][l][None, :]
    f["k_a"] = w["rw_k_a"][l][None, :]
    f["r_k"] = w["rw_r_k"][l].reshape(1, RW_W)
    f["ln_w"] = w["rw_ln_w"][l][None, :]
    f["ln_b"] = w["rw_ln_b"][l][None, :]
    f["ml_bias"] = _pad_cols([(w["ml_b_i"][l].reshape(1, -1), 0), (w["ml_b_f"][l].reshape(1, -1), 2 * ML_HEADS)], LANES)
    f["ml_norm_w"] = w["ml_norm_w"][l][None, :]
    f["p_a"] = w["p_a"][l].astype(BF16)
    f["p_b"] = w["p_b"][l].astype(BF16)
    f["b_gate"] = w["b_gate"][l][None, :]
    f["w_out"] = w["w_out"][l].astype(BF16)
    f["norm_ffn"] = w["norm_ffn"][l][None, :]
    f["wq"] = w["peer_wq"][l].astype(BF16)
    f["keys"] = w["peer_keys"][l].reshape(PEER_HEADS * 2, PEER_NKEYS, PEER_HALF)
    f["u_t"] = w["peer_u"][l].astype(BF16).T
    f["v"] = w["peer_v"][l].astype(BF16)
    return f


def _head_selectors():
    ch = jnp.arange(RW_W)[:, None] // RW_HEAD
    e = (ch == jnp.arange(LANES)[None, :]).astype(BF16)
    return e, e.T


def _layer(x, f, e, et, g_final, final_norm):
    b, t, d = x.shape
    n = b * t
    x2 = x.reshape(n, d)
    (hn,) = _norm_cast(x2, f["norm_mix"], 1)
    p_f = _matmul((hn,), (f["w_f"],), F32)
    p_b = _matmul((hn,), (f["w_b"],), BF16)
    p_f3 = p_f.reshape(b, t, -1)
    p_b3 = p_b.reshape(b, t, -1)

    r, k, v, kn, ab, lw0, lw1, gg, bonus = _rw_prep(
        p_f3, f["mu_main"], f["mu_lora"], f["w0"], f["w2p"], f["a0"], f["a2p"], f["g2p"], f["k_k"], f["k_a"],
        f["r_k"], e, et)
    y_f = _rw_scan(r, k, v, kn, ab, lw0, rev=False)
    y_b = _rw_scan(r, k, v, kn, ab, lw1, rev=True)

    h_f = _ml_scan(p_b3, p_f3, f["ml_bias"], rev=False)
    h_b = _ml_scan(p_b3, p_f3, f["ml_bias"], rev=True)

    ya, yb = _post(y_f.reshape(n, -1), y_b.reshape(n, -1), bonus.reshape(n, -1), gg.reshape(n, -1),
                   f["ln_w"], f["ln_b"], e, et, h_f.reshape(n, -1), h_b.reshape(n, -1), p_b, f["ml_norm_w"])
    merged = _merge(ya, yb, f["p_a"], f["p_b"], p_b, f["b_gate"])
    x1 = _resid_mm(merged, f["w_out"], x2)

    h_parts = _norm_cast(x1, f["norm_ffn"], 1)
    a_idx, b_idx, gw = _router(h_parts[0], f["wq"], f["keys"])
    x_out = _peer(x1, h_parts[0], a_idx, b_idx, gw, f["u_t"], f["v"], g_final, final_norm)
    return x_out.reshape(b, t, d)


def _trunk(x, layers, e, et, g_final):
    for l, f in enumerate(layers):
        x = _layer(x, f, e, et, g_final, final_norm=(l == len(layers) - 1))
    return x


def kernel(x_prompt, x_sample, norm_mix, w_in, rw_mu_prev, rw_mu_next, rw_w0, rw_w2, rw_a0, rw_a2, rw_g2, rw_k_k, rw_k_a, rw_r_k, rw_ln_w, rw_ln_b, ml_b_i, ml_b_f, ml_norm_w, p_a, p_b, b_gate, w_out, norm_ffn, peer_wq, peer_keys, peer_u, peer_v, norm_final):
    w = dict(norm_mix=norm_mix, w_in=w_in, rw_mu_prev=rw_mu_prev, rw_mu_next=rw_mu_next, rw_w0=rw_w0, rw_w2=rw_w2,
             rw_a0=rw_a0, rw_a2=rw_a2, rw_g2=rw_g2, rw_k_k=rw_k_k, rw_k_a=rw_k_a, rw_r_k=rw_r_k, rw_ln_w=rw_ln_w,
             rw_ln_b=rw_ln_b, ml_b_i=ml_b_i, ml_b_f=ml_b_f, ml_norm_w=ml_norm_w, p_a=p_a, p_b=p_b, b_gate=b_gate,
             w_out=w_out, norm_ffn=norm_ffn, peer_wq=peer_wq, peer_keys=peer_keys, peer_u=peer_u, peer_v=peer_v)
    depth = w_in.shape[0]
    layers = [_prep_layer(l, w) for l in range(depth)]
    e, et = _head_selectors()
    g_final = norm_final[None, :]
    return (_trunk(x_prompt, layers, e, et, g_final), _trunk(x_sample, layers, e, et, g_final))
```
